```python
import jax, jax.numpy as jnp
from jax import lax
import numpy as np

D_MODEL = 2048
BATCH = 8
SEQ = 2048
DEPTH = 1

CHUNK = 64
RWKV_WIDTH = 1024
RWKV_HEAD = 64
RWKV_HEADS = RWKV_WIDTH // RWKV_HEAD
DECAY_LORA = 64
AAA_LORA = 64
GATE_LORA = 160
RWKV_GN_EPS = 64e-5
HGRN_WIDTH = 1024
HGRN_EXPAND = 128
HGRN_HEADS = HGRN_WIDTH // HGRN_EXPAND
HGRN_DK = HGRN_EXPAND
HGRN_DV = HGRN_WIDTH // HGRN_HEADS
GLA_BLOCK = CHUNK // 4
N_EXPERTS = 32
TOP_K = 4
D_EXPERT = 2048
SWIGLU_ALPHA = 1.702
SWIGLU_LIMIT = 7.0
MOE_BLOCK = 128
NORM_EPS = 1e-5
RWKV_COLS = 3 * RWKV_WIDTH + DECAY_LORA + AAA_LORA + GATE_LORA
HGRN_COLS = 4 * HGRN_WIDTH
GATE_COLS = 2 * D_MODEL
IN_COLS = RWKV_COLS + HGRN_COLS + GATE_COLS

kernel_name = 'hybrid_rwkv7_hgrn2_moe_block'


def _rmsnorm(x, g, eps=NORM_EPS):
    xf = x.astype(jnp.float32)
    y = xf * lax.rsqrt(jnp.mean(xf * xf, axis=-1, keepdims=True) + eps)
    return (y * g.astype(jnp.float32)).astype(x.dtype)


def _token_shift(p, mu):
    prev = jnp.pad(p, ((0, 0), (1, 0), (0, 0)))[:, :-1, :]
    return p + (prev - p) * mu


def _rwkv7_mix(p, w0, w2, a0, a2, g2, k_k, k_a, r_k, gn_g, gn_b):
    B, S, _ = p.shape
    C, H, N = RWKV_WIDTH, RWKV_HEADS, RWKV_HEAD
    f32 = jnp.float32
    p = p.astype(f32)
    r, k, v, wd, ad, gd = jnp.split(p, [C, 2 * C, 3 * C, 3 * C + DECAY_LORA, 3 * C + DECAY_LORA + AAA_LORA], axis=-1)
    w_log = -jax.nn.softplus(-(w0 + jnp.tanh(wd) @ w2)) - 0.5
    decay = jnp.exp(-jnp.exp(w_log))
    a = jax.nn.sigmoid(a0 + ad @ a2)
    g = jax.nn.sigmoid(gd) @ g2
    heads = lambda t: t.reshape(B, S, H, N)
    kk = heads(k * k_k)
    kk = kk / jnp.maximum(jnp.sqrt(jnp.sum(kk * kk, axis=-1, keepdims=True)), 1e-12)
    k = k * (1.0 + (a - 1.0) * k_a)
    r, k, v, a, decay = heads(r), heads(k), heads(v), heads(a), heads(decay)
    tm = lambda t: jnp.moveaxis(t, 1, 0)

    def step(state, inp):
        r_t, w_t, k_t, v_t, kk_t, a_t = inp
        sa = jnp.einsum('bhvk,bhk->bhv', state, -kk_t)
        state = (state * w_t[:, :, None, :]
                 + sa[..., None] * (kk_t * a_t)[:, :, None, :]
                 + v_t[..., None] * k_t[:, :, None, :])
        return state, jnp.einsum('bhvk,bhk->bhv', state, r_t)

    _, y = lax.scan(step, jnp.zeros((B, H, N, N), f32),
                    (tm(r), tm(decay), tm(k), tm(v), tm(kk), tm(a)))
    y = jnp.moveaxis(y, 0, 1)
    mu = jnp.mean(y, axis=-1, keepdims=True)
    var = jnp.mean(jnp.square(y - mu), axis=-1, keepdims=True)
    y = ((y - mu) * lax.rsqrt(var + RWKV_GN_EPS)).reshape(B, S, C) * gn_g + gn_b
    bonus = jnp.sum(r * k * r_k, axis=-1, keepdims=True) * v
    return (y + bonus.reshape(B, S, C)) * g


def _hgrn2_mix(p, lb, gn_g):
    B, S, _ = p.shape
    H, DK, DV, L = HGRN_HEADS, HGRN_DK, HGRN_DV, GLA_BLOCK
    nC = S // L
    f32 = jnp.float32
    q, f, i, og = jnp.split(p.astype(f32), 4, axis=-1)
    q = jax.nn.silu(q)
    forget = lb + (1.0 - lb) * jax.nn.sigmoid(f)
    log_f = jnp.log(forget)
    k = 1.0 - forget
    blocks = lambda t, d: t.reshape(B, nC, L, H, d).transpose(0, 3, 1, 2, 4)
    q, k, log_f, v = blocks(q, DK), blocks(k, DK), blocks(log_f, DK), blocks(i, DV)
    b = jnp.cumsum(log_f, axis=3)
    q_dec = q * jnp.exp(b)
    k_dec = k * jnp.exp(-b)
    causal = jnp.tril(jnp.ones((L, L), dtype=bool))
    attn = jnp.where(causal, jnp.einsum('bhcld,bhcmd->bhclm', q_dec, k_dec), 0.0)
    o_intra = jnp.einsum('bhclm,bhcme->bhcle', attn, v)
    b_last = b[:, :, :, -1:, :]
    k_end = k * jnp.exp(b_last - b)
    decay_end = jnp.exp(b_last[:, :, :, 0, :])

    def step(state, inp):
        q_c, k_c, v_c, d_c = inp
        o = jnp.einsum('bhld,bhde->bhle', q_c, state)
        state = state * d_c[..., None] + jnp.einsum('bhld,bhle->bhde', k_c, v_c)
        return state, o

    cm = lambda t: jnp.moveaxis(t, 2, 0)
    _, o_inter = lax.scan(step, jnp.zeros((B, H, DK, DV), f32),
                          (cm(q_dec), cm(k_end), cm(v), cm(decay_end)))
    o = o_intra + jnp.moveaxis(o_inter, 0, 2)
    o = o.transpose(0, 2, 3, 1, 4).reshape(B, S, H, DV)
    o = o * lax.rsqrt(jnp.mean(o * o, axis=-1, keepdims=True) + NORM_EPS) * gn_g
    o = o * jax.nn.silu(og.reshape(B, S, H, DV))
    return o.reshape(B, S, HGRN_WIDTH)


def _token_mixer(h, w_in, mu, w0, w2, a0, a2, g2, k_k, k_a, r_k, rgn_g, rgn_b, lb, hgn_g, proj_a, proj_b, w_out):
    proj = h @ w_in
    p_rwkv, p_hgrn, p_gate = jnp.split(proj, [RWKV_COLS, RWKV_COLS + HGRN_COLS], axis=-1)
    y_a = _rwkv7_mix(_token_shift(p_rwkv, mu), w0, w2, a0, a2, g2, k_k, k_a, r_k, rgn_g, rgn_b)
    y_b = _hgrn2_mix(p_hgrn, lb, hgn_g)
    gate_a, gate_b = jnp.split(p_gate.astype(jnp.float32), 2, axis=-1)
    merged = (jax.nn.sigmoid(gate_a) * (y_a @ proj_a.astype(jnp.float32))
              + jax.nn.sigmoid(gate_b) * (y_b @ proj_b.astype(jnp.float32)))
    return (merged.astype(h.dtype) @ w_out)


def _moe(h, router_w, router_b, w_gate_up, b_gate_up, w_down, b_down):
    B, S, D = h.shape
    T = B * S
    A = T * TOP_K
    hf = h.reshape(T, D)
    logits = hf.astype(jnp.float32) @ router_w.astype(jnp.float32) + router_b.astype(jnp.float32)
    top_vals, top_idx = lax.top_k(logits, TOP_K)
    gates = jax.nn.softmax(top_vals, axis=-1)
    flat_e = top_idx.reshape(A)
    flat_tok = jnp.repeat(jnp.arange(T, dtype=jnp.int32), TOP_K)
    flat_w = gates.reshape(A)
    order = jnp.argsort(flat_e)
    se, stok, sw = flat_e[order], flat_tok[order], flat_w[order]
    counts = jnp.bincount(flat_e, length=N_EXPERTS)
    padded = ((counts + MOE_BLOCK - 1) // MOE_BLOCK) * MOE_BLOCK
    pad_end = jnp.cumsum(padded)
    pad_start = pad_end - padded
    start = jnp.cumsum(counts) - counts
    dest = pad_start[se] + jnp.arange(A) - start[se]
    NB = -(-A // MOE_BLOCK) + N_EXPERTS
    tok_buf = jnp.zeros((NB * MOE_BLOCK,), jnp.int32).at[dest].set(stok)
    w_buf = jnp.zeros((NB * MOE_BLOCK,), h.dtype).at[dest].set(sw.astype(h.dtype))
    block_expert = jnp.clip(jnp.searchsorted(pad_end, jnp.arange(NB) * MOE_BLOCK, side='right'), 0, N_EXPERTS - 1)

    def expert_block(args):
        tok, wt, e = args
        xb = hf[tok]
        gu = xb @ w_gate_up[e] + b_gate_up[e]
        x_glu, x_lin = jnp.split(gu, 2, axis=-1)
        x_glu = jnp.minimum(x_glu, SWIGLU_LIMIT)
        x_lin = jnp.clip(x_lin, -SWIGLU_LIMIT, SWIGLU_LIMIT)
        act = x_glu * jax.nn.sigmoid(SWIGLU_ALPHA * x_glu) * (x_lin + 1.0)
        return (act @ w_down[e] + b_down[e]) * wt[:, None]

    ys = lax.map(expert_block, (tok_buf.reshape(NB, MOE_BLOCK), w_buf.reshape(NB, MOE_BLOCK), block_expert))
    out = jnp.zeros((T, D), ys.dtype).at[tok_buf].add(ys.reshape(NB * MOE_BLOCK, D))
    return out.reshape(B, S, D).astype(h.dtype)


def setup_inputs(seed: int = 0) -> dict:
    key = jax.random.key(seed)
    ks = jax.random.split(key, 32)
    f32 = jnp.float32
    nrm = lambda k, shape, s: jax.random.normal(k, shape, f32) * s
    Dp = DEPTH
    return {
        'x': nrm(ks[0], (BATCH, SEQ, D_MODEL), 1.0),
        'c': nrm(ks[1], (BATCH, D_MODEL), 1.0),
        'ada_w': nrm(ks[2], (Dp, D_MODEL, 6 * D_MODEL), D_MODEL ** -0.5),
        'ada_b': nrm(ks[3], (Dp, 6 * D_MODEL), 0.01),
        'norm1_g': 1.0 + nrm(ks[4], (Dp, D_MODEL), 0.01),
        'norm2_g': 1.0 + nrm(ks[5], (Dp, D_MODEL), 0.01),
        'w_in': nrm(ks[6], (Dp, D_MODEL, IN_COLS), D_MODEL ** -0.5),
        'rwkv_mu': jax.random.uniform(ks[7], (Dp, RWKV_COLS), f32),
        'rwkv_w0': nrm(ks[8], (Dp, RWKV_WIDTH), 0.5),
        'rwkv_w2': nrm(ks[9], (Dp, DECAY_LORA, RWKV_WIDTH), DECAY_LORA ** -0.5),
        'rwkv_a0': nrm(ks[10], (Dp, RWKV_WIDTH), 0.1),
        'rwkv_a2': nrm(ks[11], (Dp, AAA_LORA, RWKV_WIDTH), AAA_LORA ** -0.5),
        'rwkv_g2': nrm(ks[12], (Dp, GATE_LORA, RWKV_WIDTH), GATE_LORA ** -0.5),
        'rwkv_k_k': 1.0 + nrm(ks[13], (Dp, RWKV_WIDTH), 0.1),
        'rwkv_k_a': 1.0 + nrm(ks[14], (Dp, RWKV_WIDTH), 0.1),
        'rwkv_r_k': nrm(ks[15], (Dp, RWKV_HEADS, RWKV_HEAD), 0.1),
        'rwkv_gn_g': 1.0 + nrm(ks[16], (Dp, RWKV_WIDTH), 0.01),
        'rwkv_gn_b': nrm(ks[17], (Dp, RWKV_WIDTH), 0.01),
        'hgrn_lb_logits': nrm(ks[18], (DEPTH + 1, HGRN_WIDTH), 0.1),
        'hgrn_gn_g': 1.0 + nrm(ks[19], (Dp, HGRN_DV), 0.01),
        'proj_a': nrm(ks[20], (Dp, RWKV_WIDTH, D_MODEL), RWKV_WIDTH ** -0.5),
        'proj_b': nrm(ks[21], (Dp, HGRN_WIDTH, D_MODEL), HGRN_WIDTH ** -0.5),
        'w_out': nrm(ks[22], (Dp, D_MODEL, D_MODEL), D_MODEL ** -0.5),
        'router_w': nrm(ks[23], (Dp, D_MODEL, N_EXPERTS), D_MODEL ** -0.5),
        'router_b': nrm(ks[24], (Dp, N_EXPERTS), 0.01),
        'exp_w_gate_up': nrm(ks[25], (Dp, N_EXPERTS, D_MODEL, 2 * D_EXPERT), D_MODEL ** -0.5),
        'exp_b_gate_up': nrm(ks[26], (Dp, N_EXPERTS, 2 * D_EXPERT), 0.01),
        'exp_w_down': nrm(ks[27], (Dp, N_EXPERTS, D_EXPERT, D_MODEL), D_EXPERT ** -0.5),
        'exp_b_down': nrm(ks[28], (Dp, N_EXPERTS, D_MODEL), 0.01),
        'final_norm_g': 1.0 + nrm(ks[29], (D_MODEL,), 0.01),
    }


def reference(x, c, ada_w, ada_b, norm1_g, norm2_g, w_in, rwkv_mu, rwkv_w0, rwkv_w2, rwkv_a0, rwkv_a2,
              rwkv_g2, rwkv_k_k, rwkv_k_a, rwkv_r_k, rwkv_gn_g, rwkv_gn_b, hgrn_lb_logits, hgrn_gn_g,
              proj_a, proj_b, w_out, router_w, router_b, exp_w_gate_up, exp_b_gate_up, exp_w_down,
              exp_b_down, final_norm_g):
    lower_bounds = jnp.cumsum(jax.nn.softmax(hgrn_lb_logits.astype(jnp.float32), axis=0), axis=0)
    c_act = jax.nn.silu(c)
    for l in range(DEPTH):
        mod = c_act @ ada_w[l] + ada_b[l]
        sh1, sc1, g1, sh2, sc2, g2 = [m[:, None, :] for m in jnp.split(mod, 6, axis=-1)]
        h = _rmsnorm(x, norm1_g[l]) * (1.0 + sc1) + sh1
        mix = _token_mixer(h, w_in[l], rwkv_mu[l], rwkv_w0[l], rwkv_w2[l], rwkv_a0[l], rwkv_a2[l], rwkv_g2[l],
                           rwkv_k_k[l], rwkv_k_a[l], rwkv_r_k[l], rwkv_gn_g[l], rwkv_gn_b[l],
                           lower_bounds[l], hgrn_gn_g[l], proj_a[l], proj_b[l], w_out[l])
        x = x + g1 * mix
        h = _rmsnorm(x, norm2_g[l]) * (1.0 + sc2) + sh2
        ffn = _moe(h, router_w[l], router_b[l], exp_w_gate_up[l], exp_b_gate_up[l], exp_w_down[l], exp_b_down[l])
        x = x + g2 * ffn
    return _rmsnorm(x, final_norm_g)
```

```python
import functools

import jax
import jax.numpy as jnp
from jax import lax
from jax.experimental import pallas as pl
from jax.experimental.pallas import tpu as pltpu

F32 = jnp.float32
BF16 = jnp.bfloat16

D_MODEL = 2048
RWKV_WIDTH = 1024
RWKV_HEAD = 64
DECAY_LORA = 64
AAA_LORA = 64
GATE_LORA = 160
LORA_COLS = DECAY_LORA + AAA_LORA + GATE_LORA
LORA_PAD = 512
RWKV_GN_EPS = 64e-5
HGRN_WIDTH = 1024
HGRN_HEAD = 128
GLA_BLOCK = 16
N_EXPERTS = 32
TOP_K = 4
D_EXPERT = 2048
SWIGLU_ALPHA = 1.702
SWIGLU_LIMIT = 7.0
NORM_EPS = 1e-5

LANES = 128
VMEM_LIMIT = 56 * 1024 * 1024

RWKV_CHUNK = 64
HGRN_STEP = 128
PROJ_TM = 1024
PROJ_TN = 512
MOE_TM = 512
MOE_TN = 256
ROUTE_TB = 256
SCATTER_TB = 256
COMBINE_TB = 128


def _cp(sem, vmem=VMEM_LIMIT):
    return pltpu.CompilerParams(dimension_semantics=sem, vmem_limit_bytes=vmem)


def _mm(a, b):
    return jnp.dot(a.astype(BF16), b.astype(BF16), preferred_element_type=F32)


def _mm_nt(a, b):
    return lax.dot_general(a.astype(BF16), b.astype(BF16), (((1,), (1,)), ((), ())),
                           preferred_element_type=F32)


def _mm_tn(a, b):
    return lax.dot_general(a.astype(BF16), b.astype(BF16), (((0,), (0,)), ((), ())),
                           preferred_element_type=F32)


def _split3(x):
    hi = x.astype(BF16)
    r1 = x - hi.astype(F32)
    mid = r1.astype(BF16)
    lo = (r1 - mid.astype(F32)).astype(BF16)
    return hi, mid, lo


def _mm_exact_rhs(m_bf16, x):
    hi, mid, lo = _split3(x)
    d = lambda p: jnp.dot(m_bf16, p, preferred_element_type=F32)
    return d(hi) + d(mid) + d(lo)


def _sigmoid(x):
    return 1.0 / (1.0 + jnp.exp(-x))


def _softplus(x):
    return jnp.maximum(x, 0.0) + jnp.log(1.0 + jnp.exp(-jnp.abs(x)))


def _rms(x, eps):
    return x * lax.rsqrt(jnp.mean(x * x, axis=-1, keepdims=True) + eps)


def _adaln_kernel(c_ref, w_ref, b_ref, o_ref):
    c = c_ref[...]
    c_act = c * _sigmoid(c)
    o_ref[...] = jnp.dot(c_act, w_ref[...], preferred_element_type=F32,
                         precision=lax.Precision.HIGHEST) + b_ref[...]


def _adaln(c, w, b):
    bsz, d = c.shape
    n = w.shape[1]
    tn = 1024
    return pl.pallas_call(
        _adaln_kernel,
        grid=(n // tn,),
        in_specs=[pl.BlockSpec((bsz, d), lambda j: (0, 0)),
                  pl.BlockSpec((d, tn), lambda j: (0, j)),
                  pl.BlockSpec((1, tn), lambda j: (0, j))],
        out_specs=pl.BlockSpec((bsz, tn), lambda j: (0, j)),
        out_shape=jax.ShapeDtypeStruct((bsz, n), F32),
        compiler_params=_cp(("arbitrary",)),
        name="adaln",
    )(c, w, b.reshape(1, n))


def _inproj_kernel(x_ref, g_ref, sc_ref, sh_ref, w_ref, o_ref, h_ref):
    @pl.when(pl.program_id(1) == 0)
    def _():
        y = _rms(x_ref[...], NORM_EPS) * g_ref[...]
        h_ref[...] = (y * (1.0 + sc_ref[0]) + sh_ref[0]).astype(BF16)

    o_ref[...] = jnp.dot(h_ref[...], w_ref[...], preferred_element_type=F32)


def _inproj(xf, g, sc, sh, w, seq):
    t, d = xf.shape
    n = w.shape[1]
    tm = min(PROJ_TM, seq)
    tn = PROJ_TN
    per_b = seq // tm
    return pl.pallas_call(
        _inproj_kernel,
        grid=(t // tm, n // tn),
        in_specs=[pl.BlockSpec((tm, d), lambda i, j: (i, 0)),
                  pl.BlockSpec((1, d), lambda i, j: (0, 0)),
                  pl.BlockSpec((1, 1, d), lambda i, j: (i // per_b, 0, 0)),
                  pl.BlockSpec((1, 1, d), lambda i, j: (i // per_b, 0, 0)),
                  pl.BlockSpec((d, tn), lambda i, j: (0, j))],
        out_specs=pl.BlockSpec((tm, tn), lambda i, j: (i, j)),
        out_shape=jax.ShapeDtypeStruct((t, n), F32),
        scratch_shapes=[pltpu.VMEM((tm, d), BF16)],
        compiler_params=_cp(("arbitrary", "arbitrary")),
        name="inproj",
    )(xf, g.reshape(1, d), sc, sh, w)


def _rwkv_kernel(r_ref, k_ref, v_ref, lo_ref, mu_ref, mulo_ref, w0_ref, a0_ref, kk_ref, ka_ref, rk_ref,
                 gng_ref, gnb_ref, w2_ref, a2_ref, g2_ref, o_ref,
                 st_ref, cr_ref, ck_ref, cv_ref, clo_ref):
    L = RWKV_CHUNK
    H2 = 2 * L
    n_pair = RWKV_WIDTH // LANES

    @pl.when(pl.program_id(1) == 0)
    def _():
        st_ref[...] = jnp.zeros_like(st_ref)
        cr_ref[...] = jnp.zeros_like(cr_ref)
        ck_ref[...] = jnp.zeros_like(ck_ref)
        cv_ref[...] = jnp.zeros_like(cv_ref)
        clo_ref[...] = jnp.zeros_like(clo_ref)

    def shift(x_ref, mu, carry_ref):
        x = x_ref[...]
        rows = lax.broadcasted_iota(jnp.int32, x.shape, 0)
        prev = jnp.where(rows == 0, carry_ref[...], pltpu.roll(x, 1, axis=0))
        carry_ref[...] = x[L - 1:L, :]
        return x + (prev - x) * mu

    mu = mu_ref[...]
    r = shift(r_ref, mu[:, 0:RWKV_WIDTH], cr_ref)
    k = shift(k_ref, mu[:, RWKV_WIDTH:2 * RWKV_WIDTH], ck_ref)
    v = shift(v_ref, mu[:, 2 * RWKV_WIDTH:3 * RWKV_WIDTH], cv_ref)
    lo = shift(lo_ref, mulo_ref[...], clo_ref)

    wpre = w0_ref[...] + _mm(jnp.tanh(lo), w2_ref[...])
    lw = -jnp.exp(-_softplus(-wpre) - 0.5)
    a = _sigmoid(a0_ref[...] + _mm(lo, a2_ref[...]))
    g = _mm(_sigmoid(lo), g2_ref[...])

    ti = lax.broadcasted_iota(jnp.int32, (L, L), 0)
    tj = lax.broadcasted_iota(jnp.int32, (L, L), 1)
    tri = jnp.where(tj <= ti, 1.0, 0.0).astype(BF16)
    cl = _mm_exact_rhs(tri, lw)
    cl_last = cl[L - 1:L, :]
    e_pos = jnp.exp(cl)
    e_neg = jnp.exp(-cl)
    e_prev = jnp.exp(cl - lw)
    e_end = jnp.exp(cl_last - cl)
    p_last = jnp.exp(cl_last)

    kk = k * kk_ref[...]
    k2 = k * (1.0 + (a - 1.0) * ka_ref[...])
    f_b = a * e_neg
    f_bend = a * e_end
    rk = rk_ref[...]
    gng = gng_ref[...]
    gnb = gnb_ref[...]

    lane = lax.broadcasted_iota(jnp.int32, (L, LANES), 1)
    lo_half = lane < RWKV_HEAD
    si = lax.broadcasted_iota(jnp.int32, (H2, H2), 0)
    sj = lax.broadcasted_iota(jnp.int32, (H2, H2), 1)
    same = (si // L) == (sj // L)
    m_strict = same & (sj < si)
    m_incl = same & (sj <= si)
    eye = si == sj
    own = (si // L) == (sj // RWKV_HEAD)

    def stack(x):
        return jnp.concatenate([jnp.where(lo_half, x, 0.0), jnp.where(lo_half, 0.0, x)], axis=0)

    def rep(x):
        return jnp.concatenate([x, x], axis=0)

    for p in range(n_pair):
        sl = slice(p * LANES, (p + 1) * LANES)
        kk_s = stack(kk[:, sl])
        nrm = jnp.sqrt(jnp.sum(kk_s * kk_s, axis=-1, keepdims=True))
        kkn = kk_s / jnp.maximum(nrm, 1e-12)
        at = -kkn * rep(e_prev[:, sl])
        bt = kkn * rep(f_b[:, sl])
        bend = kkn * rep(f_bend[:, sl])
        k2_s = stack(k2[:, sl])
        kt = k2_s * rep(e_neg[:, sl])
        kend = k2_s * rep(e_end[:, sl])
        r_s = stack(r[:, sl])
        rt = r_s * rep(e_pos[:, sl])
        v_s = stack(v[:, sl])

        gm = _mm_nt(jnp.concatenate([at, rt], axis=0), jnp.concatenate([bt, kt], axis=0))
        a_ab = jnp.where(m_strict, gm[:H2, :H2], 0.0)
        a_ak = jnp.where(m_strict, gm[:H2, H2:], 0.0)
        a_rb = jnp.where(m_incl, gm[H2:, :H2], 0.0)
        a_rk = jnp.where(m_incl, gm[H2:, H2:], 0.0)

        tinv = jnp.where(eye, 1.0, 0.0) + a_ab
        pw = a_ab
        n_sq = max(1, (L - 1).bit_length() - 1)
        for _ in range(n_sq):
            pw = _mm(pw, pw)
            tinv = tinv + _mm(tinv, pw)

        akv = _mm(a_ak, v_s)
        tx = _mm(tinv, jnp.concatenate([at, akv], axis=1))
        y2 = _mm(a_rb, tx)
        rh = rt + y2[:, :LANES]
        y0 = y2[:, LANES:] + _mm(a_rk, v_s)
        btx = _mm_tn(bend, tx)
        m_mat = jnp.where(eye, jnp.broadcast_to(p_last[:, sl], (H2, LANES)), 0.0) + btx[:, :LANES]
        n_mat = btx[:, LANES:] + _mm_tn(kend, v_s)

        s0 = st_ref[p]
        y = _mm(rh, s0) + y0
        st_ref[p] = _mm(m_mat, s0) + n_mat

        mean = jnp.sum(y, axis=-1, keepdims=True) * (1.0 / RWKV_HEAD)
        dlt = jnp.where(own, y - mean, 0.0)
        var = jnp.sum(dlt * dlt, axis=-1, keepdims=True) * (1.0 / RWKV_HEAD)
        yn = dlt * lax.rsqrt(var + RWKV_GN_EPS) * gng[:, sl] + jnp.where(own, gnb[:, sl], 0.0)
        bonus = jnp.sum(r_s * k2_s * rk[:, sl], axis=-1, keepdims=True) * v_s
        ys = yn + bonus
        o_ref[:, sl] = ((ys[:L] + ys[L:]) * g[:, sl]).astype(o_ref.dtype)


def _rwkv(proj, bsz, seq, mu_rkv, mu_lo, w0, a0, k_k, k_a, r_k, gn_g, gn_b, w2p, a2p, g2p):
    L = RWKV_CHUNK
    nc = seq // L
    W = RWKV_WIDTH
    row = lambda n: pl.BlockSpec((1, n), lambda b, c: (0, 0))
    full = lambda a: pl.BlockSpec(a.shape, lambda b, c: (0, 0))
    lo_blk = proj.shape[1] // LORA_PAD - 1
    v1 = lambda a: a.reshape(1, -1)
    return pl.pallas_call(
        _rwkv_kernel,
        grid=(bsz, nc),
        in_specs=[pl.BlockSpec((L, W), lambda b, c: (b * nc + c, 0)),
                  pl.BlockSpec((L, W), lambda b, c: (b * nc + c, 1)),
                  pl.BlockSpec((L, W), lambda b, c: (b * nc + c, 2)),
                  pl.BlockSpec((L, LORA_PAD), lambda b, c: (b * nc + c, lo_blk)),
                  row(3 * W), row(LORA_PAD), row(W), row(W), row(W), row(W), row(W), row(W), row(W),
                  full(w2p), full(a2p), full(g2p)],
        out_specs=pl.BlockSpec((L, W), lambda b, c: (b * nc + c, 0)),
        out_shape=jax.ShapeDtypeStruct((bsz * seq, W), BF16),
        scratch_shapes=[pltpu.VMEM((W // LANES, LANES, LANES), F32),
                        pltpu.VMEM((1, W), F32), pltpu.VMEM((1, W), F32), pltpu.VMEM((1, W), F32),
                        pltpu.VMEM((1, LORA_PAD), F32)],
        compiler_params=_cp(("arbitrary", "arbitrary")),
        name="rwkv7",
    )(proj, proj, proj, proj, v1(mu_rkv), v1(mu_lo), v1(w0), v1(a0), v1(k_k), v1(k_a), v1(r_k), v1(gn_g),
      v1(gn_b), w2p, a2p, g2p)


def _hgrn_kernel(q_ref, f_ref, i_ref, og_ref, lb_ref, gn_ref, o_ref, st_ref):
    n = HGRN_STEP
    blk = GLA_BLOCK
    nb = n // blk
    n_head = HGRN_WIDTH // HGRN_HEAD

    @pl.when(pl.program_id(1) == 0)
    def _():
        st_ref[...] = jnp.zeros_like(st_ref)

    lb = lb_ref[...]
    qf = q_ref[...]
    q = qf * _sigmoid(qf)
    forget = lb + (1.0 - lb) * _sigmoid(f_ref[...])
    log_f = jnp.log(forget)
    kx = 1.0 - forget
    vv = i_ref[...]
    og = og_ref[...]

    ti = lax.broadcasted_iota(jnp.int32, (n, n), 0)
    tj = lax.broadcasted_iota(jnp.int32, (n, n), 1)
    same = (ti // blk) == (tj // blk)
    causal = same & (tj <= ti)
    b = _mm_exact_rhs(jnp.where(causal, 1.0, 0.0).astype(BF16), log_f)
    b_last = _mm_exact_rhs(jnp.where(same, 1.0, 0.0).astype(BF16), log_f)
    q_dec = q * jnp.exp(b)
    k_dec = kx * jnp.exp(-b)
    k_end = kx * jnp.exp(b_last - b)
    d_end = jnp.exp(b_last)
    rows = lax.broadcasted_iota(jnp.int32, (n, HGRN_HEAD), 0)
    gn = gn_ref[...]

    for h in range(n_head):
        sl = slice(h * HGRN_HEAD, (h + 1) * HGRN_HEAD)
        qd, kd, ke, v = q_dec[:, sl], k_dec[:, sl], k_end[:, sl], vv[:, sl]
        attn = jnp.where(causal, _mm_nt(qd, kd), 0.0)
        o = _mm(attn, v)
        vt = v.T
        st = st_ref[h]
        inter = []
        for c in range(nb):
            inter.append(_mm_nt(qd[c * blk:(c + 1) * blk], st))
            kv_t = _mm(vt, jnp.where((rows // blk) == c, ke, 0.0))
            st = st * d_end[c * blk:c * blk + 1, sl] + kv_t
        st_ref[h] = st
        o = o + jnp.concatenate(inter, axis=0)
        o = _rms(o, NORM_EPS) * gn
        ogh = og[:, sl]
        o_ref[:, sl] = (o * (ogh * _sigmoid(ogh))).astype(o_ref.dtype)


def _hgrn(proj, bsz, seq, lb, gn_g):
    n = HGRN_STEP
    ns = seq // n
    W = HGRN_WIDTH
    col = lambda j: pl.BlockSpec((n, W), lambda b, c: (b * ns + c, j))
    return pl.pallas_call(
        _hgrn_kernel,
        grid=(bsz, ns),
        in_specs=[col(3), col(4), col(5), col(6),
                  pl.BlockSpec((1, W), lambda b, c: (0, 0)),
                  pl.BlockSpec((1, HGRN_HEAD), lambda b, c: (0, 0))],
        out_specs=pl.BlockSpec((n, W), lambda b, c: (b * ns + c, 0)),
        out_shape=jax.ShapeDtypeStruct((bsz * seq, W), BF16),
        scratch_shapes=[pltpu.VMEM((W // HGRN_HEAD, HGRN_HEAD, HGRN_HEAD), F32)],
        compiler_params=_cp(("arbitrary", "arbitrary")),
        name="hgrn2",
    )(proj, proj, proj, proj, lb.reshape(1, W), gn_g.reshape(1, HGRN_HEAD))


def _merge_kernel(ya_ref, yb_ref, pa_ref, pb_ref, ga_ref, gb_ref, o_ref):
    ma = jnp.dot(ya_ref[...], pa_ref[...], preferred_element_type=F32)
    mb = jnp.dot(yb_ref[...], pb_ref[...], preferred_element_type=F32)
    o_ref[...] = (_sigmoid(ga_ref[...]) * ma + _sigmoid(gb_ref[...]) * mb).astype(o_ref.dtype)


def _merge(ya, yb, pa, pb, proj):
    t, w = ya.shape
    d = pa.shape[1]
    tm, tn = min(PROJ_TM, t), PROJ_TN
    ga0 = (3 * RWKV_WIDTH + 4 * HGRN_WIDTH) // tn
    gb0 = ga0 + d // tn
    return pl.pallas_call(
        _merge_kernel,
        grid=(t // tm, d // tn),
        in_specs=[pl.BlockSpec((tm, w), lambda i, j: (i, 0)),
                  pl.BlockSpec((tm, w), lambda i, j: (i, 0)),
                  pl.BlockSpec((w, tn), lambda i, j: (0, j)),
                  pl.BlockSpec((w, tn), lambda i, j: (0, j)),
                  pl.BlockSpec((tm, tn), lambda i, j: (i, ga0 + j)),
                  pl.BlockSpec((tm, tn), lambda i, j: (i, gb0 + j))],
        out_specs=pl.BlockSpec((tm, tn), lambda i, j: (i, j)),
        out_shape=jax.ShapeDtypeStruct((t, d), BF16),
        compiler_params=_cp(("arbitrary", "arbitrary")),
        name="merge",
    )(ya, yb, pa, pb, proj, proj)


def _outproj_kernel(m_ref, w_ref, x_ref, g_ref, o_ref):
    mix = jnp.dot(m_ref[...], w_ref[...], preferred_element_type=F32)
    o_ref[...] = x_ref[...] + g_ref[0] * mix


def _outproj(merged, w, xf, g1, seq):
    t, d = merged.shape
    tm, tn = min(PROJ_TM, seq), PROJ_TN
    per_b = seq // tm
    return pl.pallas_call(
        _outproj_kernel,
        grid=(t // tm, d // tn),
        in_specs=[pl.BlockSpec((tm, d), lambda i, j: (i, 0)),
                  pl.BlockSpec((d, tn), lambda i, j: (0, j)),
                  pl.BlockSpec((tm, tn), lambda i, j: (i, j)),
                  pl.BlockSpec((1, 1, tn), lambda i, j: (i // per_b, 0, j))],
        out_specs=pl.BlockSpec((tm, tn), lambda i, j: (i, j)),
        out_shape=jax.ShapeDtypeStruct((t, d), F32),
        compiler_params=_cp(("arbitrary", "arbitrary")),
        name="outproj",
    )(merged, w, xf, g1)


def _route_kernel(x_ref, g_ref, sc_ref, sh_ref, rw_ref, rb_ref, h_ref, idx_ref, gate_ref, rank_ref, cnt_ref,
                  carry_ref):
    tb = x_ref.shape[0]

    @pl.when(pl.program_id(0) == 0)
    def _():
        carry_ref[...] = jnp.zeros_like(carry_ref)

    h = (_rms(x_ref[...], NORM_EPS) * g_ref[...]) * (1.0 + sc_ref[0]) + sh_ref[0]
    h_ref[...] = h
    logits = jnp.dot(h, rw_ref[...], preferred_element_type=F32, precision=lax.Precision.HIGHEST) + rb_ref[...]

    lane = lax.broadcasted_iota(jnp.int32, logits.shape, 1).astype(F32)
    work = logits
    vals, idxs, sels = [], [], []
    for _ in range(TOP_K):
        m = jnp.max(work, axis=-1, keepdims=True)
        idx = jnp.min(jnp.where(work == m, lane, float(N_EXPERTS)), axis=-1, keepdims=True)
        sel = lane == idx
        vals.append(m)
        idxs.append(idx)
        sels.append(sel)
        work = jnp.where(sel, -jnp.inf, work)
    exps = [jnp.exp(v - vals[0]) for v in vals]
    den = exps[0] + exps[1] + exps[2] + exps[3]

    onehot = jnp.zeros(logits.shape, F32)
    for sel in sels:
        onehot = onehot + jnp.where(sel, 1.0, 0.0)
    ti = lax.broadcasted_iota(jnp.int32, (tb, tb), 0)
    tj = lax.broadcasted_iota(jnp.int32, (tb, tb), 1)
    below = jnp.where(tj < ti, 1.0, 0.0).astype(BF16)
    before = jnp.dot(below, onehot.astype(BF16), preferred_element_type=F32) + carry_ref[...]
    carry_ref[...] = carry_ref[...] + jnp.sum(onehot, axis=0, keepdims=True)
    cnt_ref[...] = carry_ref[...]

    out_lane = lax.broadcasted_iota(jnp.int32, (tb, LANES), 1)
    idx_out = jnp.zeros((tb, LANES), jnp.int32)
    rank_out = jnp.zeros((tb, LANES), jnp.int32)
    gate_out = jnp.zeros((tb, LANES), F32)
    for j in range(TOP_K):
        rank = jnp.sum(jnp.where(sels[j], before, 0.0), axis=-1, keepdims=True)
        idx_out = jnp.where(out_lane == j, idxs[j].astype(jnp.int32), idx_out)
        rank_out = jnp.where(out_lane == j, rank.astype(jnp.int32), rank_out)
        gate_out = jnp.where(out_lane == j, exps[j] / den, gate_out)
    idx_ref[...] = idx_out
    rank_ref[...] = rank_out
    gate_ref[...] = gate_out


def _route(x1, g, sc, sh, rw, rb, seq):
    t, d = x1.shape
    tb = min(ROUTE_TB, seq)
    per_b = seq // tb
    e = rw.shape[1]
    wide = lambda dt: jax.ShapeDtypeStruct((t, LANES), dt)
    return pl.pallas_call(
        _route_kernel,
        grid=(t // tb,),
        in_specs=[pl.BlockSpec((tb, d), lambda i: (i, 0)),
                  pl.BlockSpec((1, d), lambda i: (0, 0)),
                  pl.BlockSpec((1, 1, d), lambda i: (i // per_b, 0, 0)),
                  pl.BlockSpec((1, 1, d), lambda i: (i // per_b, 0, 0)),
                  pl.BlockSpec((d, e), lambda i: (0, 0)),
                  pl.BlockSpec((1, e), lambda i: (0, 0))],
        out_specs=[pl.BlockSpec((tb, d), lambda i: (i, 0)),
                   pl.BlockSpec((tb, LANES), lambda i: (i, 0)),
                   pl.BlockSpec((tb, LANES), lambda i: (i, 0)),
                   pl.BlockSpec((tb, LANES), lambda i: (i, 0)),
                   pl.BlockSpec((1, e), lambda i: (0, 0))],
        out_shape=[jax.ShapeDtypeStruct((t, d), F32), wide(jnp.int32), wide(F32), wide(jnp.int32),
                   jax.ShapeDtypeStruct((1, e), F32)],
        scratch_shapes=[pltpu.VMEM((1, e), F32)],
        compiler_params=_cp(("arbitrary",)),
        name="route",
    )(x1, g.reshape(1, d), sc, sh, rw, rb.reshape(1, e))


def _scatter_kernel(dest_ref, h_ref, xs_in_ref, xs_ref, sem):
    del xs_in_ref
    tb = h_ref.shape[0]
    base = pl.program_id(0) * tb

    def copy(t, j):
        d = dest_ref[(base + t) * TOP_K + j]
        return pltpu.make_async_copy(h_ref.at[pl.ds(t, 1), :], xs_ref.at[pl.ds(d, 1), :], sem)

    def start(t, carry):
        for j in range(TOP_K):
            copy(t, j).start()
        return carry

    def wait(t, carry):
        for j in range(TOP_K):
            copy(t, j).wait()
        return carry

    lax.fori_loop(0, tb, start, 0)
    lax.fori_loop(0, tb, wait, 0)


def _scatter_rows(dest_flat, h2, n_rows):
    t, d = h2.shape
    tb = min(SCATTER_TB, t)
    xs0 = jnp.zeros((n_rows, d), F32)
    return pl.pallas_call(
        _scatter_kernel,
        grid_spec=pltpu.PrefetchScalarGridSpec(
            num_scalar_prefetch=1,
            grid=(t // tb,),
            in_specs=[pl.BlockSpec((tb, d), lambda i, dest: (i, 0)),
                      pl.BlockSpec(memory_space=pl.ANY)],
            out_specs=pl.BlockSpec(memory_space=pl.ANY),
            scratch_shapes=[pltpu.SemaphoreType.DMA(())],
        ),
        out_shape=jax.ShapeDtypeStruct((n_rows, d), F32),
        input_output_aliases={2: 0},
        compiler_params=_cp(("arbitrary",)),
        name="scatter_rows",
    )(dest_flat, h2, xs0)


def _expert_kernel(be_ref, bs_ref, nu_ref, x_ref, wg_ref, wl_ref, bg_ref, bl_ref, wd_ref, bd_ref, o_ref, xb_ref):
    del be_ref, bs_ref
    i = pl.program_id(0)
    n = pl.program_id(1)

    @pl.when(i < nu_ref[0])
    def _():
        @pl.when(n == 0)
        def _():
            xb_ref[...] = x_ref[...].astype(BF16)

        xb = xb_ref[...]
        glu = jnp.dot(xb, wg_ref[0].astype(BF16), preferred_element_type=F32) + bg_ref[0]
        lin = jnp.dot(xb, wl_ref[0].astype(BF16), preferred_element_type=F32) + bl_ref[0]
        glu = jnp.minimum(glu, SWIGLU_LIMIT)
        lin = jnp.clip(lin, -SWIGLU_LIMIT, SWIGLU_LIMIT)
        act = glu * _sigmoid(SWIGLU_ALPHA * glu) * (lin + 1.0)
        part = jnp.dot(act.astype(BF16), wd_ref[0].astype(BF16), preferred_element_type=F32)

        @pl.when(n == 0)
        def _():
            o_ref[...] = part + bd_ref[0]

        @pl.when(n > 0)
        def _():
            o_ref[...] = o_ref[...] + part

    @pl.when((i >= nu_ref[0]) & (n == 0))
    def _():
        o_ref[...] = jnp.zeros_like(o_ref)


def _experts(blk_expert, blk_src, n_used, xs, w_gu, b_gu, w_down, b_down):
    n_rows, d = xs.shape
    tm, tn = MOE_TM, MOE_TN
    nb = n_rows // tm
    e, _, two_de = w_gu.shape
    de = two_de // 2
    nt = de // tn
    b_gu3 = b_gu.reshape(e, 1, two_de)
    b_d3 = b_down.reshape(e, 1, d)
    return pl.pallas_call(
        _expert_kernel,
        grid_spec=pltpu.PrefetchScalarGridSpec(
            num_scalar_prefetch=3,
            grid=(nb, nt),
            in_specs=[pl.BlockSpec((tm, d), lambda i, n, be, bs, nu: (bs[i], 0)),
                      pl.BlockSpec((1, d, tn), lambda i, n, be, bs, nu: (be[i], 0, n)),
                      pl.BlockSpec((1, d, tn), lambda i, n, be, bs, nu: (be[i], 0, n + nt)),
                      pl.BlockSpec((1, 1, tn), lambda i, n, be, bs, nu: (be[i], 0, n)),
                      pl.BlockSpec((1, 1, tn), lambda i, n, be, bs, nu: (be[i], 0, n + nt)),
                      pl.BlockSpec((1, tn, d), lambda i, n, be, bs, nu: (be[i], n, 0)),
                      pl.BlockSpec((1, 1, d), lambda i, n, be, bs, nu: (be[i], 0, 0))],
            out_specs=pl.BlockSpec((tm, d), lambda i, n, be, bs, nu: (i, 0)),
            scratch_shapes=[pltpu.VMEM((tm, d), BF16)],
        ),
        out_shape=jax.ShapeDtypeStruct((n_rows, d), F32),
        compiler_params=_cp(("arbitrary", "arbitrary")),
        name="experts",
    )(blk_expert, blk_src, n_used, xs, w_gu, w_gu, b_gu3, b_gu3, w_down, b_d3)


def _combine_kernel(dest_ref, ys_ref, gate_ref, x_ref, g2_ref, fg_ref, o_ref, buf_ref, sem):
    tb = x_ref.shape[0]
    base = pl.program_id(0) * tb

    def copy(t, j):
        d = dest_ref[(base + t) * TOP_K + j]
        return pltpu.make_async_copy(ys_ref.at[pl.ds(d, 1), :], buf_ref.at[j, pl.ds(t, 1), :], sem)

    def start(t, carry):
        for j in range(TOP_K):
            copy(t, j).start()
        return carry

    def wait(t, carry):
        for j in range(TOP_K):
            copy(t, j).wait()
        return carry

    lax.fori_loop(0, tb, start, 0)
    lax.fori_loop(0, tb, wait, 0)

    gate = gate_ref[...]
    ffn = gate[:, 0:1] * buf_ref[0]
    for j in range(1, TOP_K):
        ffn = ffn + gate[:, j:j + 1] * buf_ref[j]
    x2 = x_ref[...] + g2_ref[0] * ffn
    o_ref[...] = _rms(x2, NORM_EPS) * fg_ref[...]


def _combine(dest_flat, ys, gates, x1, g2, fg, seq):
    t, d = x1.shape
    tb = min(COMBINE_TB, seq)
    per_b = seq // tb
    return pl.pallas_call(
        _combine_kernel,
        grid_spec=pltpu.PrefetchScalarGridSpec(
            num_scalar_prefetch=1,
            grid=(t // tb,),
            in_specs=[pl.BlockSpec(memory_space=pl.ANY),
                      pl.BlockSpec((tb, LANES), lambda i, dest: (i, 0)),
                      pl.BlockSpec((tb, d), lambda i, dest: (i, 0)),
                      pl.BlockSpec((1, 1, d), lambda i, dest: (i // per_b, 0, 0)),
                      pl.BlockSpec((1, d), lambda i, dest: (0, 0))],
            out_specs=pl.BlockSpec((tb, d), lambda i, dest: (i, 0)),
            scratch_shapes=[pltpu.VMEM((TOP_K, tb, d), F32), pltpu.SemaphoreType.DMA(())],
        ),
        out_shape=jax.ShapeDtypeStruct((t, d), F32),
        compiler_params=_cp(("arbitrary",)),
        name="combine",
    )(dest_flat, ys, gates, x1, g2, fg.reshape(1, d))


def _moe(x1, g, sc2, sh2, g2, router_w, router_b, w_gu, b_gu, w_down, b_down, final_g, seq):
    t, d = x1.shape
    a = t * TOP_K
    tm = MOE_TM
    nb = -(-a // tm) + N_EXPERTS
    h2, idx_w, gate_w, rank_w, counts = _route(x1, g, sc2, sh2, router_w, router_b, seq)
    idx = idx_w[:, :TOP_K]
    rank = rank_w[:, :TOP_K]
    cnt = counts.reshape(-1).astype(jnp.int32)
    padded = ((cnt + tm - 1) // tm) * tm
    pad_end = jnp.cumsum(padded)
    pad_start = pad_end - padded
    dest = (pad_start[idx] + rank).reshape(a).astype(jnp.int32)
    n_used = (pad_end[-1] // tm).astype(jnp.int32)
    blk = jnp.arange(nb, dtype=jnp.int32)
    blk_src = jnp.minimum(blk, n_used - 1)
    blk_expert = jnp.clip(jnp.searchsorted(pad_end, blk_src * tm, side='right'), 0, N_EXPERTS - 1).astype(jnp.int32)
    xs = _scatter_rows(dest, h2, nb * tm)
    ys = _experts(blk_expert, blk_src, n_used.reshape(1), xs, w_gu, b_gu, w_down, b_down)
    return _combine(dest, ys, gate_w, x1, g2, final_g, seq)


def kernel(x, c, ada_w, ada_b, norm1_g, norm2_g, w_in, rwkv_mu, rwkv_w0, rwkv_w2, rwkv_a0, rwkv_a2, rwkv_g2,
           rwkv_k_k, rwkv_k_a, rwkv_r_k, rwkv_gn_g, rwkv_gn_b, hgrn_lb_logits, hgrn_gn_g, proj_a, proj_b, w_out,
           router_w, router_b, exp_w_gate_up, exp_b_gate_up, exp_w_down, exp_b_down, final_norm_g):
    bsz, seq, d = x.shape
    depth = ada_w.shape[0]
    lower_bounds = jnp.cumsum(jax.nn.softmax(hgrn_lb_logits.astype(F32), axis=0), axis=0)
    xf = x.reshape(bsz * seq, d)
    rkv = 3 * RWKV_WIDTH
    rwkv_cols = rkv + LORA_COLS
    hg_gate = 4 * HGRN_WIDTH + 2 * d
    for l in range(depth):
        mod = _adaln(c, ada_w[l], ada_b[l])
        sh1, sc1, g1, sh2, sc2, g2 = [m.reshape(bsz, 1, d) for m in jnp.split(mod, 6, axis=-1)]

        wl = w_in[l]
        w_cat = jnp.concatenate([wl[:, :rkv], wl[:, rwkv_cols:rwkv_cols + hg_gate], wl[:, rkv:rwkv_cols],
                                 jnp.zeros((d, LORA_PAD - LORA_COLS), wl.dtype)], axis=1).astype(BF16)
        proj = _inproj(xf, norm1_g[l], sc1, sh1, w_cat, seq)

        mu = rwkv_mu[l]
        mu_lo = jnp.pad(mu[rkv:], (0, LORA_PAD - LORA_COLS))
        pad_rows = lambda w, r0: jnp.zeros((LORA_PAD, RWKV_WIDTH), F32).at[r0:r0 + w.shape[0]].set(w).astype(BF16)
        ya = _rwkv(proj, bsz, seq, mu[:rkv], mu_lo, rwkv_w0[l], rwkv_a0[l], rwkv_k_k[l], rwkv_k_a[l],
                   rwkv_r_k[l].reshape(-1), rwkv_gn_g[l], rwkv_gn_b[l],
                   pad_rows(rwkv_w2[l], 0), pad_rows(rwkv_a2[l], DECAY_LORA),
                   pad_rows(rwkv_g2[l], DECAY_LORA + AAA_LORA))
        yb = _hgrn(proj, bsz, seq, lower_bounds[l], hgrn_gn_g[l])
        merged = _merge(ya, yb, proj_a[l].astype(BF16), proj_b[l].astype(BF16), proj)
        x1 = _outproj(merged, w_out[l].astype(BF16), xf, g1, seq)
        assert depth == 1
        out = _moe(x1, norm2_g[l], sc2, sh2, g2, router_w[l], router_b[l], exp_w_gate_up[l], exp_b_gate_up[l],
                   exp_w_down[l], exp_b_down[l], final_norm_g, seq)
    return out.reshape(bsz, seq, d)
```

```python
import functools

import jax
import jax.numpy as jnp
from jax import lax
from jax.experimental import pallas as pl
from jax.experimental.pallas import tpu as pltpu

F32 = jnp.float32
BF16 = jnp.bfloat16

D_MODEL = 2048
RWKV_WIDTH = 1024
RWKV_HEAD = 64
DECAY_LORA = 64
AAA_LORA = 64
GATE_LORA = 160
LORA_COLS = DECAY_LORA + AAA_LORA + GATE_LORA
LORA_PAD = 512
RWKV_GN_EPS = 64e-5
HGRN_WIDTH = 1024
HGRN_HEAD = 128
GLA_BLOCK = 16
N_EXPERTS = 32
TOP_K = 4
D_EXPERT = 2048
SWIGLU_ALPHA = 1.702
SWIGLU_LIMIT = 7.0
NORM_EPS = 1e-5

LANES = 128
VMEM_LIMIT = 56 * 1024 * 1024

RWKV_CHUNK = 64
HGRN_STEP = 128
PROJ_TM = 1024
PROJ_TN = 512
MOE_TM = 512
MOE_TN = 256
MOE_TNB = 512
ROUTE_TB = 256
SCATTER_TB = 256
COMBINE_TB = 128


def _cp(sem, vmem=VMEM_LIMIT):
    return pltpu.CompilerParams(dimension_semantics=sem, vmem_limit_bytes=vmem)


def _mm(a, b):
    return jnp.dot(a.astype(BF16), b.astype(BF16), preferred_element_type=F32)


def _mm_nt(a, b):
    return lax.dot_general(a.astype(BF16), b.astype(BF16), (((1,), (1,)), ((), ())),
                           preferred_element_type=F32)


def _mm_tn(a, b):
    return lax.dot_general(a.astype(BF16), b.astype(BF16), (((0,), (0,)), ((), ())),
                           preferred_element_type=F32)


def _split3(x):
    hi = x.astype(BF16)
    r1 = x - hi.astype(F32)
    mid = r1.astype(BF16)
    lo = (r1 - mid.astype(F32)).astype(BF16)
    return hi, mid, lo


def _mm_exact_rhs(m_bf16, x):
    hi, mid, lo = _split3(x)
    d = lambda p: jnp.dot(m_bf16, p, preferred_element_type=F32)
    return d(hi) + d(mid) + d(lo)


def _sigmoid(x):
    return 1.0 / (1.0 + jnp.exp(-x))


def _softplus(x):
    return jnp.maximum(x, 0.0) + jnp.log(1.0 + jnp.exp(-jnp.abs(x)))


def _rms(x, eps):
    return x * lax.rsqrt(jnp.mean(x * x, axis=-1, keepdims=True) + eps)


def _adaln_kernel(c_ref, w_ref, b_ref, o_ref):
    c = c_ref[...]
    c_act = c * _sigmoid(c)
    o_ref[...] = jnp.dot(c_act, w_ref[...], preferred_element_type=F32,
                         precision=lax.Precision.HIGHEST) + b_ref[...]


def _adaln(c, w, b):
    bsz, d = c.shape
    n = w.shape[1]
    tn = 1024
    return pl.pallas_call(
        _adaln_kernel,
        grid=(n // tn,),
        in_specs=[pl.BlockSpec((bsz, d), lambda j: (0, 0)),
                  pl.BlockSpec((d, tn), lambda j: (0, j)),
                  pl.BlockSpec((1, tn), lambda j: (0, j))],
        out_specs=pl.BlockSpec((bsz, tn), lambda j: (0, j)),
        out_shape=jax.ShapeDtypeStruct((bsz, n), F32),
        compiler_params=_cp(("arbitrary",)),
        name="adaln",
    )(c, w, b.reshape(1, n))


def _inproj_kernel(x_ref, g_ref, sc_ref, sh_ref, w_ref, o_ref, h_ref):
    @pl.when(pl.program_id(1) == 0)
    def _():
        y = _rms(x_ref[...], NORM_EPS) * g_ref[...]
        h_ref[...] = (y * (1.0 + sc_ref[0]) + sh_ref[0]).astype(BF16)

    o_ref[...] = jnp.dot(h_ref[...], w_ref[...], preferred_element_type=F32)


def _inproj(xf, g, sc, sh, w, seq):
    t, d = xf.shape
    n = w.shape[1]
    tm = min(PROJ_TM, seq)
    tn = PROJ_TN
    per_b = seq // tm
    return pl.pallas_call(
        _inproj_kernel,
        grid=(t // tm, n // tn),
        in_specs=[pl.BlockSpec((tm, d), lambda i, j: (i, 0)),
                  pl.BlockSpec((1, d), lambda i, j: (0, 0)),
                  pl.BlockSpec((1, 1, d), lambda i, j: (i // per_b, 0, 0)),
                  pl.BlockSpec((1, 1, d), lambda i, j: (i // per_b, 0, 0)),
                  pl.BlockSpec((d, tn), lambda i, j: (0, j))],
        out_specs=pl.BlockSpec((tm, tn), lambda i, j: (i, j)),
        out_shape=jax.ShapeDtypeStruct((t, n), F32),
        scratch_shapes=[pltpu.VMEM((tm, d), BF16)],
        compiler_params=_cp(("arbitrary", "arbitrary")),
        name="inproj",
    )(xf, g.reshape(1, d), sc, sh, w)


def _rwkv_kernel(r_ref, k_ref, v_ref, lo_ref, mu_ref, mulo_ref, w0_ref, a0_ref, kk_ref, ka_ref, rk_ref,
                 gng_ref, gnb_ref, w2_ref, a2_ref, g2_ref, o_ref,
                 st_ref, cr_ref, ck_ref, cv_ref, clo_ref):
    L = RWKV_CHUNK
    H2 = 2 * L
    n_pair = RWKV_WIDTH // LANES

    @pl.when(pl.program_id(1) == 0)
    def _():
        st_ref[...] = jnp.zeros_like(st_ref)
        cr_ref[...] = jnp.zeros_like(cr_ref)
        ck_ref[...] = jnp.zeros_like(ck_ref)
        cv_ref[...] = jnp.zeros_like(cv_ref)
        clo_ref[...] = jnp.zeros_like(clo_ref)

    def shift(x_ref, mu, carry_ref):
        x = x_ref[...]
        rows = lax.broadcasted_iota(jnp.int32, x.shape, 0)
        prev = jnp.where(rows == 0, carry_ref[...], pltpu.roll(x, 1, axis=0))
        carry_ref[...] = x[L - 1:L, :]
        return x + (prev - x) * mu

    mu = mu_ref[...]
    r = shift(r_ref, mu[:, 0:RWKV_WIDTH], cr_ref)
    k = shift(k_ref, mu[:, RWKV_WIDTH:2 * RWKV_WIDTH], ck_ref)
    v = shift(v_ref, mu[:, 2 * RWKV_WIDTH:3 * RWKV_WIDTH], cv_ref)
    lo = shift(lo_ref, mulo_ref[...], clo_ref)

    wpre = w0_ref[...] + _mm(jnp.tanh(lo), w2_ref[...])
    lw = -jnp.exp(-_softplus(-wpre) - 0.5)
    a = _sigmoid(a0_ref[...] + _mm(lo, a2_ref[...]))
    g = _mm(_sigmoid(lo), g2_ref[...])

    ti = lax.broadcasted_iota(jnp.int32, (L, L), 0)
    tj = lax.broadcasted_iota(jnp.int32, (L, L), 1)
    tri = jnp.where(tj <= ti, 1.0, 0.0).astype(BF16)
    cl = _mm_exact_rhs(tri, lw)
    cl_last = cl[L - 1:L, :]
    e_pos = jnp.exp(cl)
    e_neg = jnp.exp(-cl)
    e_prev = jnp.exp(cl - lw)
    e_end = jnp.exp(cl_last - cl)
    p_last = jnp.exp(cl_last)

    kk = k * kk_ref[...]
    k2 = k * (1.0 + (a - 1.0) * ka_ref[...])
    f_b = a * e_neg
    f_bend = a * e_end
    rk = rk_ref[...]
    gng = gng_ref[...]
    gnb = gnb_ref[...]

    lane = lax.broadcasted_iota(jnp.int32, (L, LANES), 1)
    lo_half = lane < RWKV_HEAD
    si = lax.broadcasted_iota(jnp.int32, (H2, H2), 0)
    sj = lax.broadcasted_iota(jnp.int32, (H2, H2), 1)
    same = (si // L) == (sj // L)
    m_strict = same & (sj < si)
    m_incl = same & (sj <= si)
    eye = si == sj
    own = (si // L) == (sj // RWKV_HEAD)

    def stack(x):
        return jnp.concatenate([jnp.where(lo_half, x, 0.0), jnp.where(lo_half, 0.0, x)], axis=0)

    def rep(x):
        return jnp.concatenate([x, x], axis=0)

    pairs = range(n_pair)
    sls = [slice(p * LANES, (p + 1) * LANES) for p in pairs]
    at, bt, bend, kt, kend, rt, v_s, bonus = [], [], [], [], [], [], [], []
    for sl in sls:
        kk_s = stack(kk[:, sl])
        nrm = jnp.sqrt(jnp.sum(kk_s * kk_s, axis=-1, keepdims=True))
        kkn = kk_s / jnp.maximum(nrm, 1e-12)
        at.append(-kkn * rep(e_prev[:, sl]))
        bt.append(kkn * rep(f_b[:, sl]))
        bend.append(kkn * rep(f_bend[:, sl]))
        k2_s = stack(k2[:, sl])
        kt.append(k2_s * rep(e_neg[:, sl]))
        kend.append(k2_s * rep(e_end[:, sl]))
        r_s = stack(r[:, sl])
        rt.append(r_s * rep(e_pos[:, sl]))
        vs = stack(v[:, sl])
        v_s.append(vs)
        bonus.append(jnp.sum(r_s * k2_s * rk[:, sl], axis=-1, keepdims=True) * vs)

    gm = [_mm_nt(jnp.concatenate([at[p], rt[p]], axis=0), jnp.concatenate([bt[p], kt[p]], axis=0))
          for p in pairs]
    a_ab = [jnp.where(m_strict, gm[p][:H2, :H2], 0.0) for p in pairs]
    a_ak = [jnp.where(m_strict, gm[p][:H2, H2:], 0.0) for p in pairs]
    a_rb = [jnp.where(m_incl, gm[p][H2:, :H2], 0.0) for p in pairs]
    a_rk = [jnp.where(m_incl, gm[p][H2:, H2:], 0.0) for p in pairs]

    ident = jnp.where(eye, 1.0, 0.0)
    tinv = [ident + a_ab[p] for p in pairs]
    pw = a_ab
    akv = [_mm(a_ak[p], v_s[p]) for p in pairs]
    ark_v = [_mm(a_rk[p], v_s[p]) for p in pairs]
    kend_v = [_mm_tn(kend[p], v_s[p]) for p in pairs]
    for _ in range(max(1, (L - 1).bit_length() - 1)):
        pw = [_mm(pw[p], pw[p]) for p in pairs]
        tinv = [tinv[p] + _mm(tinv[p], pw[p]) for p in pairs]

    tx = [_mm(tinv[p], jnp.concatenate([at[p], akv[p]], axis=1)) for p in pairs]
    y2 = [_mm(a_rb[p], tx[p]) for p in pairs]
    btx = [_mm_tn(bend[p], tx[p]) for p in pairs]
    s0 = [st_ref[p] for p in pairs]
    y = [_mm(rt[p] + y2[p][:, :LANES], s0[p]) + (y2[p][:, LANES:] + ark_v[p]) for p in pairs]
    for p in pairs:
        m_mat = jnp.where(eye, jnp.broadcast_to(p_last[:, sls[p]], (H2, LANES)), 0.0) + btx[p][:, :LANES]
        st_ref[p] = _mm(m_mat, s0[p]) + (btx[p][:, LANES:] + kend_v[p])

    for p in pairs:
        sl = sls[p]
        mean = jnp.sum(y[p], axis=-1, keepdims=True) * (1.0 / RWKV_HEAD)
        dlt = jnp.where(own, y[p] - mean, 0.0)
        var = jnp.sum(dlt * dlt, axis=-1, keepdims=True) * (1.0 / RWKV_HEAD)
        yn = dlt * lax.rsqrt(var + RWKV_GN_EPS) * gng[:, sl] + jnp.where(own, gnb[:, sl], 0.0)
        ys = yn + bonus[p]
        o_ref[:, sl] = ((ys[:L] + ys[L:]) * g[:, sl]).astype(o_ref.dtype)


def _rwkv(proj, bsz, seq, mu_rkv, mu_lo, w0, a0, k_k, k_a, r_k, gn_g, gn_b, w2p, a2p, g2p):
    L = RWKV_CHUNK
    nc = seq // L
    W = RWKV_WIDTH
    row = lambda n: pl.BlockSpec((1, n), lambda b, c: (0, 0))
    full = lambda a: pl.BlockSpec(a.shape, lambda b, c: (0, 0))
    lo_blk = proj.shape[1] // LORA_PAD - 1
    v1 = lambda a: a.reshape(1, -1)
    return pl.pallas_call(
        _rwkv_kernel,
        grid=(bsz, nc),
        in_specs=[pl.BlockSpec((L, W), lambda b, c: (b * nc + c, 0)),
                  pl.BlockSpec((L, W), lambda b, c: (b * nc + c, 1)),
                  pl.BlockSpec((L, W), lambda b, c: (b * nc + c, 2)),
                  pl.BlockSpec((L, LORA_PAD), lambda b, c: (b * nc + c, lo_blk)),
                  row(3 * W), row(LORA_PAD), row(W), row(W), row(W), row(W), row(W), row(W), row(W),
                  full(w2p), full(a2p), full(g2p)],
        out_specs=pl.BlockSpec((L, W), lambda b, c: (b * nc + c, 0)),
        out_shape=jax.ShapeDtypeStruct((bsz * seq, W), BF16),
        scratch_shapes=[pltpu.VMEM((W // LANES, LANES, LANES), F32),
                        pltpu.VMEM((1, W), F32), pltpu.VMEM((1, W), F32), pltpu.VMEM((1, W), F32),
                        pltpu.VMEM((1, LORA_PAD), F32)],
        compiler_params=_cp(("arbitrary", "arbitrary")),
        name="rwkv7",
    )(proj, proj, proj, proj, v1(mu_rkv), v1(mu_lo), v1(w0), v1(a0), v1(k_k), v1(k_a), v1(r_k), v1(gn_g),
      v1(gn_b), w2p, a2p, g2p)


def _hgrn_kernel(q_ref, f_ref, i_ref, og_ref, lb_ref, gn_ref, o_ref, st_ref):
    n = HGRN_STEP
    blk = GLA_BLOCK
    nb = n // blk
    n_head = HGRN_WIDTH // HGRN_HEAD

    @pl.when(pl.program_id(1) == 0)
    def _():
        st_ref[...] = jnp.zeros_like(st_ref)

    lb = lb_ref[...]
    qf = q_ref[...]
    q = qf * _sigmoid(qf)
    forget = lb + (1.0 - lb) * _sigmoid(f_ref[...])
    log_f = jnp.log(forget)
    kx = 1.0 - forget
    vv = i_ref[...]
    og = og_ref[...]

    ti = lax.broadcasted_iota(jnp.int32, (n, n), 0)
    tj = lax.broadcasted_iota(jnp.int32, (n, n), 1)
    same = (ti // blk) == (tj // blk)
    causal = same & (tj <= ti)
    b = _mm_exact_rhs(jnp.where(causal, 1.0, 0.0).astype(BF16), log_f)
    b_last = _mm_exact_rhs(jnp.where(same, 1.0, 0.0).astype(BF16), log_f)
    q_dec = q * jnp.exp(b)
    k_dec = kx * jnp.exp(-b)
    k_end = kx * jnp.exp(b_last - b)
    d_end = jnp.exp(b_last)
    rows = lax.broadcasted_iota(jnp.int32, (n, HGRN_HEAD), 0)
    gn = gn_ref[...]

    heads = range(n_head)
    sls = [slice(h * HGRN_HEAD, (h + 1) * HGRN_HEAD) for h in heads]
    attn = [jnp.where(causal, _mm_nt(q_dec[:, sl], k_dec[:, sl]), 0.0) for sl in sls]
    o_intra = [_mm(attn[h], vv[:, sls[h]]) for h in heads]
    vt = [vv[:, sl].T for sl in sls]
    blk_mask = [(rows // blk) == c for c in range(nb)]
    kv_t = [[_mm(vt[h], jnp.where(blk_mask[c], k_end[:, sls[h]], 0.0)) for c in range(nb)]
            for h in heads]
    st = [[st_ref[h]] for h in heads]
    for h in heads:
        for c in range(nb):
            st[h].append(st[h][c] * d_end[c * blk:c * blk + 1, sls[h]] + kv_t[h][c])
        st_ref[h] = st[h][nb]
    for h in heads:
        sl = sls[h]
        inter = [_mm_nt(q_dec[c * blk:(c + 1) * blk, sl], st[h][c]) for c in range(nb)]
        o = o_intra[h] + jnp.concatenate(inter, axis=0)
        o = _rms(o, NORM_EPS) * gn
        ogh = og[:, sl]
        o_ref[:, sl] = (o * (ogh * _sigmoid(ogh))).astype(o_ref.dtype)


def _hgrn(proj, bsz, seq, lb, gn_g):
    n = HGRN_STEP
    ns = seq // n
    W = HGRN_WIDTH
    col = lambda j: pl.BlockSpec((n, W), lambda b, c: (b * ns + c, j))
    return pl.pallas_call(
        _hgrn_kernel,
        grid=(bsz, ns),
        in_specs=[col(3), col(4), col(5), col(6),
                  pl.BlockSpec((1, W), lambda b, c: (0, 0)),
                  pl.BlockSpec((1, HGRN_HEAD), lambda b, c: (0, 0))],
        out_specs=pl.BlockSpec((n, W), lambda b, c: (b * ns + c, 0)),
        out_shape=jax.ShapeDtypeStruct((bsz * seq, W), BF16),
        scratch_shapes=[pltpu.VMEM((W // HGRN_HEAD, HGRN_HEAD, HGRN_HEAD), F32)],
        compiler_params=_cp(("arbitrary", "arbitrary")),
        name="hgrn2",
    )(proj, proj, proj, proj, lb.reshape(1, W), gn_g.reshape(1, HGRN_HEAD))


def _merge_kernel(ya_ref, yb_ref, pa_ref, pb_ref, ga_ref, gb_ref, o_ref):
    ma = jnp.dot(ya_ref[...], pa_ref[...], preferred_element_type=F32)
    mb = jnp.dot(yb_ref[...], pb_ref[...], preferred_element_type=F32)
    o_ref[...] = (_sigmoid(ga_ref[...]) * ma + _sigmoid(gb_ref[...]) * mb).astype(o_ref.dtype)


def _merge(ya, yb, pa, pb, proj):
    t, w = ya.shape
    d = pa.shape[1]
    tm, tn = min(PROJ_TM, t), PROJ_TN
    ga0 = (3 * RWKV_WIDTH + 4 * HGRN_WIDTH) // tn
    gb0 = ga0 + d // tn
    return pl.pallas_call(
        _merge_kernel,
        grid=(t // tm, d // tn),
        in_specs=[pl.BlockSpec((tm, w), lambda i, j: (i, 0)),
                  pl.BlockSpec((tm, w), lambda i, j: (i, 0)),
                  pl.BlockSpec((w, tn), lambda i, j: (0, j)),
                  pl.BlockSpec((w, tn), lambda i, j: (0, j)),
                  pl.BlockSpec((tm, tn), lambda i, j: (i, ga0 + j)),
                  pl.BlockSpec((tm, tn), lambda i, j: (i, gb0 + j))],
        out_specs=pl.BlockSpec((tm, tn), lambda i, j: (i, j)),
        out_shape=jax.ShapeDtypeStruct((t, d), BF16),
        compiler_params=_cp(("arbitrary", "arbitrary")),
        name="merge",
    )(ya, yb, pa, pb, proj, proj)


def _outproj_kernel(m_ref, w_ref, x_ref, g_ref, o_ref):
    mix = jnp.dot(m_ref[...], w_ref[...], preferred_element_type=F32)
    o_ref[...] = x_ref[...] + g_ref[0] * mix


def _outproj(merged, w, xf, g1, seq):
    t, d = merged.shape
    tm, tn = min(PROJ_TM, seq), PROJ_TN
    per_b = seq // tm
    return pl.pallas_call(
        _outproj_kernel,
        grid=(t // tm, d // tn),
        in_specs=[pl.BlockSpec((tm, d), lambda i, j: (i, 0)),
                  pl.BlockSpec((d, tn), lambda i, j: (0, j)),
                  pl.BlockSpec((tm, tn), lambda i, j: (i, j)),
                  pl.BlockSpec((1, 1, tn), lambda i, j: (i // per_b, 0, j))],
        out_specs=pl.BlockSpec((tm, tn), lambda i, j: (i, j)),
        out_shape=jax.ShapeDtypeStruct((t, d), F32),
        compiler_params=_cp(("arbitrary", "arbitrary")),
        name="outproj",
    )(merged, w, xf, g1)


def _route_kernel(x_ref, g_ref, sc_ref, sh_ref, rw_ref, rb_ref, h_ref, idx_ref, gate_ref, rank_ref, cnt_ref,
                  carry_ref):
    tb = x_ref.shape[0]

    @pl.when(pl.program_id(0) == 0)
    def _():
        carry_ref[...] = jnp.zeros_like(carry_ref)

    h = (_rms(x_ref[...], NORM_EPS) * g_ref[...]) * (1.0 + sc_ref[0]) + sh_ref[0]
    h_ref[...] = h
    logits = jnp.dot(h, rw_ref[...], preferred_element_type=F32, precision=lax.Precision.HIGHEST) + rb_ref[...]

    lane = lax.broadcasted_iota(jnp.int32, logits.shape, 1).astype(F32)
    work = logits
    vals, idxs, sels = [], [], []
    for _ in range(TOP_K):
        m = jnp.max(work, axis=-1, keepdims=True)
        idx = jnp.min(jnp.where(work == m, lane, float(N_EXPERTS)), axis=-1, keepdims=True)
        sel = lane == idx
        vals.append(m)
        idxs.append(idx)
        sels.append(sel)
        work = jnp.where(sel, -jnp.inf, work)
    exps = [jnp.exp(v - vals[0]) for v in vals]
    den = exps[0] + exps[1] + exps[2] + exps[3]

    onehot = jnp.zeros(logits.shape, F32)
    for sel in sels:
        onehot = onehot + jnp.where(sel, 1.0, 0.0)
    ti = lax.broadcasted_iota(jnp.int32, (tb, tb), 0)
    tj = lax.broadcasted_iota(jnp.int32, (tb, tb), 1)
    below = jnp.where(tj < ti, 1.0, 0.0).astype(BF16)
    before = jnp.dot(below, onehot.astype(BF16), preferred_element_type=F32) + carry_ref[...]
    carry_ref[...] = carry_ref[...] + jnp.sum(onehot, axis=0, keepdims=True)
    cnt_ref[...] = carry_ref[...]

    out_lane = lax.broadcasted_iota(jnp.int32, (tb, LANES), 1)
    idx_out = jnp.zeros((tb, LANES), jnp.int32)
    rank_out = jnp.zeros((tb, LANES), jnp.int32)
    gate_out = jnp.zeros((tb, LANES), F32)
    for j in range(TOP_K):
        rank = jnp.sum(jnp.where(sels[j], before, 0.0), axis=-1, keepdims=True)
        idx_out = jnp.where(out_lane == j, idxs[j].astype(jnp.int32), idx_out)
        rank_out = jnp.where(out_lane == j, rank.astype(jnp.int32), rank_out)
        gate_out = jnp.where(out_lane == j, exps[j] / den, gate_out)
    idx_ref[...] = idx_out
    rank_ref[...] = rank_out
    gate_ref[...] = gate_out


def _route(x1, g, sc, sh, rw, rb, seq):
    t, d = x1.shape
    tb = min(ROUTE_TB, seq)
    per_b = seq // tb
    e = rw.shape[1]
    wide = lambda dt: jax.ShapeDtypeStruct((t, LANES), dt)
    return pl.pallas_call(
        _route_kernel,
        grid=(t // tb,),
        in_specs=[pl.BlockSpec((tb, d), lambda i: (i, 0)),
                  pl.BlockSpec((1, d), lambda i: (0, 0)),
                  pl.BlockSpec((1, 1, d), lambda i: (i // per_b, 0, 0)),
                  pl.BlockSpec((1, 1, d), lambda i: (i // per_b, 0, 0)),
                  pl.BlockSpec((d, e), lambda i: (0, 0)),
                  pl.BlockSpec((1, e), lambda i: (0, 0))],
        out_specs=[pl.BlockSpec((tb, d), lambda i: (i, 0)),
                   pl.BlockSpec((tb, LANES), lambda i: (i, 0)),
                   pl.BlockSpec((tb, LANES), lambda i: (i, 0)),
                   pl.BlockSpec((tb, LANES), lambda i: (i, 0)),
                   pl.BlockSpec((1, e), lambda i: (0, 0))],
        out_shape=[jax.ShapeDtypeStruct((t, d), F32), wide(jnp.int32), wide(F32), wide(jnp.int32),
                   jax.ShapeDtypeStruct((1, e), F32)],
        scratch_shapes=[pltpu.VMEM((1, e), F32)],
        compiler_params=_cp(("arbitrary",)),
        name="route",
    )(x1, g.reshape(1, d), sc, sh, rw, rb.reshape(1, e))


def _scatter_kernel(dest_ref, h_ref, xs_in_ref, xs_ref, sem):
    del xs_in_ref
    tb = h_ref.shape[0]
    base = pl.program_id(0) * tb

    def copy(t, j):
        d = dest_ref[(base + t) * TOP_K + j]
        return pltpu.make_async_copy(h_ref.at[pl.ds(t, 1), :], xs_ref.at[pl.ds(d, 1), :], sem)

    def start(t, carry):
        for j in range(TOP_K):
            copy(t, j).start()
        return carry

    def wait(t, carry):
        for j in range(TOP_K):
            copy(t, j).wait()
        return carry

    lax.fori_loop(0, tb, start, 0)
    lax.fori_loop(0, tb, wait, 0)


def _scatter_rows(dest_flat, h2, n_rows):
    t, d = h2.shape
    tb = min(SCATTER_TB, t)
    xs0 = jnp.zeros((n_rows, d), F32)
    return pl.pallas_call(
        _scatter_kernel,
        grid_spec=pltpu.PrefetchScalarGridSpec(
            num_scalar_prefetch=1,
            grid=(t // tb,),
            in_specs=[pl.BlockSpec((tb, d), lambda i, dest: (i, 0)),
                      pl.BlockSpec(memory_space=pl.ANY)],
            out_specs=pl.BlockSpec(memory_space=pl.ANY),
            scratch_shapes=[pltpu.SemaphoreType.DMA(())],
        ),
        out_shape=jax.ShapeDtypeStruct((n_rows, d), F32),
        input_output_aliases={2: 0},
        compiler_params=_cp(("arbitrary",)),
        name="scatter_rows",
    )(dest_flat, h2, xs0)


def _expert_kernel(be_ref, bs_ref, nu_ref, x_ref, wg_ref, wl_ref, bg_ref, bl_ref, wd_ref, bd_ref, o_ref,
                   xb_ref, act_ref):
    del be_ref, bs_ref
    i = pl.program_id(0)
    n = pl.program_id(1)
    nt, _, tn = act_ref.shape
    tnb = wd_ref.shape[2]
    live = i < nu_ref[0]

    @pl.when(live & (n == 0))
    def _():
        xb_ref[...] = x_ref[...].astype(BF16)

    @pl.when(live & (n < nt))
    def _():
        xb = xb_ref[...]
        glu = jnp.dot(xb, wg_ref[0].astype(BF16), preferred_element_type=F32) + bg_ref[0]
        lin = jnp.dot(xb, wl_ref[0].astype(BF16), preferred_element_type=F32) + bl_ref[0]
        glu = jnp.minimum(glu, SWIGLU_LIMIT)
        lin = jnp.clip(lin, -SWIGLU_LIMIT, SWIGLU_LIMIT)
        act_ref[n] = (glu * _sigmoid(SWIGLU_ALPHA * glu) * (lin + 1.0)).astype(BF16)

    for m in range(o_ref.shape[1] // tnb):
        @pl.when(live & (n == nt + m))
        def _():
            wd = wd_ref[0]
            y = jnp.dot(act_ref[0], wd[0:tn].astype(BF16), preferred_element_type=F32)
            for k in range(1, nt):
                y = y + jnp.dot(act_ref[k], wd[k * tn:(k + 1) * tn].astype(BF16), preferred_element_type=F32)
            o_ref[:, m * tnb:(m + 1) * tnb] = y + bd_ref[0]

    @pl.when(jnp.logical_not(live) & (n == 0))
    def _():
        o_ref[...] = jnp.zeros_like(o_ref)


def _experts(blk_expert, blk_src, n_used, xs, w_gu, b_gu, w_down, b_down):
    n_rows, d = xs.shape
    tm, tn, tnb = MOE_TM, MOE_TN, MOE_TNB
    nb = n_rows // tm
    e, _, two_de = w_gu.shape
    de = two_de // 2
    nt = de // tn
    ntb = d // tnb
    b_gu3 = b_gu.reshape(e, 1, two_de)
    b_d3 = b_down.reshape(e, 1, d)
    na = lambda n: jnp.minimum(n, nt - 1)
    nbk = lambda n: jnp.maximum(n - nt, 0)
    return pl.pallas_call(
        _expert_kernel,
        grid_spec=pltpu.PrefetchScalarGridSpec(
            num_scalar_prefetch=3,
            grid=(nb, nt + ntb),
            in_specs=[pl.BlockSpec((tm, d), lambda i, n, be, bs, nu: (bs[i], 0)),
                      pl.BlockSpec((1, d, tn), lambda i, n, be, bs, nu: (be[i], 0, na(n))),
                      pl.BlockSpec((1, d, tn), lambda i, n, be, bs, nu: (be[i], 0, na(n) + nt)),
                      pl.BlockSpec((1, 1, tn), lambda i, n, be, bs, nu: (be[i], 0, na(n))),
                      pl.BlockSpec((1, 1, tn), lambda i, n, be, bs, nu: (be[i], 0, na(n) + nt)),
                      pl.BlockSpec((1, de, tnb), lambda i, n, be, bs, nu: (be[i], 0, nbk(n))),
                      pl.BlockSpec((1, 1, tnb), lambda i, n, be, bs, nu: (be[i], 0, nbk(n)))],
            out_specs=pl.BlockSpec((tm, d), lambda i, n, be, bs, nu: (i, 0)),
            scratch_shapes=[pltpu.VMEM((tm, d), BF16), pltpu.VMEM((nt, tm, tn), BF16)],
        ),
        out_shape=jax.ShapeDtypeStruct((n_rows, d), F32),
        compiler_params=_cp(("arbitrary", "arbitrary")),
        name="experts",
    )(blk_expert, blk_src, n_used, xs, w_gu, w_gu, b_gu3, b_gu3, w_down, b_d3)


def _combine_kernel(dest_ref, ys_ref, gate_ref, x_ref, g2_ref, fg_ref, o_ref, buf_ref, sem):
    tb = x_ref.shape[0]
    base = pl.program_id(0) * tb

    def copy(t, j):
        d = dest_ref[(base + t) * TOP_K + j]
        return pltpu.make_async_copy(ys_ref.at[pl.ds(d, 1), :], buf_ref.at[j, pl.ds(t, 1), :], sem)

    def start(t, carry):
        for j in range(TOP_K):
            copy(t, j).start()
        return carry

    def wait(t, carry):
        for j in range(TOP_K):
            copy(t, j).wait()
        return carry

    lax.fori_loop(0, tb, start, 0)
    lax.fori_loop(0, tb, wait, 0)

    gate = gate_ref[...]
    ffn = gate[:, 0:1] * buf_ref[0]
    for j in range(1, TOP_K):
        ffn = ffn + gate[:, j:j + 1] * buf_ref[j]
    x2 = x_ref[...] + g2_ref[0] * ffn
    o_ref[...] = _rms(x2, NORM_EPS) * fg_ref[...]


def _combine(dest_flat, ys, gates, x1, g2, fg, seq):
    t, d = x1.shape
    tb = min(COMBINE_TB, seq)
    per_b = seq // tb
    return pl.pallas_call(
        _combine_kernel,
        grid_spec=pltpu.PrefetchScalarGridSpec(
            num_scalar_prefetch=1,
            grid=(t // tb,),
            in_specs=[pl.BlockSpec(memory_space=pl.ANY),
                      pl.BlockSpec((tb, LANES), lambda i, dest: (i, 0)),
                      pl.BlockSpec((tb, d), lambda i, dest: (i, 0)),
                      pl.BlockSpec((1, 1, d), lambda i, dest: (i // per_b, 0, 0)),
                      pl.BlockSpec((1, d), lambda i, dest: (0, 0))],
            out_specs=pl.BlockSpec((tb, d), lambda i, dest: (i, 0)),
            scratch_shapes=[pltpu.VMEM((TOP_K, tb, d), F32), pltpu.SemaphoreType.DMA(())],
        ),
        out_shape=jax.ShapeDtypeStruct((t, d), F32),
        compiler_params=_cp(("arbitrary",)),
        name="combine",
    )(dest_flat, ys, gates, x1, g2, fg.reshape(1, d))


def _moe(x1, g, sc2, sh2, g2, router_w, router_b, w_gu, b_gu, w_down, b_down, final_g, seq):
    t, d = x1.shape
    a = t * TOP_K
    tm = MOE_TM
    nb = -(-a // tm) + N_EXPERTS
    h2, idx_w, gate_w, rank_w, counts = _route(x1, g, sc2, sh2, router_w, router_b, seq)
    idx = idx_w[:, :TOP_K]
    rank = rank_w[:, :TOP_K]
    cnt = counts.reshape(-1).astype(jnp.int32)
    padded = ((cnt + tm - 1) // tm) * tm
    pad_end = jnp.cumsum(padded)
    pad_start = pad_end - padded
    dest = (pad_start[idx] + rank).reshape(a).astype(jnp.int32)
    n_used = (pad_end[-1] // tm).astype(jnp.int32)
    blk = jnp.arange(nb, dtype=jnp.int32)
    blk_src = jnp.minimum(blk, n_used - 1)
    blk_expert = jnp.sum((pad_end[None, :] <= (blk_src * tm)[:, None]).astype(jnp.int32), axis=1)
    blk_expert = jnp.minimum(blk_expert, N_EXPERTS - 1)
    xs = _scatter_rows(dest, h2, nb * tm)
    ys = _experts(blk_expert, blk_src, n_used.reshape(1), xs, w_gu, b_gu, w_down, b_down)
    return _combine(dest, ys, gate_w, x1, g2, final_g, seq)


def kernel(x, c, ada_w, ada_b, norm1_g, norm2_g, w_in, rwkv_mu, rwkv_w0, rwkv_w2, rwkv_a0, rwkv_a2, rwkv_g2,
           rwkv_k_k, rwkv_k_a, rwkv_r_k, rwkv_gn_g, rwkv_gn_b, hgrn_lb_logits, hgrn_gn_g, proj_a, proj_b, w_out,
           router_w, router_b, exp_w_gate_up, exp_b_gate_up, exp_w_down, exp_b_down, final_norm_g):
    bsz, seq, d = x.shape
    depth = ada_w.shape[0]
    lower_bounds = jnp.cumsum(jax.nn.softmax(hgrn_lb_logits.astype(F32), axis=0), axis=0)
    xf = x.reshape(bsz * seq, d)
    rkv = 3 * RWKV_WIDTH
    rwkv_cols = rkv + LORA_COLS
    hg_gate = 4 * HGRN_WIDTH + 2 * d
    for l in range(depth):
        mod = _adaln(c, ada_w[l], ada_b[l])
        sh1, sc1, g1, sh2, sc2, g2 = [m.reshape(bsz, 1, d) for m in jnp.split(mod, 6, axis=-1)]

        wl = w_in[l]
        w_cat = jnp.concatenate([wl[:, :rkv], wl[:, rwkv_cols:rwkv_cols + hg_gate], wl[:, rkv:rwkv_cols],
                                 jnp.zeros((d, LORA_PAD - LORA_COLS), wl.dtype)], axis=1).astype(BF16)
        proj = _inproj(xf, norm1_g[l], sc1, sh1, w_cat, seq)

        mu = rwkv_mu[l]
        mu_lo = jnp.pad(mu[rkv:], (0, LORA_PAD - LORA_COLS))
        pad_rows = lambda w, r0: jnp.zeros((LORA_PAD, RWKV_WIDTH), F32).at[r0:r0 + w.shape[0]].set(w).astype(BF16)
        ya = _rwkv(proj, bsz, seq, mu[:rkv], mu_lo, rwkv_w0[l], rwkv_a0[l], rwkv_k_k[l], rwkv_k_a[l],
                   rwkv_r_k[l].reshape(-1), rwkv_gn_g[l], rwkv_gn_b[l],
                   pad_rows(rwkv_w2[l], 0), pad_rows(rwkv_a2[l], DECAY_LORA),
                   pad_rows(rwkv_g2[l], DECAY_LORA + AAA_LORA))
        yb = _hgrn(proj, bsz, seq, lower_bounds[l], hgrn_gn_g[l])
        merged = _merge(ya, yb, proj_a[l].astype(BF16), proj_b[l].astype(BF16), proj)
        x1 = _outproj(merged, w_out[l].astype(BF16), xf, g1, seq)
        assert depth == 1
        out = _moe(x1, norm2_g[l], sc2, sh2, g2, router_w[l], router_b[l], exp_w_gate_up[l], exp_b_gate_up[l],
                   exp_w_down[l], exp_b_down[l], final_norm_g, seq)
    return out.reshape(bsz, seq, d)
```

```python
import functools

import jax
import jax.numpy as jnp
from jax import lax
from jax.experimental import pallas as pl
from jax.experimental.pallas import tpu as pltpu

F32 = jnp.float32
BF16 = jnp.bfloat16

D_MODEL = 2048
RWKV_WIDTH = 1024
RWKV_HEAD = 64
DECAY_LORA = 64
AAA_LORA = 64
GATE_LORA = 160
LORA_COLS = DECAY_LORA + AAA_LORA + GATE_LORA
LORA_PAD = 512
RWKV_GN_EPS = 64e-5
HGRN_WIDTH = 1024
HGRN_HEAD = 128
GLA_BLOCK = 16
N_EXPERTS = 32
TOP_K = 4
D_EXPERT = 2048
SWIGLU_ALPHA = 1.702
SWIGLU_LIMIT = 7.0
NORM_EPS = 1e-5

LANES = 128
VMEM_LIMIT = 56 * 1024 * 1024

RWKV_CHUNK = 64
HGRN_STEP = 128
PROJ_TM = 1024
PROJ_TN = 512
MOE_TM = 512
MOE_TN = 256
MOE_TNB = 512
MOE_RING = 4
ROUTE_TB = 256
SCATTER_TB = 256
COMBINE_TB = 128
ROW_DMA_UNROLL = 8


def _cp(sem, vmem=VMEM_LIMIT):
    return pltpu.CompilerParams(dimension_semantics=sem, vmem_limit_bytes=vmem)


def _mm(a, b):
    return jnp.dot(a.astype(BF16), b.astype(BF16), preferred_element_type=F32)


def _mm_nt(a, b):
    return lax.dot_general(a.astype(BF16), b.astype(BF16), (((1,), (1,)), ((), ())),
                           preferred_element_type=F32)


def _mm_tn(a, b):
    return lax.dot_general(a.astype(BF16), b.astype(BF16), (((0,), (0,)), ((), ())),
                           preferred_element_type=F32)


def _split3(x):
    hi = x.astype(BF16)
    r1 = x - hi.astype(F32)
    mid = r1.astype(BF16)
    lo = (r1 - mid.astype(F32)).astype(BF16)
    return hi, mid, lo


def _mm_exact_rhs(m_bf16, x):
    hi, mid, lo = _split3(x)
    d = lambda p: jnp.dot(m_bf16, p, preferred_element_type=F32)
    return d(hi) + d(mid) + d(lo)


def _sigmoid(x):
    return 1.0 / (1.0 + jnp.exp(-x))


def _softplus(x):
    return jnp.maximum(x, 0.0) + jnp.log(1.0 + jnp.exp(-jnp.abs(x)))


def _rms(x, eps):
    return x * lax.rsqrt(jnp.mean(x * x, axis=-1, keepdims=True) + eps)


def _adaln_kernel(c_ref, w_ref, b_ref, o_ref):
    c = c_ref[...]
    c_act = c * _sigmoid(c)
    o_ref[...] = jnp.dot(c_act, w_ref[...], preferred_element_type=F32,
                         precision=lax.Precision.HIGHEST) + b_ref[...]


def _adaln(c, w, b):
    bsz, d = c.shape
    n = w.shape[1]
    tn = 1024
    return pl.pallas_call(
        _adaln_kernel,
        grid=(n // tn,),
        in_specs=[pl.BlockSpec((bsz, d), lambda j: (0, 0)),
                  pl.BlockSpec((d, tn), lambda j: (0, j)),
                  pl.BlockSpec((1, tn), lambda j: (0, j))],
        out_specs=pl.BlockSpec((bsz, tn), lambda j: (0, j)),
        out_shape=jax.ShapeDtypeStruct((bsz, n), F32),
        compiler_params=_cp(("arbitrary",)),
        name="adaln",
    )(c, w, b.reshape(1, n))


def _inproj_kernel(x_ref, g_ref, sc_ref, sh_ref, w_ref, o_ref, h_ref):
    @pl.when(pl.program_id(1) == 0)
    def _():
        y = _rms(x_ref[...], NORM_EPS) * g_ref[...]
        h_ref[...] = (y * (1.0 + sc_ref[0]) + sh_ref[0]).astype(BF16)

    o_ref[...] = jnp.dot(h_ref[...], w_ref[...], preferred_element_type=F32)


def _inproj(xf, g, sc, sh, w, seq):
    t, d = xf.shape
    n = w.shape[1]
    tm = min(PROJ_TM, seq)
    tn = PROJ_TN
    per_b = seq // tm
    return pl.pallas_call(
        _inproj_kernel,
        grid=(t // tm, n // tn),
        in_specs=[pl.BlockSpec((tm, d), lambda i, j: (i, 0)),
                  pl.BlockSpec((1, d), lambda i, j: (0, 0)),
                  pl.BlockSpec((1, 1, d), lambda i, j: (i // per_b, 0, 0)),
                  pl.BlockSpec((1, 1, d), lambda i, j: (i // per_b, 0, 0)),
                  pl.BlockSpec((d, tn), lambda i, j: (0, j))],
        out_specs=pl.BlockSpec((tm, tn), lambda i, j: (i, j)),
        out_shape=jax.ShapeDtypeStruct((t, n), F32),
        scratch_shapes=[pltpu.VMEM((tm, d), BF16)],
        compiler_params=_cp(("arbitrary", "arbitrary")),
        name="inproj",
    )(xf, g.reshape(1, d), sc, sh, w)


def _rwkv_kernel(r_ref, k_ref, v_ref, lo_ref, mu_ref, mulo_ref, w0_ref, a0_ref, kk_ref, ka_ref, rk_ref,
                 gng_ref, gnb_ref, w2_ref, a2_ref, g2_ref, o_ref,
                 st_ref, cr_ref, ck_ref, cv_ref, clo_ref):
    L = RWKV_CHUNK
    H2 = 2 * L
    n_pair = RWKV_WIDTH // LANES

    @pl.when(pl.program_id(1) == 0)
    def _():
        st_ref[...] = jnp.zeros_like(st_ref)
        cr_ref[...] = jnp.zeros_like(cr_ref)
        ck_ref[...] = jnp.zeros_like(ck_ref)
        cv_ref[...] = jnp.zeros_like(cv_ref)
        clo_ref[...] = jnp.zeros_like(clo_ref)

    def shift(x_ref, mu, carry_ref):
        x = x_ref[...]
        rows = lax.broadcasted_iota(jnp.int32, x.shape, 0)
        prev = jnp.where(rows == 0, carry_ref[...], pltpu.roll(x, 1, axis=0))
        carry_ref[...] = x[L - 1:L, :]
        return x + (prev - x) * mu

    mu = mu_ref[...]
    r = shift(r_ref, mu[:, 0:RWKV_WIDTH], cr_ref)
    k = shift(k_ref, mu[:, RWKV_WIDTH:2 * RWKV_WIDTH], ck_ref)
    v = shift(v_ref, mu[:, 2 * RWKV_WIDTH:3 * RWKV_WIDTH], cv_ref)
    lo = shift(lo_ref, mulo_ref[...], clo_ref)

    wpre = w0_ref[...] + _mm(jnp.tanh(lo), w2_ref[...])
    lw = -jnp.exp(-_softplus(-wpre) - 0.5)
    a = _sigmoid(a0_ref[...] + _mm(lo, a2_ref[...]))
    g = _mm(_sigmoid(lo), g2_ref[...])

    ti = lax.broadcasted_iota(jnp.int32, (L, L), 0)
    tj = lax.broadcasted_iota(jnp.int32, (L, L), 1)
    tri = jnp.where(tj <= ti, 1.0, 0.0).astype(BF16)
    cl = _mm_exact_rhs(tri, lw)
    cl_last = cl[L - 1:L, :]
    e_pos = jnp.exp(cl)
    e_neg = jnp.exp(-cl)
    e_prev = jnp.exp(cl - lw)
    e_end = jnp.exp(cl_last - cl)
    p_last = jnp.exp(cl_last)

    kk = k * kk_ref[...]
    k2 = k * (1.0 + (a - 1.0) * ka_ref[...])
    f_b = a * e_neg
    f_bend = a * e_end
    rk = rk_ref[...]
    gng = gng_ref[...]
    gnb = gnb_ref[...]

    lane = lax.broadcasted_iota(jnp.int32, (L, LANES), 1)
    lo_half = lane < RWKV_HEAD
    si = lax.broadcasted_iota(jnp.int32, (H2, H2), 0)
    sj = lax.broadcasted_iota(jnp.int32, (H2, H2), 1)
    same = (si // L) == (sj // L)
    m_strict = same & (sj < si)
    m_incl = same & (sj <= si)
    eye = si == sj
    own = (si // L) == (sj // RWKV_HEAD)

    def stack(x):
        return jnp.concatenate([jnp.where(lo_half, x, 0.0), jnp.where(lo_half, 0.0, x)], axis=0)

    def rep(x):
        return jnp.concatenate([x, x], axis=0)

    pairs = range(n_pair)
    sls = [slice(p * LANES, (p + 1) * LANES) for p in pairs]
    at, bt, bend, kt, kend, rt, v_s, bonus = [], [], [], [], [], [], [], []
    for sl in sls:
        kk_s = stack(kk[:, sl])
        nrm = jnp.sqrt(jnp.sum(kk_s * kk_s, axis=-1, keepdims=True))
        kkn = kk_s / jnp.maximum(nrm, 1e-12)
        at.append(-kkn * rep(e_prev[:, sl]))
        bt.append(kkn * rep(f_b[:, sl]))
        bend.append(kkn * rep(f_bend[:, sl]))
        k2_s = stack(k2[:, sl])
        kt.append(k2_s * rep(e_neg[:, sl]))
        kend.append(k2_s * rep(e_end[:, sl]))
        r_s = stack(r[:, sl])
        rt.append(r_s * rep(e_pos[:, sl]))
        vs = stack(v[:, sl])
        v_s.append(vs)
        bonus.append(jnp.sum(r_s * k2_s * rk[:, sl], axis=-1, keepdims=True) * vs)

    gm = [_mm_nt(jnp.concatenate([at[p], rt[p]], axis=0), jnp.concatenate([bt[p], kt[p]], axis=0))
          for p in pairs]
    a_ab = [jnp.where(m_strict, gm[p][:H2, :H2], 0.0) for p in pairs]
    a_ak = [jnp.where(m_strict, gm[p][:H2, H2:], 0.0) for p in pairs]
    a_rb = [jnp.where(m_incl, gm[p][H2:, :H2], 0.0) for p in pairs]
    a_rk = [jnp.where(m_incl, gm[p][H2:, H2:], 0.0) for p in pairs]

    ident = jnp.where(eye, 1.0, 0.0)
    tinv = [ident + a_ab[p] for p in pairs]
    pw = a_ab
    akv = [_mm(a_ak[p], v_s[p]) for p in pairs]
    ark_v = [_mm(a_rk[p], v_s[p]) for p in pairs]
    kend_v = [_mm_tn(kend[p], v_s[p]) for p in pairs]
    for _ in range(max(1, (L - 1).bit_length() - 1)):
        pw = [_mm(pw[p], pw[p]) for p in pairs]
        tinv = [tinv[p] + _mm(tinv[p], pw[p]) for p in pairs]

    tx = [_mm(tinv[p], jnp.concatenate([at[p], akv[p]], axis=1)) for p in pairs]
    y2 = [_mm(a_rb[p], tx[p]) for p in pairs]
    btx = [_mm_tn(bend[p], tx[p]) for p in pairs]
    s0 = [st_ref[p] for p in pairs]
    y = [_mm(rt[p] + y2[p][:, :LANES], s0[p]) + (y2[p][:, LANES:] + ark_v[p]) for p in pairs]
    for p in pairs:
        m_mat = jnp.where(eye, jnp.broadcast_to(p_last[:, sls[p]], (H2, LANES)), 0.0) + btx[p][:, :LANES]
        st_ref[p] = _mm(m_mat, s0[p]) + (btx[p][:, LANES:] + kend_v[p])

    for p in pairs:
        sl = sls[p]
        mean = jnp.sum(y[p], axis=-1, keepdims=True) * (1.0 / RWKV_HEAD)
        dlt = jnp.where(own, y[p] - mean, 0.0)
        var = jnp.sum(dlt * dlt, axis=-1, keepdims=True) * (1.0 / RWKV_HEAD)
        yn = dlt * lax.rsqrt(var + RWKV_GN_EPS) * gng[:, sl] + jnp.where(own, gnb[:, sl], 0.0)
        ys = yn + bonus[p]
        o_ref[:, sl] = ((ys[:L] + ys[L:]) * g[:, sl]).astype(o_ref.dtype)


def _rwkv(proj, bsz, seq, mu_rkv, mu_lo, w0, a0, k_k, k_a, r_k, gn_g, gn_b, w2p, a2p, g2p):
    L = RWKV_CHUNK
    nc = seq // L
    W = RWKV_WIDTH
    row = lambda n: pl.BlockSpec((1, n), lambda b, c: (0, 0))
    full = lambda a: pl.BlockSpec(a.shape, lambda b, c: (0, 0))
    lo_blk = proj.shape[1] // LORA_PAD - 1
    v1 = lambda a: a.reshape(1, -1)
    return pl.pallas_call(
        _rwkv_kernel,
        grid=(bsz, nc),
        in_specs=[pl.BlockSpec((L, W), lambda b, c: (b * nc + c, 0)),
                  pl.BlockSpec((L, W), lambda b, c: (b * nc + c, 1)),
                  pl.BlockSpec((L, W), lambda b, c: (b * nc + c, 2)),
                  pl.BlockSpec((L, LORA_PAD), lambda b, c: (b * nc + c, lo_blk)),
                  row(3 * W), row(LORA_PAD), row(W), row(W), row(W), row(W), row(W), row(W), row(W),
                  full(w2p), full(a2p), full(g2p)],
        out_specs=pl.BlockSpec((L, W), lambda b, c: (b * nc + c, 0)),
        out_shape=jax.ShapeDtypeStruct((bsz * seq, W), BF16),
        scratch_shapes=[pltpu.VMEM((W // LANES, LANES, LANES), F32),
                        pltpu.VMEM((1, W), F32), pltpu.VMEM((1, W), F32), pltpu.VMEM((1, W), F32),
                        pltpu.VMEM((1, LORA_PAD), F32)],
        compiler_params=_cp(("arbitrary", "arbitrary")),
        name="rwkv7",
    )(proj, proj, proj, proj, v1(mu_rkv), v1(mu_lo), v1(w0), v1(a0), v1(k_k), v1(k_a), v1(r_k), v1(gn_g),
      v1(gn_b), w2p, a2p, g2p)


def _hgrn_kernel(q_ref, f_ref, i_ref, og_ref, lb_ref, gn_ref, o_ref, st_ref):
    n = HGRN_STEP
    blk = GLA_BLOCK
    nb = n // blk
    n_head = HGRN_WIDTH // HGRN_HEAD

    @pl.when(pl.program_id(1) == 0)
    def _():
        st_ref[...] = jnp.zeros_like(st_ref)

    lb = lb_ref[...]
    qf = q_ref[...]
    q = qf * _sigmoid(qf)
    forget = lb + (1.0 - lb) * _sigmoid(f_ref[...])
    log_f = jnp.log(forget)
    kx = 1.0 - forget
    vv = i_ref[...]
    og = og_ref[...]

    ti = lax.broadcasted_iota(jnp.int32, (n, n), 0)
    tj = lax.broadcasted_iota(jnp.int32, (n, n), 1)
    same = (ti // blk) == (tj // blk)
    causal = same & (tj <= ti)
    b = _mm_exact_rhs(jnp.where(causal, 1.0, 0.0).astype(BF16), log_f)
    b_last = _mm_exact_rhs(jnp.where(same, 1.0, 0.0).astype(BF16), log_f)
    q_dec = q * jnp.exp(b)
    k_dec = kx * jnp.exp(-b)
    k_end = kx * jnp.exp(b_last - b)
    d_end = jnp.exp(b_last)
    rows = lax.broadcasted_iota(jnp.int32, (n, HGRN_HEAD), 0)
    gn = gn_ref[...]

    heads = range(n_head)
    sls = [slice(h * HGRN_HEAD, (h + 1) * HGRN_HEAD) for h in heads]
    attn = [jnp.where(causal, _mm_nt(q_dec[:, sl], k_dec[:, sl]), 0.0) for sl in sls]
    o_intra = [_mm(attn[h], vv[:, sls[h]]) for h in heads]
    vt = [vv[:, sl].T for sl in sls]
    blk_mask = [(rows // blk) == c for c in range(nb)]
    kv_t = [[_mm(vt[h], jnp.where(blk_mask[c], k_end[:, sls[h]], 0.0)) for c in range(nb)]
            for h in heads]
    st = [[st_ref[h]] for h in heads]
    for h in heads:
        for c in range(nb):
            st[h].append(st[h][c] * d_end[c * blk:c * blk + 1, sls[h]] + kv_t[h][c])
        st_ref[h] = st[h][nb]
    for h in heads:
        sl = sls[h]
        inter = [_mm_nt(q_dec[c * blk:(c + 1) * blk, sl], st[h][c]) for c in range(nb)]
        o = o_intra[h] + jnp.concatenate(inter, axis=0)
        o = _rms(o, NORM_EPS) * gn
        ogh = og[:, sl]
        o_ref[:, sl] = (o * (ogh * _sigmoid(ogh))).astype(o_ref.dtype)


def _hgrn(proj, bsz, seq, lb, gn_g):
    n = HGRN_STEP
    ns = seq // n
    W = HGRN_WIDTH
    col = lambda j: pl.BlockSpec((n, W), lambda b, c: (b * ns + c, j))
    return pl.pallas_call(
        _hgrn_kernel,
        grid=(bsz, ns),
        in_specs=[col(3), col(4), col(5), col(6),
                  pl.BlockSpec((1, W), lambda b, c: (0, 0)),
                  pl.BlockSpec((1, HGRN_HEAD), lambda b, c: (0, 0))],
        out_specs=pl.BlockSpec((n, W), lambda b, c: (b * ns + c, 0)),
        out_shape=jax.ShapeDtypeStruct((bsz * seq, W), BF16),
        scratch_shapes=[pltpu.VMEM((W // HGRN_HEAD, HGRN_HEAD, HGRN_HEAD), F32)],
        compiler_params=_cp(("arbitrary", "arbitrary")),
        name="hgrn2",
    )(proj, proj, proj, proj, lb.reshape(1, W), gn_g.reshape(1, HGRN_HEAD))


def _merge_kernel(ya_ref, yb_ref, pa_ref, pb_ref, ga_ref, gb_ref, o_ref):
    ma = jnp.dot(ya_ref[...], pa_ref[...], preferred_element_type=F32)
    mb = jnp.dot(yb_ref[...], pb_ref[...], preferred_element_type=F32)
    o_ref[...] = (_sigmoid(ga_ref[...]) * ma + _sigmoid(gb_ref[...]) * mb).astype(o_ref.dtype)


def _merge(ya, yb, pa, pb, proj):
    t, w = ya.shape
    d = pa.shape[1]
    tm, tn = min(PROJ_TM, t), PROJ_TN
    ga0 = (3 * RWKV_WIDTH + 4 * HGRN_WIDTH) // tn
    gb0 = ga0 + d // tn
    return pl.pallas_call(
        _merge_kernel,
        grid=(t // tm, d // tn),
        in_specs=[pl.BlockSpec((tm, w), lambda i, j: (i, 0)),
                  pl.BlockSpec((tm, w), lambda i, j: (i, 0)),
                  pl.BlockSpec((w, tn), lambda i, j: (0, j)),
                  pl.BlockSpec((w, tn), lambda i, j: (0, j)),
                  pl.BlockSpec((tm, tn), lambda i, j: (i, ga0 + j)),
                  pl.BlockSpec((tm, tn), lambda i, j: (i, gb0 + j))],
        out_specs=pl.BlockSpec((tm, tn), lambda i, j: (i, j)),
        out_shape=jax.ShapeDtypeStruct((t, d), BF16),
        compiler_params=_cp(("arbitrary", "arbitrary")),
        name="merge",
    )(ya, yb, pa, pb, proj, proj)


def _outproj_kernel(m_ref, w_ref, x_ref, g_ref, o_ref):
    mix = jnp.dot(m_ref[...], w_ref[...], preferred_element_type=F32)
    o_ref[...] = x_ref[...] + g_ref[0] * mix


def _outproj(merged, w, xf, g1, seq):
    t, d = merged.shape
    tm, tn = min(PROJ_TM, seq), PROJ_TN
    per_b = seq // tm
    return pl.pallas_call(
        _outproj_kernel,
        grid=(t // tm, d // tn),
        in_specs=[pl.BlockSpec((tm, d), lambda i, j: (i, 0)),
                  pl.BlockSpec((d, tn), lambda i, j: (0, j)),
                  pl.BlockSpec((tm, tn), lambda i, j: (i, j)),
                  pl.BlockSpec((1, 1, tn), lambda i, j: (i // per_b, 0, j))],
        out_specs=pl.BlockSpec((tm, tn), lambda i, j: (i, j)),
        out_shape=jax.ShapeDtypeStruct((t, d), F32),
        compiler_params=_cp(("arbitrary", "arbitrary")),
        name="outproj",
    )(merged, w, xf, g1)


def _route_kernel(x_ref, g_ref, sc_ref, sh_ref, rw_ref, rb_ref, h_ref, idx_ref, gate_ref, rank_ref, cnt_ref,
                  carry_ref):
    tb = x_ref.shape[0]

    @pl.when(pl.program_id(0) == 0)
    def _():
        carry_ref[...] = jnp.zeros_like(carry_ref)

    h = (_rms(x_ref[...], NORM_EPS) * g_ref[...]) * (1.0 + sc_ref[0]) + sh_ref[0]
    h_ref[...] = h
    logits = jnp.dot(h, rw_ref[...], preferred_element_type=F32, precision=lax.Precision.HIGHEST) + rb_ref[...]

    lane = lax.broadcasted_iota(jnp.int32, logits.shape, 1).astype(F32)
    work = logits
    vals, idxs, sels = [], [], []
    for _ in range(TOP_K):
        m = jnp.max(work, axis=-1, keepdims=True)
        idx = jnp.min(jnp.where(work == m, lane, float(N_EXPERTS)), axis=-1, keepdims=True)
        sel = lane == idx
        vals.append(m)
        idxs.append(idx)
        sels.append(sel)
        work = jnp.where(sel, -jnp.inf, work)
    exps = [jnp.exp(v - vals[0]) for v in vals]
    den = exps[0] + exps[1] + exps[2] + exps[3]

    onehot = jnp.zeros(logits.shape, F32)
    for sel in sels:
        onehot = onehot + jnp.where(sel, 1.0, 0.0)
    ti = lax.broadcasted_iota(jnp.int32, (tb, tb), 0)
    tj = lax.broadcasted_iota(jnp.int32, (tb, tb), 1)
    below = jnp.where(tj < ti, 1.0, 0.0).astype(BF16)
    before = jnp.dot(below, onehot.astype(BF16), preferred_element_type=F32) + carry_ref[...]
    carry_ref[...] = carry_ref[...] + jnp.sum(onehot, axis=0, keepdims=True)
    cnt_ref[...] = carry_ref[...]

    out_lane = lax.broadcasted_iota(jnp.int32, (tb, LANES), 1)
    idx_out = jnp.zeros((tb, LANES), jnp.int32)
    rank_out = jnp.zeros((tb, LANES), jnp.int32)
    gate_out = jnp.zeros((tb, LANES), F32)
    for j in range(TOP_K):
        rank = jnp.sum(jnp.where(sels[j], before, 0.0), axis=-1, keepdims=True)
        idx_out = jnp.where(out_lane == j, idxs[j].astype(jnp.int32), idx_out)
        rank_out = jnp.where(out_lane == j, rank.astype(jnp.int32), rank_out)
        gate_out = jnp.where(out_lane == j, exps[j] / den, gate_out)
    idx_ref[...] = idx_out
    rank_ref[...] = rank_out
    gate_ref[...] = gate_out


def _route(x1, g, sc, sh, rw, rb, seq):
    t, d = x1.shape
    tb = min(ROUTE_TB, seq)
    per_b = seq // tb
    e = rw.shape[1]
    wide = lambda dt: jax.ShapeDtypeStruct((t, LANES), dt)
    return pl.pallas_call(
        _route_kernel,
        grid=(t // tb,),
        in_specs=[pl.BlockSpec((tb, d), lambda i: (i, 0)),
                  pl.BlockSpec((1, d), lambda i: (0, 0)),
                  pl.BlockSpec((1, 1, d), lambda i: (i // per_b, 0, 0)),
                  pl.BlockSpec((1, 1, d), lambda i: (i // per_b, 0, 0)),
                  pl.BlockSpec((d, e), lambda i: (0, 0)),
                  pl.BlockSpec((1, e), lambda i: (0, 0))],
        out_specs=[pl.BlockSpec((tb, d), lambda i: (i, 0)),
                   pl.BlockSpec((tb, LANES), lambda i: (i, 0)),
                   pl.BlockSpec((tb, LANES), lambda i: (i, 0)),
                   pl.BlockSpec((tb, LANES), lambda i: (i, 0)),
                   pl.BlockSpec((1, e), lambda i: (0, 0))],
        out_shape=[jax.ShapeDtypeStruct((t, d), F32), wide(jnp.int32), wide(F32), wide(jnp.int32),
                   jax.ShapeDtypeStruct((1, e), F32)],
        scratch_shapes=[pltpu.VMEM((1, e), F32)],
        compiler_params=_cp(("arbitrary",)),
        name="route",
    )(x1, g.reshape(1, d), sc, sh, rw, rb.reshape(1, e))


def _scatter_kernel(dest_ref, h_ref, xs_in_ref, xs_ref, sem):
    del xs_in_ref
    tb = h_ref.shape[0]
    base = pl.program_id(0) * tb

    def copy(t, j):
        d = dest_ref[(base + t) * TOP_K + j]
        return pltpu.make_async_copy(h_ref.at[pl.ds(t, 1), :], xs_ref.at[pl.ds(d, 1), :], sem)

    def start(t, carry):
        for j in range(TOP_K):
            copy(t, j).start()
        return carry

    lax.fori_loop(0, tb, start, 0, unroll=ROW_DMA_UNROLL)
    for j in range(TOP_K):
        pltpu.make_async_copy(h_ref, xs_ref.at[pl.ds(0, tb), :], sem).wait()


def _scatter_rows(dest_flat, h2, n_rows):
    t, d = h2.shape
    tb = min(SCATTER_TB, t)
    xs0 = jnp.zeros((n_rows, d), F32)
    return pl.pallas_call(
        _scatter_kernel,
        grid_spec=pltpu.PrefetchScalarGridSpec(
            num_scalar_prefetch=1,
            grid=(t // tb,),
            in_specs=[pl.BlockSpec((tb, d), lambda i, dest: (i, 0)),
                      pl.BlockSpec(memory_space=pl.ANY)],
            out_specs=pl.BlockSpec(memory_space=pl.ANY),
            scratch_shapes=[pltpu.SemaphoreType.DMA(())],
        ),
        out_shape=jax.ShapeDtypeStruct((n_rows, d), F32),
        input_output_aliases={2: 0},
        compiler_params=_cp(("arbitrary",)),
        name="scatter_rows",
    )(dest_flat, h2, xs0)


def _expert_kernel(be_ref, bs_ref, nu_ref, x_ref, bgu_ref, bd_ref, wgu_hbm, wd_hbm, o_ref,
                   xb_ref, act_ref, wbuf, sem):
    del bs_ref
    i = pl.program_id(0)
    n = pl.program_id(1)
    n_steps = pl.num_programs(1)
    nt, _, tn = act_ref.shape
    n_slot, de, tnb = wbuf.shape
    look = n_slot - 1
    g = i * n_steps + n
    n_live = nu_ref[0] * n_steps
    live = i < nu_ref[0]

    def up_copies(blk, step, slot):
        e = be_ref[blk]
        col = pl.multiple_of(step * tn, tn)
        glu = pltpu.make_async_copy(wgu_hbm.at[e, :, pl.ds(col, tn)], wbuf.at[slot, :, 0:tn], sem.at[slot])
        lin = pltpu.make_async_copy(wgu_hbm.at[e, :, pl.ds(pl.multiple_of(de + col, tn), tn)],
                                    wbuf.at[slot, :, tn:2 * tn], sem.at[slot])
        return glu, lin

    def down_copy(blk, step, slot):
        colb = pl.multiple_of((step - nt) * tnb, tnb)
        return pltpu.make_async_copy(wd_hbm.at[be_ref[blk], :, pl.ds(colb, tnb)], wbuf.at[slot], sem.at[slot])

    def start(gs):
        blk = gs // n_steps
        step = gs - blk * n_steps
        slot = gs % n_slot

        @pl.when((gs < n_live) & (step < nt))
        def _():
            glu, lin = up_copies(blk, step, slot)
            glu.start()
            lin.start()

        @pl.when((gs < n_live) & (step >= nt))
        def _():
            down_copy(blk, step, slot).start()

    @pl.when(g == 0)
    def _():
        for gs in range(look):
            start(jnp.int32(gs))

    start(g + look)
    slot = g % n_slot

    @pl.when(live & (n == 0))
    def _():
        xb_ref[...] = x_ref[...].astype(BF16)

    @pl.when(live & (n < nt))
    def _():
        glu_cp, lin_cp = up_copies(i, n, slot)
        glu_cp.wait()
        lin_cp.wait()
        xb = xb_ref[...]
        w = wbuf[slot]
        glu = jnp.dot(xb, w[:, 0:tn].astype(BF16), preferred_element_type=F32) + bgu_ref[0, n]
        lin = jnp.dot(xb, w[:, tn:2 * tn].astype(BF16), preferred_element_type=F32) + bgu_ref[0, n + nt]
        glu = jnp.minimum(glu, SWIGLU_LIMIT)
        lin = jnp.clip(lin, -SWIGLU_LIMIT, SWIGLU_LIMIT)
        act_ref[n] = (glu * _sigmoid(SWIGLU_ALPHA * glu) * (lin + 1.0)).astype(BF16)

    for m in range(o_ref.shape[1] // tnb):
        @pl.when(live & (n == nt + m))
        def _():
            down_copy(i, n, slot).wait()
            wd = wbuf[slot]
            y = jnp.dot(act_ref[0], wd[0:tn].astype(BF16), preferred_element_type=F32)
            for k in range(1, nt):
                y = y + jnp.dot(act_ref[k], wd[k * tn:(k + 1) * tn].astype(BF16), preferred_element_type=F32)
            o_ref[:, m * tnb:(m + 1) * tnb] = y + bd_ref[0, m]

    @pl.when(jnp.logical_not(live) & (n == 0))
    def _():
        o_ref[...] = jnp.zeros_like(o_ref)


def _experts(blk_expert, blk_src, n_used, xs, w_gu, b_gu, w_down, b_down):
    n_rows, d = xs.shape
    tm, tn, tnb = MOE_TM, MOE_TN, MOE_TNB
    nb = n_rows // tm
    e, _, two_de = w_gu.shape
    de = two_de // 2
    nt = de // tn
    ntb = d // tnb
    assert 2 * tn == tnb and de == d
    b_gu4 = b_gu.reshape(e, 2 * nt, 1, tn)
    b_d4 = b_down.reshape(e, ntb, 1, tnb)
    return pl.pallas_call(
        _expert_kernel,
        grid_spec=pltpu.PrefetchScalarGridSpec(
            num_scalar_prefetch=3,
            grid=(nb, nt + ntb),
            in_specs=[pl.BlockSpec((tm, d), lambda i, n, be, bs, nu: (bs[i], 0)),
                      pl.BlockSpec((1, 2 * nt, 1, tn), lambda i, n, be, bs, nu: (be[i], 0, 0, 0)),
                      pl.BlockSpec((1, ntb, 1, tnb), lambda i, n, be, bs, nu: (be[i], 0, 0, 0)),
                      pl.BlockSpec(memory_space=pl.ANY),
                      pl.BlockSpec(memory_space=pl.ANY)],
            out_specs=pl.BlockSpec((tm, d), lambda i, n, be, bs, nu: (i, 0)),
            scratch_shapes=[pltpu.VMEM((tm, d), BF16), pltpu.VMEM((nt, tm, tn), BF16),
                            pltpu.VMEM((MOE_RING, de, tnb), F32), pltpu.SemaphoreType.DMA((MOE_RING,))],
        ),
        out_shape=jax.ShapeDtypeStruct((n_rows, d), F32),
        compiler_params=_cp(("arbitrary", "arbitrary")),
        name="experts",
    )(blk_expert, blk_src, n_used, xs, b_gu4, b_d4, w_gu, w_down)


def _combine_kernel(dest_ref, ys_ref, gate_ref, x_ref, g2_ref, fg_ref, o_ref, buf_ref, sem):
    tb = x_ref.shape[0]
    base = pl.program_id(0) * tb

    def copy(t, j):
        d = dest_ref[(base + t) * TOP_K + j]
        return pltpu.make_async_copy(ys_ref.at[pl.ds(d, 1), :], buf_ref.at[j, pl.ds(t, 1), :], sem)

    def start(t, carry):
        for j in range(TOP_K):
            copy(t, j).start()
        return carry

    lax.fori_loop(0, tb, start, 0, unroll=ROW_DMA_UNROLL)
    for j in range(TOP_K):
        pltpu.make_async_copy(ys_ref.at[pl.ds(0, tb), :], buf_ref.at[j], sem).wait()

    gate = gate_ref[...]
    ffn = gate[:, 0:1] * buf_ref[0]
    for j in range(1, TOP_K):
        ffn = ffn + gate[:, j:j + 1] * buf_ref[j]
    x2 = x_ref[...] + g2_ref[0] * ffn
    o_ref[...] = _rms(x2, NORM_EPS) * fg_ref[...]


def _combine(dest_flat, ys, gates, x1, g2, fg, seq):
    t, d = x1.shape
    tb = min(COMBINE_TB, seq)
    per_b = seq // tb
    return pl.pallas_call(
        _combine_kernel,
        grid_spec=pltpu.PrefetchScalarGridSpec(
            num_scalar_prefetch=1,
            grid=(t // tb,),
            in_specs=[pl.BlockSpec(memory_space=pl.ANY),
                      pl.BlockSpec((tb, LANES), lambda i, dest: (i, 0)),
                      pl.BlockSpec((tb, d), lambda i, dest: (i, 0)),
                      pl.BlockSpec((1, 1, d), lambda i, dest: (i // per_b, 0, 0)),
                      pl.BlockSpec((1, d), lambda i, dest: (0, 0))],
            out_specs=pl.BlockSpec((tb, d), lambda i, dest: (i, 0)),
            scratch_shapes=[pltpu.VMEM((TOP_K, tb, d), F32), pltpu.SemaphoreType.DMA(())],
        ),
        out_shape=jax.ShapeDtypeStruct((t, d), F32),
        compiler_params=_cp(("arbitrary",)),
        name="combine",
    )(dest_flat, ys, gates, x1, g2, fg.reshape(1, d))


def _moe(x1, g, sc2, sh2, g2, router_w, router_b, w_gu, b_gu, w_down, b_down, final_g, seq):
    t, d = x1.shape
    a = t * TOP_K
    tm = MOE_TM
    nb = -(-a // tm) + N_EXPERTS
    h2, idx_w, gate_w, rank_w, counts = _route(x1, g, sc2, sh2, router_w, router_b, seq)
    idx = idx_w[:, :TOP_K]
    rank = rank_w[:, :TOP_K]
    cnt = counts.reshape(-1).astype(jnp.int32)
    padded = ((cnt + tm - 1) // tm) * tm
    pad_end = jnp.cumsum(padded)
    pad_start = pad_end - padded
    dest = (pad_start[idx] + rank).reshape(a).astype(jnp.int32)
    n_used = (pad_end[-1] // tm).astype(jnp.int32)
    blk = jnp.arange(nb, dtype=jnp.int32)
    blk_src = jnp.minimum(blk, n_used - 1)
    blk_expert = jnp.sum((pad_end[None, :] <= (blk_src * tm)[:, None]).astype(jnp.int32), axis=1)
    blk_expert = jnp.minimum(blk_expert, N_EXPERTS - 1)
    xs = _scatter_rows(dest, h2, nb * tm)
    ys = _experts(blk_expert, blk_src, n_used.reshape(1), xs, w_gu, b_gu, w_down, b_down)
    return _combine(dest, ys, gate_w, x1, g2, final_g, seq)


def kernel(x, c, ada_w, ada_b, norm1_g, norm2_g, w_in, rwkv_mu, rwkv_w0, rwkv_w2, rwkv_a0, rwkv_a2, rwkv_g2,
           rwkv_k_k, rwkv_k_a, rwkv_r_k, rwkv_gn_g, rwkv_gn_b, hgrn_lb_logits, hgrn_gn_g, proj_a, proj_b, w_out,
           router_w, router_b, exp_w_gate_up, exp_b_gate_up, exp_w_down, exp_b_down, final_norm_g):
    bsz, seq, d = x.shape
    depth = ada_w.shape[0]
    lower_bounds = jnp.cumsum(jax.nn.softmax(hgrn_lb_logits.astype(F32), axis=0), axis=0)
    xf = x.reshape(bsz * seq, d)
    rkv = 3 * RWKV_WIDTH
    rwkv_cols = rkv + LORA_COLS
    hg_gate = 4 * HGRN_WIDTH + 2 * d
    for l in range(depth):
        mod = _adaln(c, ada_w[l], ada_b[l])
        sh1, sc1, g1, sh2, sc2, g2 = [m.reshape(bsz, 1, d) for m in jnp.split(mod, 6, axis=-1)]

        wl = w_in[l]
        w_cat = jnp.concatenate([wl[:, :rkv], wl[:, rwkv_cols:rwkv_cols + hg_gate], wl[:, rkv:rwkv_cols],
                                 jnp.zeros((d, LORA_PAD - LORA_COLS), wl.dtype)], axis=1).astype(BF16)
        proj = _inproj(xf, norm1_g[l], sc1, sh1, w_cat, seq)

        mu = rwkv_mu[l]
        mu_lo = jnp.pad(mu[rkv:], (0, LORA_PAD - LORA_COLS))
        pad_rows = lambda w, r0: jnp.zeros((LORA_PAD, RWKV_WIDTH), F32).at[r0:r0 + w.shape[0]].set(w).astype(BF16)
        ya = _rwkv(proj, bsz, seq, mu[:rkv], mu_lo, rwkv_w0[l], rwkv_a0[l], rwkv_k_k[l], rwkv_k_a[l],
                   rwkv_r_k[l].reshape(-1), rwkv_gn_g[l], rwkv_gn_b[l],
                   pad_rows(rwkv_w2[l], 0), pad_rows(rwkv_a2[l], DECAY_LORA),
                   pad_rows(rwkv_g2[l], DECAY_LORA + AAA_LORA))
        yb = _hgrn(proj, bsz, seq, lower_bounds[l], hgrn_gn_g[l])
        merged = _merge(ya, yb, proj_a[l].astype(BF16), proj_b[l].astype(BF16), proj)
        x1 = _outproj(merged, w_out[l].astype(BF16), xf, g1, seq)
        assert depth == 1
        out = _moe(x1, norm2_g[l], sc2, sh2, g2, router_w[l], router_b[l], exp_w_gate_up[l], exp_b_gate_up[l],
                   exp_w_down[l], exp_b_down[l], final_norm_g, seq)
    return out.reshape(bsz, seq, d)
```

```python
import functools

import jax
import jax.numpy as jnp
from jax import lax
from jax.experimental import pallas as pl
from jax.experimental.pallas import tpu as pltpu

F32 = jnp.float32
BF16 = jnp.bfloat16

D_MODEL = 2048
RWKV_WIDTH = 1024
RWKV_HEAD = 64
DECAY_LORA = 64
AAA_LORA = 64
GATE_LORA = 160
LORA_COLS = DECAY_LORA + AAA_LORA + GATE_LORA
LORA_PAD = 512
RWKV_GN_EPS = 64e-5
HGRN_WIDTH = 1024
HGRN_HEAD = 128
GLA_BLOCK = 16
N_EXPERTS = 32
TOP_K = 4
D_EXPERT = 2048
SWIGLU_ALPHA = 1.702
SWIGLU_LIMIT = 7.0
NORM_EPS = 1e-5

LANES = 128
VMEM_LIMIT = 56 * 1024 * 1024

RWKV_CHUNK = 64
HGRN_STEP = 128
PROJ_TM = 1024
PROJ_TN = 512
MOE_TM = 512
MOE_TN = 512
MOE_TNB = 1024
MOE_RING = 3
ROUTE_TB = 256
SCATTER_TB = 256
COMBINE_TB = 128
ROW_DMA_UNROLL = 8


def _cp(sem, vmem=VMEM_LIMIT):
    return pltpu.CompilerParams(dimension_semantics=sem, vmem_limit_bytes=vmem)


def _mm(a, b):
    return jnp.dot(a.astype(BF16), b.astype(BF16), preferred_element_type=F32)


def _mm_nt(a, b):
    return lax.dot_general(a.astype(BF16), b.astype(BF16), (((1,), (1,)), ((), ())),
                           preferred_element_type=F32)


def _mm_tn(a, b):
    return lax.dot_general(a.astype(BF16), b.astype(BF16), (((0,), (0,)), ((), ())),
                           preferred_element_type=F32)


def _split3(x):
    hi = x.astype(BF16)
    r1 = x - hi.astype(F32)
    mid = r1.astype(BF16)
    lo = (r1 - mid.astype(F32)).astype(BF16)
    return hi, mid, lo


def _mm_exact_rhs(m_bf16, x):
    hi, mid, lo = _split3(x)
    d = lambda p: jnp.dot(m_bf16, p, preferred_element_type=F32)
    return d(hi) + d(mid) + d(lo)


def _sigmoid(x):
    return 1.0 / (1.0 + jnp.exp(-x))


def _softplus(x):
    return jnp.maximum(x, 0.0) + jnp.log(1.0 + jnp.exp(-jnp.abs(x)))


def _rms(x, eps):
    return x * lax.rsqrt(jnp.mean(x * x, axis=-1, keepdims=True) + eps)


def _adaln_kernel(c_ref, w_ref, b_ref, o_ref):
    c = c_ref[...]
    c_act = c * _sigmoid(c)
    o_ref[...] = jnp.dot(c_act, w_ref[...], preferred_element_type=F32,
                         precision=lax.Precision.HIGHEST) + b_ref[...]


def _adaln(c, w, b):
    bsz, d = c.shape
    n = w.shape[1]
    tn = 1024
    return pl.pallas_call(
        _adaln_kernel,
        grid=(n // tn,),
        in_specs=[pl.BlockSpec((bsz, d), lambda j: (0, 0)),
                  pl.BlockSpec((d, tn), lambda j: (0, j)),
                  pl.BlockSpec((1, tn), lambda j: (0, j))],
        out_specs=pl.BlockSpec((bsz, tn), lambda j: (0, j)),
        out_shape=jax.ShapeDtypeStruct((bsz, n), F32),
        compiler_params=_cp(("arbitrary",)),
        name="adaln",
    )(c, w, b.reshape(1, n))


def _inproj_kernel(x_ref, g_ref, sc_ref, sh_ref, w_ref, o_ref, h_ref):
    @pl.when(pl.program_id(1) == 0)
    def _():
        y = _rms(x_ref[...], NORM_EPS) * g_ref[...]
        h_ref[...] = (y * (1.0 + sc_ref[0]) + sh_ref[0]).astype(BF16)

    o_ref[...] = jnp.dot(h_ref[...], w_ref[...], preferred_element_type=F32)


def _inproj(xf, g, sc, sh, w, seq):
    t, d = xf.shape
    n = w.shape[1]
    tm = min(PROJ_TM, seq)
    tn = PROJ_TN
    per_b = seq // tm
    return pl.pallas_call(
        _inproj_kernel,
        grid=(t // tm, n // tn),
        in_specs=[pl.BlockSpec((tm, d), lambda i, j: (i, 0)),
                  pl.BlockSpec((1, d), lambda i, j: (0, 0)),
                  pl.BlockSpec((1, 1, d), lambda i, j: (i // per_b, 0, 0)),
                  pl.BlockSpec((1, 1, d), lambda i, j: (i // per_b, 0, 0)),
                  pl.BlockSpec((d, tn), lambda i, j: (0, j))],
        out_specs=pl.BlockSpec((tm, tn), lambda i, j: (i, j)),
        out_shape=jax.ShapeDtypeStruct((t, n), F32),
        scratch_shapes=[pltpu.VMEM((tm, d), BF16)],
        compiler_params=_cp(("arbitrary", "arbitrary")),
        name="inproj",
    )(xf, g.reshape(1, d), sc, sh, w)


def _rwkv_kernel(r_ref, k_ref, v_ref, lo_ref, mu_ref, mulo_ref, w0_ref, a0_ref, kk_ref, ka_ref, rk_ref,
                 gng_ref, gnb_ref, w2_ref, a2_ref, g2_ref, o_ref,
                 st_ref, cr_ref, ck_ref, cv_ref, clo_ref):
    L = RWKV_CHUNK
    H2 = 2 * L
    n_pair = RWKV_WIDTH // LANES

    @pl.when(pl.program_id(1) == 0)
    def _():
        st_ref[...] = jnp.zeros_like(st_ref)
        cr_ref[...] = jnp.zeros_like(cr_ref)
        ck_ref[...] = jnp.zeros_like(ck_ref)
        cv_ref[...] = jnp.zeros_like(cv_ref)
        clo_ref[...] = jnp.zeros_like(clo_ref)

    def shift(x_ref, mu, carry_ref):
        x = x_ref[...]
        rows = lax.broadcasted_iota(jnp.int32, x.shape, 0)
        prev = jnp.where(rows == 0, carry_ref[...], pltpu.roll(x, 1, axis=0))
        carry_ref[...] = x[L - 1:L, :]
        return x + (prev - x) * mu

    mu = mu_ref[...]
    r = shift(r_ref, mu[:, 0:RWKV_WIDTH], cr_ref)
    k = shift(k_ref, mu[:, RWKV_WIDTH:2 * RWKV_WIDTH], ck_ref)
    v = shift(v_ref, mu[:, 2 * RWKV_WIDTH:3 * RWKV_WIDTH], cv_ref)
    lo = shift(lo_ref, mulo_ref[...], clo_ref)

    wpre = w0_ref[...] + _mm(jnp.tanh(lo), w2_ref[...])
    lw = -jnp.exp(-_softplus(-wpre) - 0.5)
    a = _sigmoid(a0_ref[...] + _mm(lo, a2_ref[...]))
    g = _mm(_sigmoid(lo), g2_ref[...])

    ti = lax.broadcasted_iota(jnp.int32, (L, L), 0)
    tj = lax.broadcasted_iota(jnp.int32, (L, L), 1)
    tri = jnp.where(tj <= ti, 1.0, 0.0).astype(BF16)
    cl = _mm_exact_rhs(tri, lw)
    cl_last = cl[L - 1:L, :]
    e_pos = jnp.exp(cl)
    e_neg = jnp.exp(-cl)
    e_prev = jnp.exp(cl - lw)
    e_end = jnp.exp(cl_last - cl)
    p_last = jnp.exp(cl_last)

    kk = k * kk_ref[...]
    k2 = k * (1.0 + (a - 1.0) * ka_ref[...])
    f_b = a * e_neg
    f_bend = a * e_end
    rk = rk_ref[...]
    gng = gng_ref[...]
    gnb = gnb_ref[...]

    lane = lax.broadcasted_iota(jnp.int32, (L, LANES), 1)
    lo_half = lane < RWKV_HEAD
    si = lax.broadcasted_iota(jnp.int32, (H2, H2), 0)
    sj = lax.broadcasted_iota(jnp.int32, (H2, H2), 1)
    same = (si // L) == (sj // L)
    m_strict = same & (sj < si)
    m_incl = same & (sj <= si)
    eye = si == sj
    own = (si // L) == (sj // RWKV_HEAD)

    def stack(x):
        return jnp.concatenate([jnp.where(lo_half, x, 0.0), jnp.where(lo_half, 0.0, x)], axis=0)

    def rep(x):
        return jnp.concatenate([x, x], axis=0)

    pairs = range(n_pair)
    sls = [slice(p * LANES, (p + 1) * LANES) for p in pairs]
    at, bt, bend, kt, kend, rt, v_s, bonus = [], [], [], [], [], [], [], []
    for sl in sls:
        kk_s = stack(kk[:, sl])
        nrm = jnp.sqrt(jnp.sum(kk_s * kk_s, axis=-1, keepdims=True))
        kkn = kk_s / jnp.maximum(nrm, 1e-12)
        at.append(-kkn * rep(e_prev[:, sl]))
        bt.append(kkn * rep(f_b[:, sl]))
        bend.append(kkn * rep(f_bend[:, sl]))
        k2_s = stack(k2[:, sl])
        kt.append(k2_s * rep(e_neg[:, sl]))
        kend.append(k2_s * rep(e_end[:, sl]))
        r_s = stack(r[:, sl])
        rt.append(r_s * rep(e_pos[:, sl]))
        vs = stack(v[:, sl])
        v_s.append(vs)
        bonus.append(jnp.sum(r_s * k2_s * rk[:, sl], axis=-1, keepdims=True) * vs)

    gm = [_mm_nt(jnp.concatenate([at[p], rt[p]], axis=0), jnp.concatenate([bt[p], kt[p]], axis=0))
          for p in pairs]
    a_ab = [jnp.where(m_strict, gm[p][:H2, :H2], 0.0) for p in pairs]
    a_ak = [jnp.where(m_strict, gm[p][:H2, H2:], 0.0) for p in pairs]
    a_rb = [jnp.where(m_incl, gm[p][H2:, :H2], 0.0) for p in pairs]
    a_rk = [jnp.where(m_incl, gm[p][H2:, H2:], 0.0) for p in pairs]

    ident = jnp.where(eye, 1.0, 0.0)
    tinv = [ident + a_ab[p] for p in pairs]
    pw = a_ab
    akv = [_mm(a_ak[p], v_s[p]) for p in pairs]
    ark_v = [_mm(a_rk[p], v_s[p]) for p in pairs]
    kend_v = [_mm_tn(kend[p], v_s[p]) for p in pairs]
    for _ in range(max(1, (L - 1).bit_length() - 1)):
        pw = [_mm(pw[p], pw[p]) for p in pairs]
        tinv = [tinv[p] + _mm(tinv[p], pw[p]) for p in pairs]

    tx = [_mm(tinv[p], jnp.concatenate([at[p], akv[p]], axis=1)) for p in pairs]
    y2 = [_mm(a_rb[p], tx[p]) for p in pairs]
    btx = [_mm_tn(bend[p], tx[p]) for p in pairs]
    s0 = [st_ref[p] for p in pairs]
    y = [_mm(rt[p] + y2[p][:, :LANES], s0[p]) + (y2[p][:, LANES:] + ark_v[p]) for p in pairs]
    for p in pairs:
        m_mat = jnp.where(eye, jnp.broadcast_to(p_last[:, sls[p]], (H2, LANES)), 0.0) + btx[p][:, :LANES]
        st_ref[p] = _mm(m_mat, s0[p]) + (btx[p][:, LANES:] + kend_v[p])

    for p in pairs:
        sl = sls[p]
        mean = jnp.sum(y[p], axis=-1, keepdims=True) * (1.0 / RWKV_HEAD)
        dlt = jnp.where(own, y[p] - mean, 0.0)
        var = jnp.sum(dlt * dlt, axis=-1, keepdims=True) * (1.0 / RWKV_HEAD)
        yn = dlt * lax.rsqrt(var + RWKV_GN_EPS) * gng[:, sl] + jnp.where(own, gnb[:, sl], 0.0)
        ys = yn + bonus[p]
        o_ref[:, sl] = ((ys[:L] + ys[L:]) * g[:, sl]).astype(o_ref.dtype)


def _rwkv(proj, bsz, seq, mu_rkv, mu_lo, w0, a0, k_k, k_a, r_k, gn_g, gn_b, w2p, a2p, g2p):
    L = RWKV_CHUNK
    nc = seq // L
    W = RWKV_WIDTH
    row = lambda n: pl.BlockSpec((1, n), lambda b, c: (0, 0))
    full = lambda a: pl.BlockSpec(a.shape, lambda b, c: (0, 0))
    lo_blk = proj.shape[1] // LORA_PAD - 1
    v1 = lambda a: a.reshape(1, -1)
    return pl.pallas_call(
        _rwkv_kernel,
        grid=(bsz, nc),
        in_specs=[pl.BlockSpec((L, W), lambda b, c: (b * nc + c, 0)),
                  pl.BlockSpec((L, W), lambda b, c: (b * nc + c, 1)),
                  pl.BlockSpec((L, W), lambda b, c: (b * nc + c, 2)),
                  pl.BlockSpec((L, LORA_PAD), lambda b, c: (b * nc + c, lo_blk)),
                  row(3 * W), row(LORA_PAD), row(W), row(W), row(W), row(W), row(W), row(W), row(W),
                  full(w2p), full(a2p), full(g2p)],
        out_specs=pl.BlockSpec((L, W), lambda b, c: (b * nc + c, 0)),
        out_shape=jax.ShapeDtypeStruct((bsz * seq, W), BF16),
        scratch_shapes=[pltpu.VMEM((W // LANES, LANES, LANES), F32),
                        pltpu.VMEM((1, W), F32), pltpu.VMEM((1, W), F32), pltpu.VMEM((1, W), F32),
                        pltpu.VMEM((1, LORA_PAD), F32)],
        compiler_params=_cp(("arbitrary", "arbitrary")),
        name="rwkv7",
    )(proj, proj, proj, proj, v1(mu_rkv), v1(mu_lo), v1(w0), v1(a0), v1(k_k), v1(k_a), v1(r_k), v1(gn_g),
      v1(gn_b), w2p, a2p, g2p)


def _hgrn_kernel(q_ref, f_ref, i_ref, og_ref, lb_ref, gn_ref, o_ref, st_ref):
    n = HGRN_STEP
    blk = GLA_BLOCK
    nb = n // blk
    n_head = HGRN_WIDTH // HGRN_HEAD

    @pl.when(pl.program_id(1) == 0)
    def _():
        st_ref[...] = jnp.zeros_like(st_ref)

    lb = lb_ref[...]
    qf = q_ref[...]
    q = qf * _sigmoid(qf)
    forget = lb + (1.0 - lb) * _sigmoid(f_ref[...])
    log_f = jnp.log(forget)
    kx = 1.0 - forget
    vv = i_ref[...]
    og = og_ref[...]

    ti = lax.broadcasted_iota(jnp.int32, (n, n), 0)
    tj = lax.broadcasted_iota(jnp.int32, (n, n), 1)
    same = (ti // blk) == (tj // blk)
    causal = same & (tj <= ti)
    b = _mm_exact_rhs(jnp.where(causal, 1.0, 0.0).astype(BF16), log_f)
    b_last = _mm_exact_rhs(jnp.where(same, 1.0, 0.0).astype(BF16), log_f)
    q_dec = q * jnp.exp(b)
    k_dec = kx * jnp.exp(-b)
    k_end = kx * jnp.exp(b_last - b)
    d_end = jnp.exp(b_last)
    rows = lax.broadcasted_iota(jnp.int32, (n, HGRN_HEAD), 0)
    gn = gn_ref[...]

    heads = range(n_head)
    sls = [slice(h * HGRN_HEAD, (h + 1) * HGRN_HEAD) for h in heads]
    attn = [jnp.where(causal, _mm_nt(q_dec[:, sl], k_dec[:, sl]), 0.0) for sl in sls]
    o_intra = [_mm(attn[h], vv[:, sls[h]]) for h in heads]
    vt = [vv[:, sl].T for sl in sls]
    blk_mask = [(rows // blk) == c for c in range(nb)]
    kv_t = [[_mm(vt[h], jnp.where(blk_mask[c], k_end[:, sls[h]], 0.0)) for c in range(nb)]
            for h in heads]
    st = [[st_ref[h]] for h in heads]
    for h in heads:
        for c in range(nb):
            st[h].append(st[h][c] * d_end[c * blk:c * blk + 1, sls[h]] + kv_t[h][c])
        st_ref[h] = st[h][nb]
    for h in heads:
        sl = sls[h]
        inter = [_mm_nt(q_dec[c * blk:(c + 1) * blk, sl], st[h][c]) for c in range(nb)]
        o = o_intra[h] + jnp.concatenate(inter, axis=0)
        o = _rms(o, NORM_EPS) * gn
        ogh = og[:, sl]
        o_ref[:, sl] = (o * (ogh * _sigmoid(ogh))).astype(o_ref.dtype)


def _hgrn(proj, bsz, seq, lb, gn_g):
    n = HGRN_STEP
    ns = seq // n
    W = HGRN_WIDTH
    col = lambda j: pl.BlockSpec((n, W), lambda b, c: (b * ns + c, j))
    return pl.pallas_call(
        _hgrn_kernel,
        grid=(bsz, ns),
        in_specs=[col(3), col(4), col(5), col(6),
                  pl.BlockSpec((1, W), lambda b, c: (0, 0)),
                  pl.BlockSpec((1, HGRN_HEAD), lambda b, c: (0, 0))],
        out_specs=pl.BlockSpec((n, W), lambda b, c: (b * ns + c, 0)),
        out_shape=jax.ShapeDtypeStruct((bsz * seq, W), BF16),
        scratch_shapes=[pltpu.VMEM((W // HGRN_HEAD, HGRN_HEAD, HGRN_HEAD), F32)],
        compiler_params=_cp(("arbitrary", "arbitrary")),
        name="hgrn2",
    )(proj, proj, proj, proj, lb.reshape(1, W), gn_g.reshape(1, HGRN_HEAD))


def _merge_kernel(ya_ref, yb_ref, pa_ref, pb_ref, ga_ref, gb_ref, o_ref):
    ma = jnp.dot(ya_ref[...], pa_ref[...], preferred_element_type=F32)
    mb = jnp.dot(yb_ref[...], pb_ref[...], preferred_element_type=F32)
    o_ref[...] = (_sigmoid(ga_ref[...]) * ma + _sigmoid(gb_ref[...]) * mb).astype(o_ref.dtype)


def _merge(ya, yb, pa, pb, proj):
    t, w = ya.shape
    d = pa.shape[1]
    tm, tn = min(PROJ_TM, t), PROJ_TN
    ga0 = (3 * RWKV_WIDTH + 4 * HGRN_WIDTH) // tn
    gb0 = ga0 + d // tn
    return pl.pallas_call(
        _merge_kernel,
        grid=(t // tm, d // tn),
        in_specs=[pl.BlockSpec((tm, w), lambda i, j: (i, 0)),
                  pl.BlockSpec((tm, w), lambda i, j: (i, 0)),
                  pl.BlockSpec((w, tn), lambda i, j: (0, j)),
                  pl.BlockSpec((w, tn), lambda i, j: (0, j)),
                  pl.BlockSpec((tm, tn), lambda i, j: (i, ga0 + j)),
                  pl.BlockSpec((tm, tn), lambda i, j: (i, gb0 + j))],
        out_specs=pl.BlockSpec((tm, tn), lambda i, j: (i, j)),
        out_shape=jax.ShapeDtypeStruct((t, d), BF16),
        compiler_params=_cp(("arbitrary", "arbitrary")),
        name="merge",
    )(ya, yb, pa, pb, proj, proj)


def _outproj_kernel(m_ref, w_ref, x_ref, g_ref, o_ref):
    mix = jnp.dot(m_ref[...], w_ref[...], preferred_element_type=F32)
    o_ref[...] = x_ref[...] + g_ref[0] * mix


def _outproj(merged, w, xf, g1, seq):
    t, d = merged.shape
    tm, tn = min(PROJ_TM, seq), PROJ_TN
    per_b = seq // tm
    return pl.pallas_call(
        _outproj_kernel,
        grid=(t // tm, d // tn),
        in_specs=[pl.BlockSpec((tm, d), lambda i, j: (i, 0)),
                  pl.BlockSpec((d, tn), lambda i, j: (0, j)),
                  pl.BlockSpec((tm, tn), lambda i, j: (i, j)),
                  pl.BlockSpec((1, 1, tn), lambda i, j: (i // per_b, 0, j))],
        out_specs=pl.BlockSpec((tm, tn), lambda i, j: (i, j)),
        out_shape=jax.ShapeDtypeStruct((t, d), F32),
        compiler_params=_cp(("arbitrary", "arbitrary")),
        name="outproj",
    )(merged, w, xf, g1)


def _route_kernel(x_ref, g_ref, sc_ref, sh_ref, rw_ref, rb_ref, h_ref, idx_ref, gate_ref, rank_ref, cnt_ref,
                  carry_ref):
    tb = x_ref.shape[0]

    @pl.when(pl.program_id(0) == 0)
    def _():
        carry_ref[...] = jnp.zeros_like(carry_ref)

    h = (_rms(x_ref[...], NORM_EPS) * g_ref[...]) * (1.0 + sc_ref[0]) + sh_ref[0]
    h_ref[...] = h
    logits = jnp.dot(h, rw_ref[...], preferred_element_type=F32, precision=lax.Precision.HIGHEST) + rb_ref[...]

    lane = lax.broadcasted_iota(jnp.int32, logits.shape, 1).astype(F32)
    work = logits
    vals, idxs, sels = [], [], []
    for _ in range(TOP_K):
        m = jnp.max(work, axis=-1, keepdims=True)
        idx = jnp.min(jnp.where(work == m, lane, float(N_EXPERTS)), axis=-1, keepdims=True)
        sel = lane == idx
        vals.append(m)
        idxs.append(idx)
        sels.append(sel)
        work = jnp.where(sel, -jnp.inf, work)
    exps = [jnp.exp(v - vals[0]) for v in vals]
    den = exps[0] + exps[1] + exps[2] + exps[3]

    onehot = jnp.zeros(logits.shape, F32)
    for sel in sels:
        onehot = onehot + jnp.where(sel, 1.0, 0.0)
    ti = lax.broadcasted_iota(jnp.int32, (tb, tb), 0)
    tj = lax.broadcasted_iota(jnp.int32, (tb, tb), 1)
    below = jnp.where(tj < ti, 1.0, 0.0).astype(BF16)
    before = jnp.dot(below, onehot.astype(BF16), preferred_element_type=F32) + carry_ref[...]
    carry_ref[...] = carry_ref[...] + jnp.sum(onehot, axis=0, keepdims=True)
    cnt_ref[...] = carry_ref[...]

    out_lane = lax.broadcasted_iota(jnp.int32, (tb, LANES), 1)
    idx_out = jnp.zeros((tb, LANES), jnp.int32)
    rank_out = jnp.zeros((tb, LANES), jnp.int32)
    gate_out = jnp.zeros((tb, LANES), F32)
    for j in range(TOP_K):
        rank = jnp.sum(jnp.where(sels[j], before, 0.0), axis=-1, keepdims=True)
        idx_out = jnp.where(out_lane == j, idxs[j].astype(jnp.int32), idx_out)
        rank_out = jnp.where(out_lane == j, rank.astype(jnp.int32), rank_out)
        gate_out = jnp.where(out_lane == j, exps[j] / den, gate_out)
    idx_ref[...] = idx_out
    rank_ref[...] = rank_out
    gate_ref[...] = gate_out


def _route(x1, g, sc, sh, rw, rb, seq):
    t, d = x1.shape
    tb = min(ROUTE_TB, seq)
    per_b = seq // tb
    e = rw.shape[1]
    wide = lambda dt: jax.ShapeDtypeStruct((t, LANES), dt)
    return pl.pallas_call(
        _route_kernel,
        grid=(t // tb,),
        in_specs=[pl.BlockSpec((tb, d), lambda i: (i, 0)),
                  pl.BlockSpec((1, d), lambda i: (0, 0)),
                  pl.BlockSpec((1, 1, d), lambda i: (i // per_b, 0, 0)),
                  pl.BlockSpec((1, 1, d), lambda i: (i // per_b, 0, 0)),
                  pl.BlockSpec((d, e), lambda i: (0, 0)),
                  pl.BlockSpec((1, e), lambda i: (0, 0))],
        out_specs=[pl.BlockSpec((tb, d), lambda i: (i, 0)),
                   pl.BlockSpec((tb, LANES), lambda i: (i, 0)),
                   pl.BlockSpec((tb, LANES), lambda i: (i, 0)),
                   pl.BlockSpec((tb, LANES), lambda i: (i, 0)),
                   pl.BlockSpec((1, e), lambda i: (0, 0))],
        out_shape=[jax.ShapeDtypeStruct((t, d), F32), wide(jnp.int32), wide(F32), wide(jnp.int32),
                   jax.ShapeDtypeStruct((1, e), F32)],
        scratch_shapes=[pltpu.VMEM((1, e), F32)],
        compiler_params=_cp(("arbitrary",)),
        name="route",
    )(x1, g.reshape(1, d), sc, sh, rw, rb.reshape(1, e))


def _scatter_kernel(dest_ref, zflag_ref, h_ref, xs_ref, zbuf, sem, zsem):
    tb = h_ref.shape[0]
    zrows = zbuf.shape[0]
    base = pl.program_id(0) * tb

    @pl.when(pl.program_id(0) == 0)
    def _():
        zbuf[...] = jnp.zeros_like(zbuf)

        def zcopy(b):
            return pltpu.make_async_copy(zbuf, xs_ref.at[pl.ds(pl.multiple_of(b * zrows, zrows), zrows), :], zsem)

        def zstart(b, carry):
            @pl.when(zflag_ref[b] != 0)
            def _():
                zcopy(b).start()
            return carry

        def zwait(b, carry):
            @pl.when(zflag_ref[b] != 0)
            def _():
                zcopy(b).wait()
            return carry

        n_blk = xs_ref.shape[0] // zrows
        lax.fori_loop(0, n_blk, zstart, 0)
        lax.fori_loop(0, n_blk, zwait, 0)

    def copy(t, j):
        d = dest_ref[(base + t) * TOP_K + j]
        return pltpu.make_async_copy(h_ref.at[pl.ds(t, 1), :], xs_ref.at[pl.ds(d, 1), :], sem)

    def start(t, carry):
        for j in range(TOP_K):
            copy(t, j).start()
        return carry

    lax.fori_loop(0, tb, start, 0, unroll=ROW_DMA_UNROLL)
    for j in range(TOP_K):
        pltpu.make_async_copy(h_ref, xs_ref.at[pl.ds(0, tb), :], sem).wait()


def _scatter_rows(dest_flat, zero_flag, h2, n_rows):
    t, d = h2.shape
    tb = min(SCATTER_TB, t)
    return pl.pallas_call(
        _scatter_kernel,
        grid_spec=pltpu.PrefetchScalarGridSpec(
            num_scalar_prefetch=2,
            grid=(t // tb,),
            in_specs=[pl.BlockSpec((tb, d), lambda i, dest, zf: (i, 0))],
            out_specs=pl.BlockSpec(memory_space=pl.ANY),
            scratch_shapes=[pltpu.VMEM((MOE_TM, d), F32), pltpu.SemaphoreType.DMA(()),
                            pltpu.SemaphoreType.DMA(())],
        ),
        out_shape=jax.ShapeDtypeStruct((n_rows, d), F32),
        compiler_params=_cp(("arbitrary",)),
        name="scatter_rows",
    )(dest_flat, zero_flag, h2)


def _expert_kernel(be_ref, bs_ref, nu_ref, x_ref, bgu_ref, bd_ref, wgu_hbm, wd_hbm, o_ref,
                   xb_ref, act_ref, wbuf, sem):
    del bs_ref
    i = pl.program_id(0)
    n = pl.program_id(1)
    n_steps = pl.num_programs(1)
    nt, _, tn = act_ref.shape
    n_slot, de, tnb = wbuf.shape
    look = n_slot - 1
    g = i * n_steps + n
    n_live = nu_ref[0] * n_steps
    live = i < nu_ref[0]

    def up_copies(blk, step, slot):
        e = be_ref[blk]
        col = pl.multiple_of(step * tn, tn)
        glu = pltpu.make_async_copy(wgu_hbm.at[e, :, pl.ds(col, tn)], wbuf.at[slot, :, 0:tn], sem.at[slot])
        lin = pltpu.make_async_copy(wgu_hbm.at[e, :, pl.ds(pl.multiple_of(de + col, tn), tn)],
                                    wbuf.at[slot, :, tn:2 * tn], sem.at[slot])
        return glu, lin

    def down_copy(blk, step, slot):
        colb = pl.multiple_of((step - nt) * tnb, tnb)
        return pltpu.make_async_copy(wd_hbm.at[be_ref[blk], :, pl.ds(colb, tnb)], wbuf.at[slot], sem.at[slot])

    def start(gs):
        blk = gs // n_steps
        step = gs - blk * n_steps
        slot = gs % n_slot

        @pl.when((gs < n_live) & (step < nt))
        def _():
            glu, lin = up_copies(blk, step, slot)
            glu.start()
            lin.start()

        @pl.when((gs < n_live) & (step >= nt))
        def _():
            down_copy(blk, step, slot).start()

    @pl.when(g == 0)
    def _():
        for gs in range(look):
            start(jnp.int32(gs))

    start(g + look)
    slot = g % n_slot

    @pl.when(live & (n == 0))
    def _():
        xb_ref[...] = x_ref[...].astype(BF16)

    @pl.when(live & (n < nt))
    def _():
        glu_cp, lin_cp = up_copies(i, n, slot)
        glu_cp.wait()
        lin_cp.wait()
        xb = xb_ref[...]
        w = wbuf[slot]
        glu = jnp.dot(xb, w[:, 0:tn].astype(BF16), preferred_element_type=F32) + bgu_ref[0, n]
        lin = jnp.dot(xb, w[:, tn:2 * tn].astype(BF16), preferred_element_type=F32) + bgu_ref[0, n + nt]
        glu = jnp.minimum(glu, SWIGLU_LIMIT)
        lin = jnp.clip(lin, -SWIGLU_LIMIT, SWIGLU_LIMIT)
        act_ref[n] = (glu * _sigmoid(SWIGLU_ALPHA * glu) * (lin + 1.0)).astype(BF16)

    for m in range(o_ref.shape[1] // tnb):
        @pl.when(live & (n == nt + m))
        def _():
            down_copy(i, n, slot).wait()
            wd = wbuf[slot]
            y = jnp.dot(act_ref[0], wd[0:tn].astype(BF16), preferred_element_type=F32)
            for k in range(1, nt):
                y = y + jnp.dot(act_ref[k], wd[k * tn:(k + 1) * tn].astype(BF16), preferred_element_type=F32)
            o_ref[:, m * tnb:(m + 1) * tnb] = y + bd_ref[0, m]

    @pl.when(jnp.logical_not(live) & (n == 0))
    def _():
        o_ref[...] = jnp.zeros_like(o_ref)


def _experts(blk_expert, blk_src, n_used, xs, w_gu, b_gu, w_down, b_down):
    n_rows, d = xs.shape
    tm, tn, tnb = MOE_TM, MOE_TN, MOE_TNB
    nb = n_rows // tm
    e, _, two_de = w_gu.shape
    de = two_de // 2
    nt = de // tn
    ntb = d // tnb
    assert 2 * tn == tnb and de == d
    b_gu4 = b_gu.reshape(e, 2 * nt, 1, tn)
    b_d4 = b_down.reshape(e, ntb, 1, tnb)
    return pl.pallas_call(
        _expert_kernel,
        grid_spec=pltpu.PrefetchScalarGridSpec(
            num_scalar_prefetch=3,
            grid=(nb, nt + ntb),
            in_specs=[pl.BlockSpec((tm, d), lambda i, n, be, bs, nu: (bs[i], 0)),
                      pl.BlockSpec((1, 2 * nt, 1, tn), lambda i, n, be, bs, nu: (be[i], 0, 0, 0)),
                      pl.BlockSpec((1, ntb, 1, tnb), lambda i, n, be, bs, nu: (be[i], 0, 0, 0)),
                      pl.BlockSpec(memory_space=pl.ANY),
                      pl.BlockSpec(memory_space=pl.ANY)],
            out_specs=pl.BlockSpec((tm, d), lambda i, n, be, bs, nu: (i, 0)),
            scratch_shapes=[pltpu.VMEM((tm, d), BF16), pltpu.VMEM((nt, tm, tn), BF16),
                            pltpu.VMEM((MOE_RING, de, tnb), F32), pltpu.SemaphoreType.DMA((MOE_RING,))],
        ),
        out_shape=jax.ShapeDtypeStruct((n_rows, d), F32),
        compiler_params=_cp(("arbitrary", "arbitrary")),
        name="experts",
    )(blk_expert, blk_src, n_used, xs, b_gu4, b_d4, w_gu, w_down)


def _combine_kernel(dest_ref, ys_ref, gate_ref, x_ref, g2_ref, fg_ref, o_ref, buf_ref, sem):
    tb = x_ref.shape[0]
    base = pl.program_id(0) * tb

    def copy(t, j):
        d = dest_ref[(base + t) * TOP_K + j]
        return pltpu.make_async_copy(ys_ref.at[pl.ds(d, 1), :], buf_ref.at[j, pl.ds(t, 1), :], sem)

    def start(t, carry):
        for j in range(TOP_K):
            copy(t, j).start()
        return carry

    lax.fori_loop(0, tb, start, 0, unroll=ROW_DMA_UNROLL)
    for j in range(TOP_K):
        pltpu.make_async_copy(ys_ref.at[pl.ds(0, tb), :], buf_ref.at[j], sem).wait()

    gate = gate_ref[...]
    ffn = gate[:, 0:1] * buf_ref[0]
    for j in range(1, TOP_K):
        ffn = ffn + gate[:, j:j + 1] * buf_ref[j]
    x2 = x_ref[...] + g2_ref[0] * ffn
    o_ref[...] = _rms(x2, NORM_EPS) * fg_ref[...]


def _combine(dest_flat, ys, gates, x1, g2, fg, seq):
    t, d = x1.shape
    tb = min(COMBINE_TB, seq)
    per_b = seq // tb
    return pl.pallas_call(
        _combine_kernel,
        grid_spec=pltpu.PrefetchScalarGridSpec(
            num_scalar_prefetch=1,
            grid=(t // tb,),
            in_specs=[pl.BlockSpec(memory_space=pl.ANY),
                      pl.BlockSpec((tb, LANES), lambda i, dest: (i, 0)),
                      pl.BlockSpec((tb, d), lambda i, dest: (i, 0)),
                      pl.BlockSpec((1, 1, d), lambda i, dest: (i // per_b, 0, 0)),
                      pl.BlockSpec((1, d), lambda i, dest: (0, 0))],
            out_specs=pl.BlockSpec((tb, d), lambda i, dest: (i, 0)),
            scratch_shapes=[pltpu.VMEM((TOP_K, tb, d), F32), pltpu.SemaphoreType.DMA(())],
        ),
        out_shape=jax.ShapeDtypeStruct((t, d), F32),
        compiler_params=_cp(("arbitrary",)),
        name="combine",
    )(dest_flat, ys, gates, x1, g2, fg.reshape(1, d))


def _moe(x1, g, sc2, sh2, g2, router_w, router_b, w_gu, b_gu, w_down, b_down, final_g, seq):
    t, d = x1.shape
    a = t * TOP_K
    tm = MOE_TM
    nb = -(-a // tm) + N_EXPERTS
    h2, idx_w, gate_w, rank_w, counts = _route(x1, g, sc2, sh2, router_w, router_b, seq)
    idx = idx_w[:, :TOP_K]
    rank = rank_w[:, :TOP_K]
    cnt = counts.reshape(-1).astype(jnp.int32)
    padded = ((cnt + tm - 1) // tm) * tm
    pad_end = jnp.cumsum(padded)
    pad_start = pad_end - padded
    dest = (pad_start[idx] + rank).reshape(a).astype(jnp.int32)
    n_used = (pad_end[-1] // tm).astype(jnp.int32)
    blk = jnp.arange(nb, dtype=jnp.int32)
    blk_src = jnp.clip(blk, 0, jnp.maximum(n_used - 1, 0))
    blk_expert = jnp.sum((pad_end[None, :] <= (blk_src * tm)[:, None]).astype(jnp.int32), axis=1)
    blk_expert = jnp.minimum(blk_expert, N_EXPERTS - 1)
    last_blk = jnp.where(padded > 0, pad_end // tm - 1, -1)
    zero_flag = ((blk >= n_used) | jnp.any(last_blk[None, :] == blk[:, None], axis=1)).astype(jnp.int32)
    xs = _scatter_rows(dest, zero_flag, h2, nb * tm)
    ys = _experts(blk_expert, blk_src, n_used.reshape(1), xs, w_gu, b_gu, w_down, b_down)
    return _combine(dest, ys, gate_w, x1, g2, final_g, seq)


def kernel(x, c, ada_w, ada_b, norm1_g, norm2_g, w_in, rwkv_mu, rwkv_w0, rwkv_w2, rwkv_a0, rwkv_a2, rwkv_g2,
           rwkv_k_k, rwkv_k_a, rwkv_r_k, rwkv_gn_g, rwkv_gn_b, hgrn_lb_logits, hgrn_gn_g, proj_a, proj_b, w_out,
           router_w, router_b, exp_w_gate_up, exp_b_gate_up, exp_w_down, exp_b_down, final_norm_g):
    bsz, seq, d = x.shape
    depth = ada_w.shape[0]
    lower_bounds = jnp.cumsum(jax.nn.softmax(hgrn_lb_logits.astype(F32), axis=0), axis=0)
    xf = x.reshape(bsz * seq, d)
    rkv = 3 * RWKV_WIDTH
    rwkv_cols = rkv + LORA_COLS
    hg_gate = 4 * HGRN_WIDTH + 2 * d
    for l in range(depth):
        mod = _adaln(c, ada_w[l], ada_b[l])
        sh1, sc1, g1, sh2, sc2, g2 = [m.reshape(bsz, 1, d) for m in jnp.split(mod, 6, axis=-1)]

        wl = w_in[l]
        w_cat = jnp.concatenate([wl[:, :rkv], wl[:, rwkv_cols:rwkv_cols + hg_gate], wl[:, rkv:rwkv_cols],
                                 jnp.zeros((d, LORA_PAD - LORA_COLS), wl.dtype)], axis=1).astype(BF16)
        proj = _inproj(xf, norm1_g[l], sc1, sh1, w_cat, seq)

        mu = rwkv_mu[l]
        mu_lo = jnp.pad(mu[rkv:], (0, LORA_PAD - LORA_COLS))
        pad_rows = lambda w, r0: jnp.zeros((LORA_PAD, RWKV_WIDTH), F32).at[r0:r0 + w.shape[0]].set(w).astype(BF16)
        ya = _rwkv(proj, bsz, seq, mu[:rkv], mu_lo, rwkv_w0[l], rwkv_a0[l], rwkv_k_k[l], rwkv_k_a[l],
                   rwkv_r_k[l].reshape(-1), rwkv_gn_g[l], rwkv_gn_b[l],
                   pad_rows(rwkv_w2[l], 0), pad_rows(rwkv_a2[l], DECAY_LORA),
                   pad_rows(rwkv_g2[l], DECAY_LORA + AAA_LORA))
        yb = _hgrn(proj, bsz, seq, lower_bounds[l], hgrn_gn_g[l])
        merged = _merge(ya, yb, proj_a[l].astype(BF16), proj_b[l].astype(BF16), proj)
        x1 = _outproj(merged, w_out[l].astype(BF16), xf, g1, seq)
        assert depth == 1
        out = _moe(x1, norm2_g[l], sc2, sh2, g2, router_w[l], router_b[l], exp_w_gate_up[l], exp_b_gate_up[l],
                   exp_w_down[l], exp_b_down[l], final_norm_g, seq)
    return out.reshape(bsz, seq, d)
```

```python
import functools

import jax
import jax.numpy as jnp
from jax import lax
from jax.experimental import pallas as pl
from jax.experimental.pallas import tpu as pltpu

F32 = jnp.float32
BF16 = jnp.bfloat16

D_MODEL = 2048
RWKV_WIDTH = 1024
RWKV_HEAD = 64
DECAY_LORA = 64
AAA_LORA = 64
GATE_LORA = 160
LORA_COLS = DECAY_LORA + AAA_LORA + GATE_LORA
LORA_PAD = 512
RWKV_GN_EPS = 64e-5
HGRN_WIDTH = 1024
HGRN_HEAD = 128
GLA_BLOCK = 16
N_EXPERTS = 32
TOP_K = 4
D_EXPERT = 2048
SWIGLU_ALPHA = 1.702
SWIGLU_LIMIT = 7.0
NORM_EPS = 1e-5

LANES = 128
VMEM_LIMIT = 56 * 1024 * 1024

RWKV_CHUNK = 64
HGRN_STEP = 128
PROJ_TM = 1024
PROJ_TN = 512
MIX_TM = 256
MOE_TM = 512
MOE_TN = 512
MOE_TNB = 1024
MOE_RING = 3
ROUTE_TB = 256
SCATTER_TB = 256
COMBINE_TB = 128
ROW_DMA_UNROLL = 8


def _cp(sem, vmem=VMEM_LIMIT):
    return pltpu.CompilerParams(dimension_semantics=sem, vmem_limit_bytes=vmem)


def _mm(a, b):
    return jnp.dot(a.astype(BF16), b.astype(BF16), preferred_element_type=F32)


def _mm_nt(a, b):
    return lax.dot_general(a.astype(BF16), b.astype(BF16), (((1,), (1,)), ((), ())),
                           preferred_element_type=F32)


def _mm_tn(a, b):
    return lax.dot_general(a.astype(BF16), b.astype(BF16), (((0,), (0,)), ((), ())),
                           preferred_element_type=F32)


def _split3(x):
    hi = x.astype(BF16)
    r1 = x - hi.astype(F32)
    mid = r1.astype(BF16)
    lo = (r1 - mid.astype(F32)).astype(BF16)
    return hi, mid, lo


def _mm_exact_rhs(m_bf16, x):
    hi, mid, lo = _split3(x)
    d = lambda p: jnp.dot(m_bf16, p, preferred_element_type=F32)
    return d(hi) + d(mid) + d(lo)


def _sigmoid(x):
    return 1.0 / (1.0 + jnp.exp(-x))


def _softplus(x):
    return jnp.maximum(x, 0.0) + jnp.log(1.0 + jnp.exp(-jnp.abs(x)))


def _rms(x, eps):
    return x * lax.rsqrt(jnp.mean(x * x, axis=-1, keepdims=True) + eps)


def _adaln_kernel(c_ref, w_ref, b_ref, o_ref):
    c = c_ref[...]
    c_act = c * _sigmoid(c)
    o_ref[...] = jnp.dot(c_act, w_ref[...], preferred_element_type=F32,
                         precision=lax.Precision.HIGHEST) + b_ref[...]


def _adaln(c, w, b):
    bsz, d = c.shape
    n = w.shape[1]
    tn = 1024
    return pl.pallas_call(
        _adaln_kernel,
        grid=(n // tn,),
        in_specs=[pl.BlockSpec((bsz, d), lambda j: (0, 0)),
                  pl.BlockSpec((d, tn), lambda j: (0, j)),
                  pl.BlockSpec((1, tn), lambda j: (0, j))],
        out_specs=pl.BlockSpec((bsz, tn), lambda j: (0, j)),
        out_shape=jax.ShapeDtypeStruct((bsz, n), F32),
        compiler_params=_cp(("arbitrary",)),
        name="adaln",
    )(c, w, b.reshape(1, n))


def _inproj_kernel(x_ref, g_ref, sc_ref, sh_ref, w_ref, o_ref, h_ref):
    @pl.when(pl.program_id(1) == 0)
    def _():
        y = _rms(x_ref[...], NORM_EPS) * g_ref[...]
        h_ref[...] = (y * (1.0 + sc_ref[0]) + sh_ref[0]).astype(BF16)

    o_ref[...] = jnp.dot(h_ref[...], w_ref[...], preferred_element_type=F32).astype(o_ref.dtype)


def _inproj(xf, g, sc, sh, w, seq):
    t, d = xf.shape
    n = w.shape[1]
    tm = min(PROJ_TM, seq)
    tn = PROJ_TN
    per_b = seq // tm
    return pl.pallas_call(
        _inproj_kernel,
        grid=(t // tm, n // tn),
        in_specs=[pl.BlockSpec((tm, d), lambda i, j: (i, 0)),
                  pl.BlockSpec((1, d), lambda i, j: (0, 0)),
                  pl.BlockSpec((1, 1, d), lambda i, j: (i // per_b, 0, 0)),
                  pl.BlockSpec((1, 1, d), lambda i, j: (i // per_b, 0, 0)),
                  pl.BlockSpec((d, tn), lambda i, j: (0, j))],
        out_specs=pl.BlockSpec((tm, tn), lambda i, j: (i, j)),
        out_shape=jax.ShapeDtypeStruct((t, n), BF16),
        scratch_shapes=[pltpu.VMEM((tm, d), BF16)],
        compiler_params=_cp(("arbitrary", "arbitrary")),
        name="inproj",
    )(xf, g.reshape(1, d), sc, sh, w)


def _rwkv_kernel(r_ref, k_ref, v_ref, lo_ref, mu_ref, mulo_ref, w0_ref, a0_ref, kk_ref, ka_ref, rk_ref,
                 gng_ref, gnb_ref, w2_ref, a2_ref, g2_ref, o_ref,
                 st_ref, cr_ref, ck_ref, cv_ref, clo_ref):
    L = RWKV_CHUNK
    H2 = 2 * L
    n_pair = RWKV_WIDTH // LANES

    @pl.when(pl.program_id(1) == 0)
    def _():
        st_ref[...] = jnp.zeros_like(st_ref)
        cr_ref[...] = jnp.zeros_like(cr_ref)
        ck_ref[...] = jnp.zeros_like(ck_ref)
        cv_ref[...] = jnp.zeros_like(cv_ref)
        clo_ref[...] = jnp.zeros_like(clo_ref)

    def shift(x_ref, mu, carry_ref):
        x = x_ref[...].astype(F32)
        rows = lax.broadcasted_iota(jnp.int32, x.shape, 0)
        prev = jnp.where(rows == 0, carry_ref[...], pltpu.roll(x, 1, axis=0))
        carry_ref[...] = x[L - 1:L, :]
        return x + (prev - x) * mu

    mu = mu_ref[...]
    r = shift(r_ref, mu[:, 0:RWKV_WIDTH], cr_ref)
    k = shift(k_ref, mu[:, RWKV_WIDTH:2 * RWKV_WIDTH], ck_ref)
    v = shift(v_ref, mu[:, 2 * RWKV_WIDTH:3 * RWKV_WIDTH], cv_ref)
    lo = shift(lo_ref, mulo_ref[...], clo_ref)

    wpre = w0_ref[...] + _mm(jnp.tanh(lo), w2_ref[...])
    lw = -jnp.exp(-_softplus(-wpre) - 0.5)
    a = _sigmoid(a0_ref[...] + _mm(lo, a2_ref[...]))
    g = _mm(_sigmoid(lo), g2_ref[...])

    ti = lax.broadcasted_iota(jnp.int32, (L, L), 0)
    tj = lax.broadcasted_iota(jnp.int32, (L, L), 1)
    tri = jnp.where(tj <= ti, 1.0, 0.0).astype(BF16)
    cl = _mm_exact_rhs(tri, lw)
    cl_last = cl[L - 1:L, :]
    e_pos = jnp.exp(cl)
    e_neg = jnp.exp(-cl)
    e_prev = jnp.exp(cl - lw)
    e_end = jnp.exp(cl_last - cl)
    p_last = jnp.exp(cl_last)

    kk = k * kk_ref[...]
    k2 = k * (1.0 + (a - 1.0) * ka_ref[...])
    f_b = a * e_neg
    f_bend = a * e_end
    rk = rk_ref[...]
    gng = gng_ref[...]
    gnb = gnb_ref[...]

    lane = lax.broadcasted_iota(jnp.int32, (L, LANES), 1)
    lo_half = lane < RWKV_HEAD
    si = lax.broadcasted_iota(jnp.int32, (H2, H2), 0)
    sj = lax.broadcasted_iota(jnp.int32, (H2, H2), 1)
    same = (si // L) == (sj // L)
    m_strict = same & (sj < si)
    m_incl = same & (sj <= si)
    eye = si == sj
    own = (si // L) == (sj // RWKV_HEAD)

    def stack(x):
        return jnp.concatenate([jnp.where(lo_half, x, 0.0), jnp.where(lo_half, 0.0, x)], axis=0)

    def rep(x):
        return jnp.concatenate([x, x], axis=0)

    pairs = range(n_pair)
    sls = [slice(p * LANES, (p + 1) * LANES) for p in pairs]
    at, bt, bend, kt, kend, rt, v_s, bonus = [], [], [], [], [], [], [], []
    for sl in sls:
        kk_s = stack(kk[:, sl])
        nrm = jnp.sqrt(jnp.sum(kk_s * kk_s, axis=-1, keepdims=True))
        kkn = kk_s / jnp.maximum(nrm, 1e-12)
        at.append(-kkn * rep(e_prev[:, sl]))
        bt.append(kkn * rep(f_b[:, sl]))
        bend.append(kkn * rep(f_bend[:, sl]))
        k2_s = stack(k2[:, sl])
        kt.append(k2_s * rep(e_neg[:, sl]))
        kend.append(k2_s * rep(e_end[:, sl]))
        r_s = stack(r[:, sl])
        rt.append(r_s * rep(e_pos[:, sl]))
        vs = stack(v[:, sl])
        v_s.append(vs)
        bonus.append(jnp.sum(r_s * k2_s * rk[:, sl], axis=-1, keepdims=True) * vs)

    gm = [_mm_nt(jnp.concatenate([at[p], rt[p]], axis=0), jnp.concatenate([bt[p], kt[p]], axis=0))
          for p in pairs]
    a_ab = [jnp.where(m_strict, gm[p][:H2, :H2], 0.0) for p in pairs]
    a_ak = [jnp.where(m_strict, gm[p][:H2, H2:], 0.0) for p in pairs]
    a_rb = [jnp.where(m_incl, gm[p][H2:, :H2], 0.0) for p in pairs]
    a_rk = [jnp.where(m_incl, gm[p][H2:, H2:], 0.0) for p in pairs]

    ident = jnp.where(eye, 1.0, 0.0)
    tinv = [ident + a_ab[p] for p in pairs]
    pw = a_ab
    akv = [_mm(a_ak[p], v_s[p]) for p in pairs]
    ark_v = [_mm(a_rk[p], v_s[p]) for p in pairs]
    kend_v = [_mm_tn(kend[p], v_s[p]) for p in pairs]
    for _ in range(max(1, (L - 1).bit_length() - 1)):
        pw = [_mm(pw[p], pw[p]) for p in pairs]
        tinv = [tinv[p] + _mm(tinv[p], pw[p]) for p in pairs]

    tx = [_mm(tinv[p], jnp.concatenate([at[p], akv[p]], axis=1)) for p in pairs]
    y2 = [_mm(a_rb[p], tx[p]) for p in pairs]
    btx = [_mm_tn(bend[p], tx[p]) for p in pairs]
    s0 = [st_ref[p] for p in pairs]
    y = [_mm(rt[p] + y2[p][:, :LANES], s0[p]) + (y2[p][:, LANES:] + ark_v[p]) for p in pairs]
    for p in pairs:
        m_mat = jnp.where(eye, jnp.broadcast_to(p_last[:, sls[p]], (H2, LANES)), 0.0) + btx[p][:, :LANES]
        st_ref[p] = _mm(m_mat, s0[p]) + (btx[p][:, LANES:] + kend_v[p])

    for p in pairs:
        sl = sls[p]
        mean = jnp.sum(y[p], axis=-1, keepdims=True) * (1.0 / RWKV_HEAD)
        dlt = jnp.where(own, y[p] - mean, 0.0)
        var = jnp.sum(dlt * dlt, axis=-1, keepdims=True) * (1.0 / RWKV_HEAD)
        yn = dlt * lax.rsqrt(var + RWKV_GN_EPS) * gng[:, sl] + jnp.where(own, gnb[:, sl], 0.0)
        ys = yn + bonus[p]
        o_ref[:, sl] = ((ys[:L] + ys[L:]) * g[:, sl]).astype(o_ref.dtype)


def _rwkv(proj, bsz, seq, mu_rkv, mu_lo, w0, a0, k_k, k_a, r_k, gn_g, gn_b, w2p, a2p, g2p):
    L = RWKV_CHUNK
    nc = seq // L
    W = RWKV_WIDTH
    row = lambda n: pl.BlockSpec((1, n), lambda b, c: (0, 0))
    full = lambda a: pl.BlockSpec(a.shape, lambda b, c: (0, 0))
    lo_blk = proj.shape[1] // LORA_PAD - 1
    v1 = lambda a: a.reshape(1, -1)
    return pl.pallas_call(
        _rwkv_kernel,
        grid=(bsz, nc),
        in_specs=[pl.BlockSpec((L, W), lambda b, c: (b * nc + c, 0)),
                  pl.BlockSpec((L, W), lambda b, c: (b * nc + c, 1)),
                  pl.BlockSpec((L, W), lambda b, c: (b * nc + c, 2)),
                  pl.BlockSpec((L, LORA_PAD), lambda b, c: (b * nc + c, lo_blk)),
                  row(3 * W), row(LORA_PAD), row(W), row(W), row(W), row(W), row(W), row(W), row(W),
                  full(w2p), full(a2p), full(g2p)],
        out_specs=pl.BlockSpec((L, W), lambda b, c: (b * nc + c, 0)),
        out_shape=jax.ShapeDtypeStruct((bsz * seq, W), BF16),
        scratch_shapes=[pltpu.VMEM((W // LANES, LANES, LANES), F32),
                        pltpu.VMEM((1, W), F32), pltpu.VMEM((1, W), F32), pltpu.VMEM((1, W), F32),
                        pltpu.VMEM((1, LORA_PAD), F32)],
        compiler_params=_cp(("arbitrary", "arbitrary")),
        name="rwkv7",
    )(proj, proj, proj, proj, v1(mu_rkv), v1(mu_lo), v1(w0), v1(a0), v1(k_k), v1(k_a), v1(r_k), v1(gn_g),
      v1(gn_b), w2p, a2p, g2p)


def _hgrn_kernel(q_ref, f_ref, i_ref, og_ref, lb_ref, gn_ref, o_ref, st_ref):
    n = HGRN_STEP
    blk = GLA_BLOCK
    nb = n // blk
    n_head = HGRN_WIDTH // HGRN_HEAD

    @pl.when(pl.program_id(1) == 0)
    def _():
        st_ref[...] = jnp.zeros_like(st_ref)

    lb = lb_ref[...]
    qf = q_ref[...].astype(F32)
    q = qf * _sigmoid(qf)
    forget = lb + (1.0 - lb) * _sigmoid(f_ref[...].astype(F32))
    log_f = jnp.log(forget)
    kx = 1.0 - forget
    vv = i_ref[...].astype(F32)
    og = og_ref[...].astype(F32)

    ti = lax.broadcasted_iota(jnp.int32, (n, n), 0)
    tj = lax.broadcasted_iota(jnp.int32, (n, n), 1)
    same = (ti // blk) == (tj // blk)
    causal = same & (tj <= ti)
    b = _mm_exact_rhs(jnp.where(causal, 1.0, 0.0).astype(BF16), log_f)
    b_last = _mm_exact_rhs(jnp.where(same, 1.0, 0.0).astype(BF16), log_f)
    q_dec = q * jnp.exp(b)
    k_dec = kx * jnp.exp(-b)
    k_end = kx * jnp.exp(b_last - b)
    d_end = jnp.exp(b_last)
    rows = lax.broadcasted_iota(jnp.int32, (n, HGRN_HEAD), 0)
    gn = gn_ref[...]

    heads = range(n_head)
    sls = [slice(h * HGRN_HEAD, (h + 1) * HGRN_HEAD) for h in heads]
    attn = [jnp.where(causal, _mm_nt(q_dec[:, sl], k_dec[:, sl]), 0.0) for sl in sls]
    o_intra = [_mm(attn[h], vv[:, sls[h]]) for h in heads]
    vt = [vv[:, sl].T for sl in sls]
    blk_mask = [(rows // blk) == c for c in range(nb)]
    kv_t = [[_mm(vt[h], jnp.where(blk_mask[c], k_end[:, sls[h]], 0.0)) for c in range(nb)]
            for h in heads]
    st = [[st_ref[h]] for h in heads]
    for h in heads:
        for c in range(nb):
            st[h].append(st[h][c] * d_end[c * blk:c * blk + 1, sls[h]] + kv_t[h][c])
        st_ref[h] = st[h][nb]
    for h in heads:
        sl = sls[h]
        inter = [_mm_nt(q_dec[c * blk:(c + 1) * blk, sl], st[h][c]) for c in range(nb)]
        o = o_intra[h] + jnp.concatenate(inter, axis=0)
        o = _rms(o, NORM_EPS) * gn
        ogh = og[:, sl]
        o_ref[:, sl] = (o * (ogh * _sigmoid(ogh))).astype(o_ref.dtype)


def _hgrn(proj, bsz, seq, lb, gn_g):
    n = HGRN_STEP
    ns = seq // n
    W = HGRN_WIDTH
    col = lambda j: pl.BlockSpec((n, W), lambda b, c: (b * ns + c, j))
    return pl.pallas_call(
        _hgrn_kernel,
        grid=(bsz, ns),
        in_specs=[col(3), col(4), col(5), col(6),
                  pl.BlockSpec((1, W), lambda b, c: (0, 0)),
                  pl.BlockSpec((1, HGRN_HEAD), lambda b, c: (0, 0))],
        out_specs=pl.BlockSpec((n, W), lambda b, c: (b * ns + c, 0)),
        out_shape=jax.ShapeDtypeStruct((bsz * seq, W), BF16),
        scratch_shapes=[pltpu.VMEM((W // HGRN_HEAD, HGRN_HEAD, HGRN_HEAD), F32)],
        compiler_params=_cp(("arbitrary", "arbitrary")),
        name="hgrn2",
    )(proj, proj, proj, proj, lb.reshape(1, W), gn_g.reshape(1, HGRN_HEAD))


def _mixout_kernel(ya_ref, yb_ref, ga0_ref, ga1_ref, gb0_ref, gb1_ref, pa_ref, pb_ref, wo_ref, x_ref, g_ref, o_ref):
    half = ga0_ref.shape[1]
    ya = ya_ref[...]
    yb = yb_ref[...]
    parts = []
    for c, (ga_ref, gb_ref) in enumerate(((ga0_ref, gb0_ref), (ga1_ref, gb1_ref))):
        cs = slice(c * half, (c + 1) * half)
        ma = jnp.dot(ya, pa_ref[:, cs], preferred_element_type=F32)
        mb = jnp.dot(yb, pb_ref[:, cs], preferred_element_type=F32)
        parts.append((_sigmoid(ga_ref[...].astype(F32)) * ma + _sigmoid(gb_ref[...].astype(F32)) * mb).astype(BF16))
    mix = (jnp.dot(parts[0], wo_ref[0:half, :], preferred_element_type=F32)
           + jnp.dot(parts[1], wo_ref[half:2 * half, :], preferred_element_type=F32))
    o_ref[...] = x_ref[...] + g_ref[0] * mix


def _mixout(ya, yb, proj, pa, pb, wo, xf, g1, seq):
    t, w = ya.shape
    d = pa.shape[1]
    tm = min(MIX_TM, seq)
    per_b = seq // tm
    half = d // 2
    ga0 = (3 * RWKV_WIDTH + 4 * HGRN_WIDTH) // half
    gate = lambda j: pl.BlockSpec((tm, half), lambda i: (i, ga0 + j))
    const = lambda a: pl.BlockSpec(a.shape, lambda i: (0, 0), pipeline_mode=pl.Buffered(1))
    return pl.pallas_call(
        _mixout_kernel,
        grid=(t // tm,),
        in_specs=[pl.BlockSpec((tm, w), lambda i: (i, 0)),
                  pl.BlockSpec((tm, w), lambda i: (i, 0)),
                  gate(0), gate(1), gate(2), gate(3),
                  const(pa), const(pb), const(wo),
                  pl.BlockSpec((tm, d), lambda i: (i, 0)),
                  pl.BlockSpec((1, 1, d), lambda i: (i // per_b, 0, 0))],
        out_specs=pl.BlockSpec((tm, d), lambda i: (i, 0)),
        out_shape=jax.ShapeDtypeStruct((t, d), F32),
        compiler_params=_cp(("arbitrary",)),
        name="mixout",
    )(ya, yb, proj, proj, proj, proj, pa, pb, wo, xf, g1)


def _route_kernel(x_ref, g_ref, sc_ref, sh_ref, rw_ref, rb_ref, h_ref, idx_ref, gate_ref, rank_ref, cnt_ref,
                  carry_ref):
    tb = x_ref.shape[0]

    @pl.when(pl.program_id(0) == 0)
    def _():
        carry_ref[...] = jnp.zeros_like(carry_ref)

    h = (_rms(x_ref[...], NORM_EPS) * g_ref[...]) * (1.0 + sc_ref[0]) + sh_ref[0]
    h_ref[...] = h
    logits = jnp.dot(h, rw_ref[...], preferred_element_type=F32, precision=lax.Precision.HIGHEST) + rb_ref[...]

    lane = lax.broadcasted_iota(jnp.int32, logits.shape, 1).astype(F32)
    work = logits
    vals, idxs, sels = [], [], []
    for _ in range(TOP_K):
        m = jnp.max(work, axis=-1, keepdims=True)
        idx = jnp.min(jnp.where(work == m, lane, float(N_EXPERTS)), axis=-1, keepdims=True)
        sel = lane == idx
        vals.append(m)
        idxs.append(idx)
        sels.append(sel)
        work = jnp.where(sel, -jnp.inf, work)
    exps = [jnp.exp(v - vals[0]) for v in vals]
    den = exps[0] + exps[1] + exps[2] + exps[3]

    onehot = jnp.zeros(logits.shape, F32)
    for sel in sels:
        onehot = onehot + jnp.where(sel, 1.0, 0.0)
    ti = lax.broadcasted_iota(jnp.int32, (tb, tb), 0)
    tj = lax.broadcasted_iota(jnp.int32, (tb, tb), 1)
    below = jnp.where(tj < ti, 1.0, 0.0).astype(BF16)
    before = jnp.dot(below, onehot.astype(BF16), preferred_element_type=F32) + carry_ref[...]
    carry_ref[...] = carry_ref[...] + jnp.sum(onehot, axis=0, keepdims=True)
    cnt_ref[...] = carry_ref[...]

    out_lane = lax.broadcasted_iota(jnp.int32, (tb, LANES), 1)
    idx_out = jnp.zeros((tb, LANES), jnp.int32)
    rank_out = jnp.zeros((tb, LANES), jnp.int32)
    gate_out = jnp.zeros((tb, LANES), F32)
    for j in range(TOP_K):
        rank = jnp.sum(jnp.where(sels[j], before, 0.0), axis=-1, keepdims=True)
        idx_out = jnp.where(out_lane == j, idxs[j].astype(jnp.int32), idx_out)
        rank_out = jnp.where(out_lane == j, rank.astype(jnp.int32), rank_out)
        gate_out = jnp.where(out_lane == j, exps[j] / den, gate_out)
    idx_ref[...] = idx_out
    rank_ref[...] = rank_out
    gate_ref[...] = gate_out


def _route(x1, g, sc, sh, rw, rb, seq):
    t, d = x1.shape
    tb = min(ROUTE_TB, seq)
    per_b = seq // tb
    e = rw.shape[1]
    wide = lambda dt: jax.ShapeDtypeStruct((t, LANES), dt)
    return pl.pallas_call(
        _route_kernel,
        grid=(t // tb,),
        in_specs=[pl.BlockSpec((tb, d), lambda i: (i, 0)),
                  pl.BlockSpec((1, d), lambda i: (0, 0)),
                  pl.BlockSpec((1, 1, d), lambda i: (i // per_b, 0, 0)),
                  pl.BlockSpec((1, 1, d), lambda i: (i // per_b, 0, 0)),
                  pl.BlockSpec((d, e), lambda i: (0, 0)),
                  pl.BlockSpec((1, e), lambda i: (0, 0))],
        out_specs=[pl.BlockSpec((tb, d), lambda i: (i, 0)),
                   pl.BlockSpec((tb, LANES), lambda i: (i, 0)),
                   pl.BlockSpec((tb, LANES), lambda i: (i, 0)),
                   pl.BlockSpec((tb, LANES), lambda i: (i, 0)),
                   pl.BlockSpec((1, e), lambda i: (0, 0))],
        out_shape=[jax.ShapeDtypeStruct((t, d), F32), wide(jnp.int32), wide(F32), wide(jnp.int32),
                   jax.ShapeDtypeStruct((1, e), F32)],
        scratch_shapes=[pltpu.VMEM((1, e), F32)],
        compiler_params=_cp(("arbitrary",)),
        name="route",
    )(x1, g.reshape(1, d), sc, sh, rw, rb.reshape(1, e))


def _scatter_kernel(dest_ref, zflag_ref, h_ref, xs_ref, zbuf, sem, zsem):
    tb = h_ref.shape[0]
    zrows = zbuf.shape[0]
    base = pl.program_id(0) * tb

    @pl.when(pl.program_id(0) == 0)
    def _():
        zbuf[...] = jnp.zeros_like(zbuf)

        def zcopy(b):
            return pltpu.make_async_copy(zbuf, xs_ref.at[pl.ds(pl.multiple_of(b * zrows, zrows), zrows), :], zsem)

        def zstart(b, carry):
            @pl.when(zflag_ref[b] != 0)
            def _():
                zcopy(b).start()
            return carry

        def zwait(b, carry):
            @pl.when(zflag_ref[b] != 0)
            def _():
                zcopy(b).wait()
            return carry

        n_blk = xs_ref.shape[0] // zrows
        lax.fori_loop(0, n_blk, zstart, 0)
        lax.fori_loop(0, n_blk, zwait, 0)

    def copy(t, j):
        d = dest_ref[(base + t) * TOP_K + j]
        return pltpu.make_async_copy(h_ref.at[pl.ds(t, 1), :], xs_ref.at[pl.ds(d, 1), :], sem)

    def start(t, carry):
        for j in range(TOP_K):
            copy(t, j).start()
        return carry

    lax.fori_loop(0, tb, start, 0, unroll=ROW_DMA_UNROLL)
    for j in range(TOP_K):
        pltpu.make_async_copy(h_ref, xs_ref.at[pl.ds(0, tb), :], sem).wait()


def _scatter_rows(dest_flat, zero_flag, h2, n_rows):
    t, d = h2.shape
    tb = min(SCATTER_TB, t)
    return pl.pallas_call(
        _scatter_kernel,
        grid_spec=pltpu.PrefetchScalarGridSpec(
            num_scalar_prefetch=2,
            grid=(t // tb,),
            in_specs=[pl.BlockSpec((tb, d), lambda i, dest, zf: (i, 0))],
            out_specs=pl.BlockSpec(memory_space=pl.ANY),
            scratch_shapes=[pltpu.VMEM((MOE_TM, d), F32), pltpu.SemaphoreType.DMA(()),
                            pltpu.SemaphoreType.DMA(())],
        ),
        out_shape=jax.ShapeDtypeStruct((n_rows, d), F32),
        compiler_params=_cp(("arbitrary",)),
        name="scatter_rows",
    )(dest_flat, zero_flag, h2)


def _expert_kernel(be_ref, bs_ref, nu_ref, x_ref, bgu_ref, bd_ref, wgu_hbm, wd_hbm, o_ref,
                   xb_ref, act_ref, wbuf, sem):
    del bs_ref
    i = pl.program_id(0)
    n = pl.program_id(1)
    n_steps = pl.num_programs(1)
    nt, _, tn = act_ref.shape
    n_slot, de, tnb = wbuf.shape
    look = n_slot - 1
    g = i * n_steps + n
    n_live = nu_ref[0] * n_steps
    live = i < nu_ref[0]

    def up_copies(blk, step, slot):
        e = be_ref[blk]
        col = pl.multiple_of(step * tn, tn)
        glu = pltpu.make_async_copy(wgu_hbm.at[e, :, pl.ds(col, tn)], wbuf.at[slot, :, 0:tn], sem.at[slot])
        lin = pltpu.make_async_copy(wgu_hbm.at[e, :, pl.ds(pl.multiple_of(de + col, tn), tn)],
                                    wbuf.at[slot, :, tn:2 * tn], sem.at[slot])
        return glu, lin

    def down_copy(blk, step, slot):
        colb = pl.multiple_of((step - nt) * tnb, tnb)
        return pltpu.make_async_copy(wd_hbm.at[be_ref[blk], :, pl.ds(colb, tnb)], wbuf.at[slot], sem.at[slot])

    def start(gs):
        blk = gs // n_steps
        step = gs - blk * n_steps
        slot = gs % n_slot

        @pl.when((gs < n_live) & (step < nt))
        def _():
            glu, lin = up_copies(blk, step, slot)
            glu.start()
            lin.start()

        @pl.when((gs < n_live) & (step >= nt))
        def _():
            down_copy(blk, step, slot).start()

    @pl.when(g == 0)
    def _():
        for gs in range(look):
            start(jnp.int32(gs))

    start(g + look)
    slot = g % n_slot

    @pl.when(live & (n == 0))
    def _():
        xb_ref[...] = x_ref[...].astype(BF16)

    @pl.when(live & (n < nt))
    def _():
        glu_cp, lin_cp = up_copies(i, n, slot)
        glu_cp.wait()
        lin_cp.wait()
        xb = xb_ref[...]
        w = wbuf[slot]
        glu = jnp.dot(xb, w[:, 0:tn].astype(BF16), preferred_element_type=F32) + bgu_ref[0, n]
        lin = jnp.dot(xb, w[:, tn:2 * tn].astype(BF16), preferred_element_type=F32) + bgu_ref[0, n + nt]
        glu = jnp.minimum(glu, SWIGLU_LIMIT)
        lin = jnp.clip(lin, -SWIGLU_LIMIT, SWIGLU_LIMIT)
        act_ref[n] = (glu * _sigmoid(SWIGLU_ALPHA * glu) * (lin + 1.0)).astype(BF16)

    for m in range(o_ref.shape[1] // tnb):
        @pl.when(live & (n == nt + m))
        def _():
            down_copy(i, n, slot).wait()
            wd = wbuf[slot]
            y = jnp.dot(act_ref[0], wd[0:tn].astype(BF16), preferred_element_type=F32)
            for k in range(1, nt):
                y = y + jnp.dot(act_ref[k], wd[k * tn:(k + 1) * tn].astype(BF16), preferred_element_type=F32)
            o_ref[:, m * tnb:(m + 1) * tnb] = y + bd_ref[0, m]

    @pl.when(jnp.logical_not(live) & (n == 0))
    def _():
        o_ref[...] = jnp.zeros_like(o_ref)


def _experts(blk_expert, blk_src, n_used, xs, w_gu, b_gu, w_down, b_down):
    n_rows, d = xs.shape
    tm, tn, tnb = MOE_TM, MOE_TN, MOE_TNB
    nb = n_rows // tm
    e, _, two_de = w_gu.shape
    de = two_de // 2
    nt = de // tn
    ntb = d // tnb
    assert 2 * tn == tnb and de == d
    b_gu4 = b_gu.reshape(e, 2 * nt, 1, tn)
    b_d4 = b_down.reshape(e, ntb, 1, tnb)
    return pl.pallas_call(
        _expert_kernel,
        grid_spec=pltpu.PrefetchScalarGridSpec(
            num_scalar_prefetch=3,
            grid=(nb, nt + ntb),
            in_specs=[pl.BlockSpec((tm, d), lambda i, n, be, bs, nu: (bs[i], 0)),
                      pl.BlockSpec((1, 2 * nt, 1, tn), lambda i, n, be, bs, nu: (be[i], 0, 0, 0)),
                      pl.BlockSpec((1, ntb, 1, tnb), lambda i, n, be, bs, nu: (be[i], 0, 0, 0)),
                      pl.BlockSpec(memory_space=pl.ANY),
                      pl.BlockSpec(memory_space=pl.ANY)],
            out_specs=pl.BlockSpec((tm, d), lambda i, n, be, bs, nu: (i, 0)),
            scratch_shapes=[pltpu.VMEM((tm, d), BF16), pltpu.VMEM((nt, tm, tn), BF16),
                            pltpu.VMEM((MOE_RING, de, tnb), F32), pltpu.SemaphoreType.DMA((MOE_RING,))],
        ),
        out_shape=jax.ShapeDtypeStruct((n_rows, d), F32),
        compiler_params=_cp(("arbitrary", "arbitrary")),
        name="experts",
    )(blk_expert, blk_src, n_used, xs, b_gu4, b_d4, w_gu, w_down)


def _combine_kernel(dest_ref, ys_ref, gate_ref, x_ref, g2_ref, fg_ref, o_ref, buf_ref, sem):
    tb = x_ref.shape[0]
    i = pl.program_id(0)
    half = i % 2

    def gather(blk, slot):
        def start(t, carry):
            for j in range(TOP_K):
                d = dest_ref[(blk * tb + t) * TOP_K + j]
                pltpu.make_async_copy(ys_ref.at[pl.ds(d, 1), :], buf_ref.at[slot, j, pl.ds(t, 1), :],
                                      sem.at[slot]).start()
            return carry
        lax.fori_loop(0, tb, start, 0, unroll=ROW_DMA_UNROLL)

    @pl.when(i == 0)
    def _():
        gather(i, half)

    @pl.when(i + 1 < pl.num_programs(0))
    def _():
        gather(i + 1, 1 - half)

    for j in range(TOP_K):
        pltpu.make_async_copy(ys_ref.at[pl.ds(0, tb), :], buf_ref.at[half, j], sem.at[half]).wait()

    gate = gate_ref[...]
    ffn = gate[:, 0:1] * buf_ref[half, 0]
    for j in range(1, TOP_K):
        ffn = ffn + gate[:, j:j + 1] * buf_ref[half, j]
    x2 = x_ref[...] + g2_ref[0] * ffn
    o_ref[...] = _rms(x2, NORM_EPS) * fg_ref[...]


def _combine(dest_flat, ys, gates, x1, g2, fg, seq):
    t, d = x1.shape
    tb = min(COMBINE_TB, seq)
    per_b = seq // tb
    return pl.pallas_call(
        _combine_kernel,
        grid_spec=pltpu.PrefetchScalarGridSpec(
            num_scalar_prefetch=1,
            grid=(t // tb,),
            in_specs=[pl.BlockSpec(memory_space=pl.ANY),
                      pl.BlockSpec((tb, LANES), lambda i, dest: (i, 0)),
                      pl.BlockSpec((tb, d), lambda i, dest: (i, 0)),
                      pl.BlockSpec((1, 1, d), lambda i, dest: (i // per_b, 0, 0)),
                      pl.BlockSpec((1, d), lambda i, dest: (0, 0))],
            out_specs=pl.BlockSpec((tb, d), lambda i, dest: (i, 0)),
            scratch_shapes=[pltpu.VMEM((2, TOP_K, tb, d), F32), pltpu.SemaphoreType.DMA((2,))],
        ),
        out_shape=jax.ShapeDtypeStruct((t, d), F32),
        compiler_params=_cp(("arbitrary",)),
        name="combine",
    )(dest_flat, ys, gates, x1, g2, fg.reshape(1, d))


def _moe(x1, g, sc2, sh2, g2, router_w, router_b, w_gu, b_gu, w_down, b_down, final_g, seq):
    t, d = x1.shape
    a = t * TOP_K
    tm = MOE_TM
    nb = -(-a // tm) + N_EXPERTS
    h2, idx_w, gate_w, rank_w, counts = _route(x1, g, sc2, sh2, router_w, router_b, seq)
    idx = idx_w[:, :TOP_K]
    rank = rank_w[:, :TOP_K]
    cnt = counts.reshape(-1).astype(jnp.int32)
    padded = ((cnt + tm - 1) // tm) * tm
    pad_end = jnp.cumsum(padded)
    pad_start = pad_end - padded
    experts = jnp.arange(N_EXPERTS, dtype=jnp.int32)
    start_of = jnp.sum(jnp.where(idx[:, :, None] == experts, pad_start, 0), axis=-1)
    dest = (start_of + rank).reshape(a).astype(jnp.int32)
    n_used = (pad_end[-1] // tm).astype(jnp.int32)
    blk = jnp.arange(nb, dtype=jnp.int32)
    blk_src = jnp.clip(blk, 0, jnp.maximum(n_used - 1, 0))
    blk_expert = jnp.sum((pad_end[None, :] <= (blk_src * tm)[:, None]).astype(jnp.int32), axis=1)
    blk_expert = jnp.minimum(blk_expert, N_EXPERTS - 1)
    last_blk = jnp.where(padded > 0, pad_end // tm - 1, -1)
    zero_flag = ((blk >= n_used) | jnp.any(last_blk[None, :] == blk[:, None], axis=1)).astype(jnp.int32)
    xs = _scatter_rows(dest, zero_flag, h2, nb * tm)
    ys = _experts(blk_expert, blk_src, n_used.reshape(1), xs, w_gu, b_gu, w_down, b_down)
    return _combine(dest, ys, gate_w, x1, g2, final_g, seq)


def kernel(x, c, ada_w, ada_b, norm1_g, norm2_g, w_in, rwkv_mu, rwkv_w0, rwkv_w2, rwkv_a0, rwkv_a2, rwkv_g2,
           rwkv_k_k, rwkv_k_a, rwkv_r_k, rwkv_gn_g, rwkv_gn_b, hgrn_lb_logits, hgrn_gn_g, proj_a, proj_b, w_out,
           router_w, router_b, exp_w_gate_up, exp_b_gate_up, exp_w_down, exp_b_down, final_norm_g):
    bsz, seq, d = x.shape
    depth = ada_w.shape[0]
    lower_bounds = jnp.cumsum(jax.nn.softmax(hgrn_lb_logits.astype(F32), axis=0), axis=0)
    xf = x.reshape(bsz * seq, d)
    rkv = 3 * RWKV_WIDTH
    rwkv_cols = rkv + LORA_COLS
    hg_gate = 4 * HGRN_WIDTH + 2 * d
    for l in range(depth):
        mod = _adaln(c, ada_w[l], ada_b[l])
        sh1, sc1, g1, sh2, sc2, g2 = [m.reshape(bsz, 1, d) for m in jnp.split(mod, 6, axis=-1)]

        wl = w_in[l]
        w_cat = jnp.concatenate([wl[:, :rkv], wl[:, rwkv_cols:rwkv_cols + hg_gate], wl[:, rkv:rwkv_cols],
                                 jnp.zeros((d, LORA_PAD - LORA_COLS), wl.dtype)], axis=1).astype(BF16)
        proj = _inproj(xf, norm1_g[l], sc1, sh1, w_cat, seq)

        mu = rwkv_mu[l]
        mu_lo = jnp.pad(mu[rkv:], (0, LORA_PAD - LORA_COLS))
        pad_rows = lambda w, r0: jnp.zeros((LORA_PAD, RWKV_WIDTH), F32).at[r0:r0 + w.shape[0]].set(w).astype(BF16)
        ya = _rwkv(proj, bsz, seq, mu[:rkv], mu_lo, rwkv_w0[l], rwkv_a0[l], rwkv_k_k[l], rwkv_k_a[l],
                   rwkv_r_k[l].reshape(-1), rwkv_gn_g[l], rwkv_gn_b[l],
                   pad_rows(rwkv_w2[l], 0), pad_rows(rwkv_a2[l], DECAY_LORA),
                   pad_rows(rwkv_g2[l], DECAY_LORA + AAA_LORA))
        yb = _hgrn(proj, bsz, seq, lower_bounds[l], hgrn_gn_g[l])
        x1 = _mixout(ya, yb, proj, proj_a[l].astype(BF16), proj_b[l].astype(BF16), w_out[l].astype(BF16), xf, g1, seq)
        assert depth == 1
        out = _moe(x1, norm2_g[l], sc2, sh2, g2, router_w[l], router_b[l], exp_w_gate_up[l], exp_b_gate_up[l],
                   exp_w_down[l], exp_b_down[l], final_norm_g, seq)
    return out.reshape(bsz, seq, d)
```

```python
import functools

import jax
import jax.numpy as jnp
from jax import lax
from jax.experimental import pallas as pl
from jax.experimental.pallas import tpu as pltpu

F32 = jnp.float32
BF16 = jnp.bfloat16

D_MODEL = 2048
RWKV_WIDTH = 1024
RWKV_HEAD = 64
DECAY_LORA = 64
AAA_LORA = 64
GATE_LORA = 160
LORA_COLS = DECAY_LORA + AAA_LORA + GATE_LORA
LORA_PAD = 512
RWKV_GN_EPS = 64e-5
HGRN_WIDTH = 1024
HGRN_HEAD = 128
GLA_BLOCK = 16
N_EXPERTS = 32
TOP_K = 4
D_EXPERT = 2048
SWIGLU_ALPHA = 1.702
SWIGLU_LIMIT = 7.0
NORM_EPS = 1e-5

LANES = 128
VMEM_LIMIT = 56 * 1024 * 1024

RWKV_CHUNK = 64
HGRN_STEP = 128
PROJ_TM = 1024
PROJ_TN = 512
MIX_TM = 256
MOE_TM = 1024
MOE_TN = 256
MOE_TNB = 512
MOE_RING = 4
ROUTE_TB = 256
SCATTER_TB = 256
COMBINE_TB = 128
ROW_DMA_UNROLL = 8


def _cp(sem, vmem=VMEM_LIMIT):
    return pltpu.CompilerParams(dimension_semantics=sem, vmem_limit_bytes=vmem)


def _mm(a, b):
    return jnp.dot(a.astype(BF16), b.astype(BF16), preferred_element_type=F32)


def _mm_nt(a, b):
    return lax.dot_general(a.astype(BF16), b.astype(BF16), (((1,), (1,)), ((), ())),
                           preferred_element_type=F32)


def _mm_tn(a, b):
    return lax.dot_general(a.astype(BF16), b.astype(BF16), (((0,), (0,)), ((), ())),
                           preferred_element_type=F32)


def _split3(x):
    hi = x.astype(BF16)
    r1 = x - hi.astype(F32)
    mid = r1.astype(BF16)
    lo = (r1 - mid.astype(F32)).astype(BF16)
    return hi, mid, lo


def _mm_exact_rhs(m_bf16, x):
    hi, mid, lo = _split3(x)
    d = lambda p: jnp.dot(m_bf16, p, preferred_element_type=F32)
    return d(hi) + d(mid) + d(lo)


_HI_HALF = 0xFFFF0000


def _pack_bf16_pair(lo, hi):
    lo_bits = pltpu.bitcast(lo.astype(BF16).astype(F32), jnp.uint32) >> 16
    hi_bits = pltpu.bitcast(hi.astype(BF16).astype(F32), jnp.uint32) & jnp.uint32(_HI_HALF)
    return lo_bits | hi_bits


def _unpack_bf16_pair(w):
    return pltpu.bitcast(w << 16, F32), pltpu.bitcast(w & jnp.uint32(_HI_HALF), F32)


def _sigmoid(x):
    return 1.0 / (1.0 + jnp.exp(-x))


def _softplus(x):
    return jnp.maximum(x, 0.0) + jnp.log(1.0 + jnp.exp(-jnp.abs(x)))


def _rms(x, eps):
    return x * lax.rsqrt(jnp.mean(x * x, axis=-1, keepdims=True) + eps)


def _adaln_kernel(c_ref, w_ref, b_ref, o_ref):
    c = c_ref[...]
    c_act = c * _sigmoid(c)
    o_ref[...] = jnp.dot(c_act, w_ref[...], preferred_element_type=F32,
                         precision=lax.Precision.HIGHEST) + b_ref[...]


def _adaln(c, w, b):
    bsz, d = c.shape
    n = w.shape[1]
    tn = 1024
    return pl.pallas_call(
        _adaln_kernel,
        grid=(n // tn,),
        in_specs=[pl.BlockSpec((bsz, d), lambda j: (0, 0)),
                  pl.BlockSpec((d, tn), lambda j: (0, j)),
                  pl.BlockSpec((1, tn), lambda j: (0, j))],
        out_specs=pl.BlockSpec((bsz, tn), lambda j: (0, j)),
        out_shape=jax.ShapeDtypeStruct((bsz, n), F32),
        compiler_params=_cp(("arbitrary",)),
        name="adaln",
    )(c, w, b.reshape(1, n))


def _inproj_kernel(x_ref, g_ref, sc_ref, sh_ref, w_ref, o_ref, h_ref):
    @pl.when(pl.program_id(1) == 0)
    def _():
        y = _rms(x_ref[...], NORM_EPS) * g_ref[...]
        h_ref[...] = (y * (1.0 + sc_ref[0]) + sh_ref[0]).astype(BF16)

    o_ref[...] = jnp.dot(h_ref[...], w_ref[...], preferred_element_type=F32).astype(o_ref.dtype)


def _inproj(xf, g, sc, sh, w, seq):
    t, d = xf.shape
    n = w.shape[1]
    tm = min(PROJ_TM, seq)
    tn = PROJ_TN
    per_b = seq // tm
    return pl.pallas_call(
        _inproj_kernel,
        grid=(t // tm, n // tn),
        in_specs=[pl.BlockSpec((tm, d), lambda i, j: (i, 0)),
                  pl.BlockSpec((1, d), lambda i, j: (0, 0)),
                  pl.BlockSpec((1, 1, d), lambda i, j: (i // per_b, 0, 0)),
                  pl.BlockSpec((1, 1, d), lambda i, j: (i // per_b, 0, 0)),
                  pl.BlockSpec((d, tn), lambda i, j: (0, j))],
        out_specs=pl.BlockSpec((tm, tn), lambda i, j: (i, j)),
        out_shape=jax.ShapeDtypeStruct((t, n), BF16),
        scratch_shapes=[pltpu.VMEM((tm, d), BF16)],
        compiler_params=_cp(("arbitrary", "arbitrary")),
        name="inproj",
    )(xf, g.reshape(1, d), sc, sh, w)


def _rwkv_kernel(r_ref, k_ref, v_ref, lo_ref, mu_ref, mulo_ref, w0_ref, a0_ref, kk_ref, ka_ref, rk_ref,
                 gng_ref, gnb_ref, w2_ref, a2_ref, g2_ref, o_ref,
                 st_ref, cr_ref, ck_ref, cv_ref, clo_ref):
    L = RWKV_CHUNK
    H2 = 2 * L
    n_pair = RWKV_WIDTH // LANES

    @pl.when(pl.program_id(1) == 0)
    def _():
        st_ref[...] = jnp.zeros_like(st_ref)
        cr_ref[...] = jnp.zeros_like(cr_ref)
        ck_ref[...] = jnp.zeros_like(ck_ref)
        cv_ref[...] = jnp.zeros_like(cv_ref)
        clo_ref[...] = jnp.zeros_like(clo_ref)

    def shift(x_ref, mu, carry_ref):
        x = x_ref[...].astype(F32)
        rows = lax.broadcasted_iota(jnp.int32, x.shape, 0)
        prev = jnp.where(rows == 0, carry_ref[...], pltpu.roll(x, 1, axis=0))
        carry_ref[...] = x[L - 1:L, :]
        return x + (prev - x) * mu

    mu = mu_ref[...]
    r = shift(r_ref, mu[:, 0:RWKV_WIDTH], cr_ref)
    k = shift(k_ref, mu[:, RWKV_WIDTH:2 * RWKV_WIDTH], ck_ref)
    v = shift(v_ref, mu[:, 2 * RWKV_WIDTH:3 * RWKV_WIDTH], cv_ref)
    lo = shift(lo_ref, mulo_ref[...], clo_ref)

    wpre = w0_ref[...] + _mm(jnp.tanh(lo), w2_ref[...])
    lw = -jnp.exp(-_softplus(-wpre) - 0.5)
    a = _sigmoid(a0_ref[...] + _mm(lo, a2_ref[...]))
    g = _mm(_sigmoid(lo), g2_ref[...])

    ti = lax.broadcasted_iota(jnp.int32, (L, L), 0)
    tj = lax.broadcasted_iota(jnp.int32, (L, L), 1)
    tri = jnp.where(tj <= ti, 1.0, 0.0).astype(BF16)
    cl = _mm_exact_rhs(tri, lw)
    cl_last = cl[L - 1:L, :]
    e_pos = jnp.exp(cl)
    e_neg = jnp.exp(-cl)
    e_prev = jnp.exp(cl - lw)
    e_end = jnp.exp(cl_last - cl)
    p_last = jnp.exp(cl_last)

    kk = k * kk_ref[...]
    k2 = k * (1.0 + (a - 1.0) * ka_ref[...])
    f_b = a * e_neg
    f_bend = a * e_end
    rk = rk_ref[...]
    gng = gng_ref[...]
    gnb = gnb_ref[...]

    lane = lax.broadcasted_iota(jnp.int32, (L, LANES), 1)
    lo_half = lane < RWKV_HEAD
    si = lax.broadcasted_iota(jnp.int32, (H2, H2), 0)
    sj = lax.broadcasted_iota(jnp.int32, (H2, H2), 1)
    same = (si // L) == (sj // L)
    m_strict = same & (sj < si)
    m_incl = same & (sj <= si)
    eye = si == sj
    own = (si // L) == (sj // RWKV_HEAD)

    def stack(x):
        return jnp.concatenate([jnp.where(lo_half, x, 0.0), jnp.where(lo_half, 0.0, x)], axis=0)

    def rep(x):
        return jnp.concatenate([x, x], axis=0)

    pairs = range(n_pair)
    sls = [slice(p * LANES, (p + 1) * LANES) for p in pairs]
    at, bt, bend, kt, kend, rt, v_s, bonus = [], [], [], [], [], [], [], []
    for sl in sls:
        kk_s = stack(kk[:, sl])
        nrm = jnp.sqrt(jnp.sum(kk_s * kk_s, axis=-1, keepdims=True))
        kkn = kk_s / jnp.maximum(nrm, 1e-12)
        at.append(-kkn * rep(e_prev[:, sl]))
        bt.append(kkn * rep(f_b[:, sl]))
        bend.append(kkn * rep(f_bend[:, sl]))
        k2_s = stack(k2[:, sl])
        kt.append(k2_s * rep(e_neg[:, sl]))
        kend.append(k2_s * rep(e_end[:, sl]))
        r_s = stack(r[:, sl])
        rt.append(r_s * rep(e_pos[:, sl]))
        vs = stack(v[:, sl])
        v_s.append(vs)
        bonus.append(jnp.sum(r_s * k2_s * rk[:, sl], axis=-1, keepdims=True) * vs)

    gm = [_mm_nt(jnp.concatenate([at[p], rt[p]], axis=0), jnp.concatenate([bt[p], kt[p]], axis=0))
          for p in pairs]
    a_ab = [jnp.where(m_strict, gm[p][:H2, :H2], 0.0) for p in pairs]
    a_ak = [jnp.where(m_strict, gm[p][:H2, H2:], 0.0) for p in pairs]
    a_rb = [jnp.where(m_incl, gm[p][H2:, :H2], 0.0) for p in pairs]
    a_rk = [jnp.where(m_incl, gm[p][H2:, H2:], 0.0) for p in pairs]

    ident = jnp.where(eye, 1.0, 0.0)
    tinv = [ident + a_ab[p] for p in pairs]
    pw = a_ab
    akv = [_mm(a_ak[p], v_s[p]) for p in pairs]
    ark_v = [_mm(a_rk[p], v_s[p]) for p in pairs]
    kend_v = [_mm_tn(kend[p], v_s[p]) for p in pairs]
    for _ in range(max(1, (L - 1).bit_length() - 1)):
        pw = [_mm(pw[p], pw[p]) for p in pairs]
        tinv = [tinv[p] + _mm(tinv[p], pw[p]) for p in pairs]

    tx = [_mm(tinv[p], jnp.concatenate([at[p], akv[p]], axis=1)) for p in pairs]
    y2 = [_mm(a_rb[p], tx[p]) for p in pairs]
    btx = [_mm_tn(bend[p], tx[p]) for p in pairs]
    s0 = [st_ref[p] for p in pairs]
    y = [_mm(rt[p] + y2[p][:, :LANES], s0[p]) + (y2[p][:, LANES:] + ark_v[p]) for p in pairs]
    for p in pairs:
        m_mat = jnp.where(eye, jnp.broadcast_to(p_last[:, sls[p]], (H2, LANES)), 0.0) + btx[p][:, :LANES]
        st_ref[p] = _mm(m_mat, s0[p]) + (btx[p][:, LANES:] + kend_v[p])

    for p in pairs:
        sl = sls[p]
        mean = jnp.sum(y[p], axis=-1, keepdims=True) * (1.0 / RWKV_HEAD)
        dlt = jnp.where(own, y[p] - mean, 0.0)
        var = jnp.sum(dlt * dlt, axis=-1, keepdims=True) * (1.0 / RWKV_HEAD)
        yn = dlt * lax.rsqrt(var + RWKV_GN_EPS) * gng[:, sl] + jnp.where(own, gnb[:, sl], 0.0)
        ys = yn + bonus[p]
        o_ref[:, sl] = ((ys[:L] + ys[L:]) * g[:, sl]).astype(o_ref.dtype)


def _rwkv(proj, bsz, seq, mu_rkv, mu_lo, w0, a0, k_k, k_a, r_k, gn_g, gn_b, w2p, a2p, g2p):
    L = RWKV_CHUNK
    nc = seq // L
    W = RWKV_WIDTH
    row = lambda n: pl.BlockSpec((1, n), lambda b, c: (0, 0))
    full = lambda a: pl.BlockSpec(a.shape, lambda b, c: (0, 0))
    lo_blk = proj.shape[1] // LORA_PAD - 1
    v1 = lambda a: a.reshape(1, -1)
    return pl.pallas_call(
        _rwkv_kernel,
        grid=(bsz, nc),
        in_specs=[pl.BlockSpec((L, W), lambda b, c: (b * nc + c, 0)),
                  pl.BlockSpec((L, W), lambda b, c: (b * nc + c, 1)),
                  pl.BlockSpec((L, W), lambda b, c: (b * nc + c, 2)),
                  pl.BlockSpec((L, LORA_PAD), lambda b, c: (b * nc + c, lo_blk)),
                  row(3 * W), row(LORA_PAD), row(W), row(W), row(W), row(W), row(W), row(W), row(W),
                  full(w2p), full(a2p), full(g2p)],
        out_specs=pl.BlockSpec((L, W), lambda b, c: (b * nc + c, 0)),
        out_shape=jax.ShapeDtypeStruct((bsz * seq, W), BF16),
        scratch_shapes=[pltpu.VMEM((W // LANES, LANES, LANES), F32),
                        pltpu.VMEM((1, W), F32), pltpu.VMEM((1, W), F32), pltpu.VMEM((1, W), F32),
                        pltpu.VMEM((1, LORA_PAD), F32)],
        compiler_params=_cp(("arbitrary", "arbitrary")),
        name="rwkv7",
    )(proj, proj, proj, proj, v1(mu_rkv), v1(mu_lo), v1(w0), v1(a0), v1(k_k), v1(k_a), v1(r_k), v1(gn_g),
      v1(gn_b), w2p, a2p, g2p)


def _hgrn_kernel(q_ref, f_ref, i_ref, og_ref, lb_ref, gn_ref, o_ref, st_ref):
    n = HGRN_STEP
    blk = GLA_BLOCK
    nb = n // blk
    n_head = HGRN_WIDTH // HGRN_HEAD

    @pl.when(pl.program_id(1) == 0)
    def _():
        st_ref[...] = jnp.zeros_like(st_ref)

    lb = lb_ref[...]
    qf = q_ref[...].astype(F32)
    q = qf * _sigmoid(qf)
    forget = lb + (1.0 - lb) * _sigmoid(f_ref[...].astype(F32))
    log_f = jnp.log(forget)
    kx = 1.0 - forget
    vv = i_ref[...].astype(F32)
    og = og_ref[...].astype(F32)

    ti = lax.broadcasted_iota(jnp.int32, (n, n), 0)
    tj = lax.broadcasted_iota(jnp.int32, (n, n), 1)
    same = (ti // blk) == (tj // blk)
    causal = same & (tj <= ti)
    b = _mm_exact_rhs(jnp.where(causal, 1.0, 0.0).astype(BF16), log_f)
    b_last = _mm_exact_rhs(jnp.where(same, 1.0, 0.0).astype(BF16), log_f)
    q_dec = q * jnp.exp(b)
    k_dec = kx * jnp.exp(-b)
    k_end = kx * jnp.exp(b_last - b)
    d_end = jnp.exp(b_last)
    rows = lax.broadcasted_iota(jnp.int32, (n, HGRN_HEAD), 0)
    gn = gn_ref[...]

    heads = range(n_head)
    sls = [slice(h * HGRN_HEAD, (h + 1) * HGRN_HEAD) for h in heads]
    attn = [jnp.where(causal, _mm_nt(q_dec[:, sl], k_dec[:, sl]), 0.0) for sl in sls]
    o_intra = [_mm(attn[h], vv[:, sls[h]]) for h in heads]
    vt = [vv[:, sl].T for sl in sls]
    blk_mask = [(rows // blk) == c for c in range(nb)]
    kv_t = [[_mm(vt[h], jnp.where(blk_mask[c], k_end[:, sls[h]], 0.0)) for c in range(nb)]
            for h in heads]
    st = [[st_ref[h]] for h in heads]
    for h in heads:
        for c in range(nb):
            st[h].append(st[h][c] * d_end[c * blk:c * blk + 1, sls[h]] + kv_t[h][c])
        st_ref[h] = st[h][nb]
    for h in heads:
        sl = sls[h]
        inter = [_mm_nt(q_dec[c * blk:(c + 1) * blk, sl], st[h][c]) for c in range(nb)]
        o = o_intra[h] + jnp.concatenate(inter, axis=0)
        o = _rms(o, NORM_EPS) * gn
        ogh = og[:, sl]
        o_ref[:, sl] = (o * (ogh * _sigmoid(ogh))).astype(o_ref.dtype)


def _hgrn(proj, bsz, seq, lb, gn_g):
    n = HGRN_STEP
    ns = seq // n
    W = HGRN_WIDTH
    col = lambda j: pl.BlockSpec((n, W), lambda b, c: (b * ns + c, j))
    return pl.pallas_call(
        _hgrn_kernel,
        grid=(bsz, ns),
        in_specs=[col(3), col(4), col(5), col(6),
                  pl.BlockSpec((1, W), lambda b, c: (0, 0)),
                  pl.BlockSpec((1, HGRN_HEAD), lambda b, c: (0, 0))],
        out_specs=pl.BlockSpec((n, W), lambda b, c: (b * ns + c, 0)),
        out_shape=jax.ShapeDtypeStruct((bsz * seq, W), BF16),
        scratch_shapes=[pltpu.VMEM((W // HGRN_HEAD, HGRN_HEAD, HGRN_HEAD), F32)],
        compiler_params=_cp(("arbitrary", "arbitrary")),
        name="hgrn2",
    )(proj, proj, proj, proj, lb.reshape(1, W), gn_g.reshape(1, HGRN_HEAD))


def _mixout_kernel(ya_ref, yb_ref, ga0_ref, ga1_ref, gb0_ref, gb1_ref, pa_ref, pb_ref, wo_ref, x_ref, g_ref, o_ref):
    half = ga0_ref.shape[1]
    ya = ya_ref[...]
    yb = yb_ref[...]
    parts = []
    for c, (ga_ref, gb_ref) in enumerate(((ga0_ref, gb0_ref), (ga1_ref, gb1_ref))):
        cs = slice(c * half, (c + 1) * half)
        ma = jnp.dot(ya, pa_ref[:, cs], preferred_element_type=F32)
        mb = jnp.dot(yb, pb_ref[:, cs], preferred_element_type=F32)
        parts.append((_sigmoid(ga_ref[...].astype(F32)) * ma + _sigmoid(gb_ref[...].astype(F32)) * mb).astype(BF16))
    mix = (jnp.dot(parts[0], wo_ref[0:half, :], preferred_element_type=F32)
           + jnp.dot(parts[1], wo_ref[half:2 * half, :], preferred_element_type=F32))
    o_ref[...] = x_ref[...] + g_ref[0] * mix


def _mixout(ya, yb, proj, pa, pb, wo, xf, g1, seq):
    t, w = ya.shape
    d = pa.shape[1]
    tm = min(MIX_TM, seq)
    per_b = seq // tm
    half = d // 2
    ga0 = (3 * RWKV_WIDTH + 4 * HGRN_WIDTH) // half
    gate = lambda j: pl.BlockSpec((tm, half), lambda i: (i, ga0 + j))
    const = lambda a: pl.BlockSpec(a.shape, lambda i: (0, 0), pipeline_mode=pl.Buffered(1))
    return pl.pallas_call(
        _mixout_kernel,
        grid=(t // tm,),
        in_specs=[pl.BlockSpec((tm, w), lambda i: (i, 0)),
                  pl.BlockSpec((tm, w), lambda i: (i, 0)),
                  gate(0), gate(1), gate(2), gate(3),
                  const(pa), const(pb), const(wo),
                  pl.BlockSpec((tm, d), lambda i: (i, 0)),
                  pl.BlockSpec((1, 1, d), lambda i: (i // per_b, 0, 0))],
        out_specs=pl.BlockSpec((tm, d), lambda i: (i, 0)),
        out_shape=jax.ShapeDtypeStruct((t, d), F32),
        compiler_params=_cp(("arbitrary",)),
        name="mixout",
    )(ya, yb, proj, proj, proj, proj, pa, pb, wo, xf, g1)


def _route_kernel(x_ref, g_ref, sc_ref, sh_ref, rw_ref, rb_ref, h_ref, idx_ref, gate_ref, rank_ref, cnt_ref,
                  carry_ref):
    tb = x_ref.shape[0]

    @pl.when(pl.program_id(0) == 0)
    def _():
        carry_ref[...] = jnp.zeros_like(carry_ref)

    h = (_rms(x_ref[...], NORM_EPS) * g_ref[...]) * (1.0 + sc_ref[0]) + sh_ref[0]
    h_ref[...] = h
    logits = jnp.dot(h, rw_ref[...], preferred_element_type=F32, precision=lax.Precision.HIGHEST) + rb_ref[...]

    lane = lax.broadcasted_iota(jnp.int32, logits.shape, 1).astype(F32)
    work = logits
    vals, idxs, sels = [], [], []
    for _ in range(TOP_K):
        m = jnp.max(work, axis=-1, keepdims=True)
        idx = jnp.min(jnp.where(work == m, lane, float(N_EXPERTS)), axis=-1, keepdims=True)
        sel = lane == idx
        vals.append(m)
        idxs.append(idx)
        sels.append(sel)
        work = jnp.where(sel, -jnp.inf, work)
    exps = [jnp.exp(v - vals[0]) for v in vals]
    den = exps[0] + exps[1] + exps[2] + exps[3]

    onehot = jnp.zeros(logits.shape, F32)
    for sel in sels:
        onehot = onehot + jnp.where(sel, 1.0, 0.0)
    ti = lax.broadcasted_iota(jnp.int32, (tb, tb), 0)
    tj = lax.broadcasted_iota(jnp.int32, (tb, tb), 1)
    below = jnp.where(tj < ti, 1.0, 0.0).astype(BF16)
    before = jnp.dot(below, onehot.astype(BF16), preferred_element_type=F32) + carry_ref[...]
    carry_ref[...] = carry_ref[...] + jnp.sum(onehot, axis=0, keepdims=True)
    cnt_ref[...] = carry_ref[...]

    out_lane = lax.broadcasted_iota(jnp.int32, (tb, LANES), 1)
    idx_out = jnp.zeros((tb, LANES), jnp.int32)
    rank_out = jnp.zeros((tb, LANES), jnp.int32)
    gate_out = jnp.zeros((tb, LANES), F32)
    for j in range(TOP_K):
        rank = jnp.sum(jnp.where(sels[j], before, 0.0), axis=-1, keepdims=True)
        idx_out = jnp.where(out_lane == j, idxs[j].astype(jnp.int32), idx_out)
        rank_out = jnp.where(out_lane == j, rank.astype(jnp.int32), rank_out)
        gate_out = jnp.where(out_lane == j, exps[j] / den, gate_out)
    idx_ref[...] = idx_out
    rank_ref[...] = rank_out
    gate_ref[...] = gate_out


def _route(x1, g, sc, sh, rw, rb, seq):
    t, d = x1.shape
    tb = min(ROUTE_TB, seq)
    per_b = seq // tb
    e = rw.shape[1]
    wide = lambda dt: jax.ShapeDtypeStruct((t, LANES), dt)
    return pl.pallas_call(
        _route_kernel,
        grid=(t // tb,),
        in_specs=[pl.BlockSpec((tb, d), lambda i: (i, 0)),
                  pl.BlockSpec((1, d), lambda i: (0, 0)),
                  pl.BlockSpec((1, 1, d), lambda i: (i // per_b, 0, 0)),
                  pl.BlockSpec((1, 1, d), lambda i: (i // per_b, 0, 0)),
                  pl.BlockSpec((d, e), lambda i: (0, 0)),
                  pl.BlockSpec((1, e), lambda i: (0, 0))],
        out_specs=[pl.BlockSpec((tb, d), lambda i: (i, 0)),
                   pl.BlockSpec((tb, LANES), lambda i: (i, 0)),
                   pl.BlockSpec((tb, LANES), lambda i: (i, 0)),
                   pl.BlockSpec((tb, LANES), lambda i: (i, 0)),
                   pl.BlockSpec((1, e), lambda i: (0, 0))],
        out_shape=[jax.ShapeDtypeStruct((t, d), F32), wide(jnp.int32), wide(F32), wide(jnp.int32),
                   jax.ShapeDtypeStruct((1, e), F32)],
        scratch_shapes=[pltpu.VMEM((1, e), F32)],
        compiler_params=_cp(("arbitrary",)),
        name="route",
    )(x1, g.reshape(1, d), sc, sh, rw, rb.reshape(1, e))


def _scatter_kernel(dest_ref, zflag_ref, h_ref, xs_ref, pk_ref, zbuf, sem, zsem):
    tb = h_ref.shape[0]
    half_d = h_ref.shape[1] // 2
    h = h_ref[...]
    pk_ref[...] = _pack_bf16_pair(h[:, :half_d], h[:, half_d:])
    zrows = zbuf.shape[0]
    base = pl.program_id(0) * tb

    @pl.when(pl.program_id(0) == 0)
    def _():
        zbuf[...] = jnp.zeros_like(zbuf)

        def zcopy(b):
            return pltpu.make_async_copy(zbuf, xs_ref.at[pl.ds(pl.multiple_of(b * zrows, zrows), zrows), :], zsem)

        def zstart(b, carry):
            @pl.when(zflag_ref[b] != 0)
            def _():
                zcopy(b).start()
            return carry

        def zwait(b, carry):
            @pl.when(zflag_ref[b] != 0)
            def _():
                zcopy(b).wait()
            return carry

        n_blk = xs_ref.shape[0] // zrows
        lax.fori_loop(0, n_blk, zstart, 0)
        lax.fori_loop(0, n_blk, zwait, 0)

    def copy(t, j):
        d = dest_ref[(base + t) * TOP_K + j]
        return pltpu.make_async_copy(pk_ref.at[pl.ds(t, 1), :], xs_ref.at[pl.ds(d, 1), :], sem)

    def start(t, carry):
        for j in range(TOP_K):
            copy(t, j).start()
        return carry

    lax.fori_loop(0, tb, start, 0, unroll=ROW_DMA_UNROLL)
    for j in range(TOP_K):
        pltpu.make_async_copy(pk_ref, xs_ref.at[pl.ds(0, tb), :], sem).wait()


def _scatter_rows(dest_flat, zero_flag, h2, n_rows):
    t, d = h2.shape
    tb = min(SCATTER_TB, t)
    return pl.pallas_call(
        _scatter_kernel,
        grid_spec=pltpu.PrefetchScalarGridSpec(
            num_scalar_prefetch=2,
            grid=(t // tb,),
            in_specs=[pl.BlockSpec((tb, d), lambda i, dest, zf: (i, 0))],
            out_specs=pl.BlockSpec(memory_space=pl.ANY),
            scratch_shapes=[pltpu.VMEM((tb, d // 2), jnp.uint32), pltpu.VMEM((MOE_TM, d // 2), jnp.uint32),
                            pltpu.SemaphoreType.DMA(()), pltpu.SemaphoreType.DMA(())],
        ),
        out_shape=jax.ShapeDtypeStruct((n_rows, d // 2), jnp.uint32),
        compiler_params=_cp(("arbitrary",)),
        name="scatter_rows",
    )(dest_flat, zero_flag, h2)


def _expert_kernel(be_ref, bs_ref, nu_ref, rows_ref, x_ref, bgu_ref, bd_ref, wgu_hbm, wd_hbm, o_ref,
                   xb_ref, act_ref, wbuf, sem):
    del bs_ref
    i = pl.program_id(0)
    n = pl.program_id(1)
    n_steps = pl.num_programs(1)
    nt, _, tn = act_ref.shape
    n_slot, de, tnb = wbuf.shape
    look = n_slot - 1
    g = i * n_steps + n
    n_live = nu_ref[0] * n_steps
    live = i < nu_ref[0]

    def up_copies(blk, step, slot):
        e = be_ref[blk]
        col = pl.multiple_of(step * tn, tn)
        glu = pltpu.make_async_copy(wgu_hbm.at[e, :, pl.ds(col, tn)], wbuf.at[slot, :, 0:tn], sem.at[slot])
        lin = pltpu.make_async_copy(wgu_hbm.at[e, :, pl.ds(pl.multiple_of(de + col, tn), tn)],
                                    wbuf.at[slot, :, tn:2 * tn], sem.at[slot])
        return glu, lin

    def down_copy(blk, step, slot):
        colb = pl.multiple_of((step - nt) * tnb, tnb)
        return pltpu.make_async_copy(wd_hbm.at[be_ref[blk], :, pl.ds(colb, tnb)], wbuf.at[slot], sem.at[slot])

    def start(gs):
        blk = gs // n_steps
        step = gs - blk * n_steps
        slot = gs % n_slot

        @pl.when((gs < n_live) & (step < nt))
        def _():
            glu, lin = up_copies(blk, step, slot)
            glu.start()
            lin.start()

        @pl.when((gs < n_live) & (step >= nt))
        def _():
            down_copy(blk, step, slot).start()

    @pl.when(g == 0)
    def _():
        for gs in range(look):
            start(jnp.int32(gs))

    start(g + look)
    slot = g % n_slot

    tm, half_d = x_ref.shape
    sub = tm // 2
    hw = tnb // 2
    whole = rows_ref[i] > sub

    def both_sizes(run):
        @pl.when(whole)
        def _():
            run(tm)

        @pl.when(jnp.logical_not(whole))
        def _():
            run(sub)

    @pl.when(live & (n == 0))
    def _():
        lo, hi = _unpack_bf16_pair(x_ref[...])
        xb_ref[:, 0:half_d] = lo.astype(BF16)
        xb_ref[:, half_d:2 * half_d] = hi.astype(BF16)

    @pl.when(live & (n < nt))
    def _():
        glu_cp, lin_cp = up_copies(i, n, slot)
        glu_cp.wait()
        lin_cp.wait()

        def run(rows):
            xb = xb_ref[0:rows]
            w = wbuf[slot]
            glu = jnp.dot(xb, w[:, 0:tn].astype(BF16), preferred_element_type=F32) + bgu_ref[0, n]
            lin = jnp.dot(xb, w[:, tn:2 * tn].astype(BF16), preferred_element_type=F32) + bgu_ref[0, n + nt]
            glu = jnp.minimum(glu, SWIGLU_LIMIT)
            lin = jnp.clip(lin, -SWIGLU_LIMIT, SWIGLU_LIMIT)
            act_ref[n, 0:rows] = (glu * _sigmoid(SWIGLU_ALPHA * glu) * (lin + 1.0)).astype(BF16)

        both_sizes(run)

    for m in range(2 * half_d // tnb):
        @pl.when(live & (n == nt + m))
        def _():
            down_copy(i, n, slot).wait()

            def run(rows):
                wd = wbuf[slot]
                y = jnp.dot(act_ref[0, 0:rows], wd[0:tn].astype(BF16), preferred_element_type=F32)
                for k in range(1, nt):
                    y = y + jnp.dot(act_ref[k, 0:rows], wd[k * tn:(k + 1) * tn].astype(BF16),
                                    preferred_element_type=F32)
                y = y + bd_ref[0, m]
                o_ref[0:rows, m * hw:(m + 1) * hw] = _pack_bf16_pair(y[:, :hw], y[:, hw:])
                if rows < tm:
                    o_ref[rows:tm, m * hw:(m + 1) * hw] = jnp.zeros((tm - rows, hw), o_ref.dtype)

            both_sizes(run)

    @pl.when(jnp.logical_not(live) & (n == 0))
    def _():
        o_ref[...] = jnp.zeros_like(o_ref)


def _experts(blk_expert, blk_src, n_used, blk_rows, xs, w_gu, b_gu, w_down, b_down):
    n_rows, half_d = xs.shape
    d = 2 * half_d
    tm, tn, tnb = MOE_TM, MOE_TN, MOE_TNB
    nb = n_rows // tm
    e, _, two_de = w_gu.shape
    de = two_de // 2
    nt = de // tn
    ntb = d // tnb
    assert 2 * tn == tnb and de == d
    b_gu4 = b_gu.reshape(e, 2 * nt, 1, tn)
    b_d4 = b_down.reshape(e, ntb, 1, tnb)
    return pl.pallas_call(
        _expert_kernel,
        grid_spec=pltpu.PrefetchScalarGridSpec(
            num_scalar_prefetch=4,
            grid=(nb, nt + ntb),
            in_specs=[pl.BlockSpec((tm, half_d), lambda i, n, be, bs, nu, br: (bs[i], 0)),
                      pl.BlockSpec((1, 2 * nt, 1, tn), lambda i, n, be, bs, nu, br: (be[i], 0, 0, 0)),
                      pl.BlockSpec((1, ntb, 1, tnb), lambda i, n, be, bs, nu, br: (be[i], 0, 0, 0)),
                      pl.BlockSpec(memory_space=pl.ANY),
                      pl.BlockSpec(memory_space=pl.ANY)],
            out_specs=pl.BlockSpec((tm, half_d), lambda i, n, be, bs, nu, br: (i, 0)),
            scratch_shapes=[pltpu.VMEM((tm, d), BF16), pltpu.VMEM((nt, tm, tn), BF16),
                            pltpu.VMEM((MOE_RING, de, tnb), F32), pltpu.SemaphoreType.DMA((MOE_RING,))],
        ),
        out_shape=jax.ShapeDtypeStruct((n_rows, half_d), jnp.uint32),
        compiler_params=_cp(("arbitrary", "arbitrary")),
        name="experts",
    )(blk_expert, blk_src, n_used, blk_rows, xs, b_gu4, b_d4, w_gu, w_down)


def _combine_kernel(dest_ref, ys_ref, gate_ref, x_ref, g2_ref, fg_ref, o_ref, buf_ref, sem):
    tb = x_ref.shape[0]
    i = pl.program_id(0)
    half = i % 2

    def gather(blk, slot):
        def start(t, carry):
            for j in range(TOP_K):
                d = dest_ref[(blk * tb + t) * TOP_K + j]
                pltpu.make_async_copy(ys_ref.at[pl.ds(d, 1), :], buf_ref.at[slot, j, pl.ds(t, 1), :],
                                      sem.at[slot]).start()
            return carry
        lax.fori_loop(0, tb, start, 0, unroll=ROW_DMA_UNROLL)

    @pl.when(i == 0)
    def _():
        gather(i, half)

    @pl.when(i + 1 < pl.num_programs(0))
    def _():
        gather(i + 1, 1 - half)

    for j in range(TOP_K):
        pltpu.make_async_copy(ys_ref.at[pl.ds(0, tb), :], buf_ref.at[half, j], sem.at[half]).wait()

    gate = gate_ref[...]
    hw = MOE_TNB // 2
    pieces = []
    for m in range(x_ref.shape[1] // MOE_TNB):
        lo_acc = hi_acc = None
        for j in range(TOP_K):
            lo, hi = _unpack_bf16_pair(buf_ref[half, j, :, m * hw:(m + 1) * hw])
            gj = gate[:, j:j + 1]
            lo_acc = gj * lo if lo_acc is None else lo_acc + gj * lo
            hi_acc = gj * hi if hi_acc is None else hi_acc + gj * hi
        pieces += [lo_acc, hi_acc]
    ffn = jnp.concatenate(pieces, axis=1)
    x2 = x_ref[...] + g2_ref[0] * ffn
    o_ref[...] = _rms(x2, NORM_EPS) * fg_ref[...]


def _combine(dest_flat, ys, gates, x1, g2, fg, seq):
    t, d = x1.shape
    tb = min(COMBINE_TB, seq)
    per_b = seq // tb
    return pl.pallas_call(
        _combine_kernel,
        grid_spec=pltpu.PrefetchScalarGridSpec(
            num_scalar_prefetch=1,
            grid=(t // tb,),
            in_specs=[pl.BlockSpec(memory_space=pl.ANY),
                      pl.BlockSpec((tb, LANES), lambda i, dest: (i, 0)),
                      pl.BlockSpec((tb, d), lambda i, dest: (i, 0)),
                      pl.BlockSpec((1, 1, d), lambda i, dest: (i // per_b, 0, 0)),
                      pl.BlockSpec((1, d), lambda i, dest: (0, 0))],
            out_specs=pl.BlockSpec((tb, d), lambda i, dest: (i, 0)),
            scratch_shapes=[pltpu.VMEM((2, TOP_K, tb, d // 2), jnp.uint32), pltpu.SemaphoreType.DMA((2,))],
        ),
        out_shape=jax.ShapeDtypeStruct((t, d), F32),
        compiler_params=_cp(("arbitrary",)),
        name="combine",
    )(dest_flat, ys, gates, x1, g2, fg.reshape(1, d))


def _moe(x1, g, sc2, sh2, g2, router_w, router_b, w_gu, b_gu, w_down, b_down, final_g, seq):
    t, d = x1.shape
    a = t * TOP_K
    tm = MOE_TM
    nb = -(-a // tm) + N_EXPERTS
    h2, idx_w, gate_w, rank_w, counts = _route(x1, g, sc2, sh2, router_w, router_b, seq)
    idx = idx_w[:, :TOP_K]
    rank = rank_w[:, :TOP_K]
    cnt = counts.reshape(-1).astype(jnp.int32)
    padded = ((cnt + tm - 1) // tm) * tm
    pad_end = jnp.cumsum(padded)
    pad_start = pad_end - padded
    experts = jnp.arange(N_EXPERTS, dtype=jnp.int32)
    start_of = jnp.sum(jnp.where(idx[:, :, None] == experts, pad_start, 0), axis=-1)
    dest = (start_of + rank).reshape(a).astype(jnp.int32)
    n_used = (pad_end[-1] // tm).astype(jnp.int32)
    blk = jnp.arange(nb, dtype=jnp.int32)
    blk_src = jnp.clip(blk, 0, jnp.maximum(n_used - 1, 0))
    blk_expert = jnp.sum((pad_end[None, :] <= (blk_src * tm)[:, None]).astype(jnp.int32), axis=1)
    blk_expert = jnp.minimum(blk_expert, N_EXPERTS - 1)
    last_blk = jnp.where(padded > 0, pad_end // tm - 1, -1)
    zero_flag = ((blk >= n_used) | jnp.any(last_blk[None, :] == blk[:, None], axis=1)).astype(jnp.int32)
    own = blk_expert[:, None] == experts
    rows_before = blk_src * tm - jnp.sum(jnp.where(own, pad_start, 0), axis=1)
    blk_rows = jnp.clip(jnp.sum(jnp.where(own, cnt, 0), axis=1) - rows_before, 0, tm).astype(jnp.int32)
    xs = _scatter_rows(dest, zero_flag, h2, nb * tm)
    ys = _experts(blk_expert, blk_src, n_used.reshape(1), blk_rows, xs, w_gu, b_gu, w_down, b_down)
    return _combine(dest, ys, gate_w, x1, g2, final_g, seq)


def kernel(x, c, ada_w, ada_b, norm1_g, norm2_g, w_in, rwkv_mu, rwkv_w0, rwkv_w2, rwkv_a0, rwkv_a2, rwkv_g2,
           rwkv_k_k, rwkv_k_a, rwkv_r_k, rwkv_gn_g, rwkv_gn_b, hgrn_lb_logits, hgrn_gn_g, proj_a, proj_b, w_out,
           router_w, router_b, exp_w_gate_up, exp_b_gate_up, exp_w_down, exp_b_down, final_norm_g):
    bsz, seq, d = x.shape
    depth = ada_w.shape[0]
    lower_bounds = jnp.cumsum(jax.nn.softmax(hgrn_lb_logits.astype(F32), axis=0), axis=0)
    xf = x.reshape(bsz * seq, d)
    rkv = 3 * RWKV_WIDTH
    rwkv_cols = rkv + LORA_COLS
    hg_gate = 4 * HGRN_WIDTH + 2 * d
    for l in range(depth):
        mod = _adaln(c, ada_w[l], ada_b[l])
        sh1, sc1, g1, sh2, sc2, g2 = [m.reshape(bsz, 1, d) for m in jnp.split(mod, 6, axis=-1)]

        wl = w_in[l]
        w_cat = jnp.concatenate([wl[:, :rkv], wl[:, rwkv_cols:rwkv_cols + hg_gate], wl[:, rkv:rwkv_cols],
                                 jnp.zeros((d, LORA_PAD - LORA_COLS), wl.dtype)], axis=1).astype(BF16)
        proj = _inproj(xf, norm1_g[l], sc1, sh1, w_cat, seq)

        mu = rwkv_mu[l]
        mu_lo = jnp.pad(mu[rkv:], (0, LORA_PAD - LORA_COLS))
        pad_rows = lambda w, r0: jnp.zeros((LORA_PAD, RWKV_WIDTH), F32).at[r0:r0 + w.shape[0]].set(w).astype(BF16)
        ya = _rwkv(proj, bsz, seq, mu[:rkv], mu_lo, rwkv_w0[l], rwkv_a0[l], rwkv_k_k[l], rwkv_k_a[l],
                   rwkv_r_k[l].reshape(-1), rwkv_gn_g[l], rwkv_gn_b[l],
                   pad_rows(rwkv_w2[l], 0), pad_rows(rwkv_a2[l], DECAY_LORA),
                   pad_rows(rwkv_g2[l], DECAY_LORA + AAA_LORA))
        yb = _hgrn(proj, bsz, seq, lower_bounds[l], hgrn_gn_g[l])
        x1 = _mixout(ya, yb, proj, proj_a[l].astype(BF16), proj_b[l].astype(BF16), w_out[l].astype(BF16), xf, g1, seq)
        assert depth == 1
        out = _moe(x1, norm2_g[l], sc2, sh2, g2, router_w[l], router_b[l], exp_w_gate_up[l], exp_b_gate_up[l],
                   exp_w_down[l], exp_b_down[l], final_norm_g, seq)
    return out.reshape(bsz, seq, d)
```

```python
import jax
import jax.numpy as jnp
from jax import lax
from jax.experimental import pallas as pl
from jax.experimental.pallas import tpu as pltpu

F32 = jnp.float32
BF16 = jnp.bfloat16

D_MODEL = 2048
RWKV_WIDTH = 1024
RWKV_HEAD = 64
DECAY_LORA = 64
AAA_LORA = 64
GATE_LORA = 160
LORA_COLS = DECAY_LORA + AAA_LORA + GATE_LORA
LORA_PAD = 512
RWKV_GN_EPS = 64e-5
HGRN_WIDTH = 1024
HGRN_HEAD = 128
GLA_BLOCK = 16
N_EXPERTS = 32
TOP_K = 4
D_EXPERT = 2048
SWIGLU_ALPHA = 1.702
SWIGLU_LIMIT = 7.0
NORM_EPS = 1e-5

LANES = 128
VMEM_LIMIT = 56 * 1024 * 1024

RWKV_CHUNK = 64
HGRN_STEP = 128
PROJ_TM = 1024
PROJ_TN = 512
NORM_ROWS = 128
MIX_TM = 256
MOE_TM = 512
MOE_TN = 512
MOE_TNB = 1024
MOE_RING = 3
ROUTE_TB = 256
SCATTER_TB = 256
COMBINE_TB = 128
ROW_DMA_UNROLL = 8


def _cp(sem, vmem=VMEM_LIMIT):
    return pltpu.CompilerParams(dimension_semantics=sem, vmem_limit_bytes=vmem)


def _mm(a, b):
    return jnp.dot(a.astype(BF16), b.astype(BF16), preferred_element_type=F32)


def _mm_nt(a, b):
    return lax.dot_general(a.astype(BF16), b.astype(BF16), (((1,), (1,)), ((), ())),
                           preferred_element_type=F32)


def _mm_tn(a, b):
    return lax.dot_general(a.astype(BF16), b.astype(BF16), (((0,), (0,)), ((), ())),
                           preferred_element_type=F32)


def _split3(x):
    hi = x.astype(BF16)
    r1 = x - hi.astype(F32)
    mid = r1.astype(BF16)
    lo = (r1 - mid.astype(F32)).astype(BF16)
    return hi, mid, lo


def _mm_exact_rhs(m_bf16, x):
    hi, mid, lo = _split3(x)
    d = lambda p: jnp.dot(m_bf16, p, preferred_element_type=F32)
    return d(hi) + d(mid) + d(lo)


def _sigmoid(x):
    return 1.0 / (1.0 + jnp.exp(-x))


def _softplus(x):
    return jnp.maximum(x, 0.0) + jnp.log(1.0 + jnp.exp(-jnp.abs(x)))


def _rms(x, eps):
    return x * lax.rsqrt(jnp.mean(x * x, axis=-1, keepdims=True) + eps)


def _adaln_kernel(c_ref, w_ref, b_ref, o_ref):
    c = c_ref[...]
    c_act = c * _sigmoid(c)
    o_ref[...] = jnp.dot(c_act, w_ref[...], preferred_element_type=F32,
                         precision=lax.Precision.HIGHEST) + b_ref[...]


def _adaln(c, w, b):
    bsz, d = c.shape
    n = w.shape[1]
    tn = 1024
    return pl.pallas_call(
        _adaln_kernel,
        grid=(n // tn,),
        in_specs=[pl.BlockSpec((bsz, d), lambda j: (0, 0)),
                  pl.BlockSpec((d, tn), lambda j: (0, j)),
                  pl.BlockSpec((1, tn), lambda j: (0, j))],
        out_specs=pl.BlockSpec((bsz, tn), lambda j: (0, j)),
        out_shape=jax.ShapeDtypeStruct((bsz, n), F32),
        compiler_params=_cp(("arbitrary",)),
        name="adaln",
    )(c, w, b.reshape(1, n))


def _inproj_kernel(x_ref, g_ref, sc_ref, sh_ref, w_ref, o_ref, h_ref):
    @pl.when(pl.program_id(1) == 0)
    def _():
        def rows(c, carry):
            rs = pl.ds(pl.multiple_of(c * NORM_ROWS, NORM_ROWS), NORM_ROWS)
            y = _rms(x_ref[rs, :], NORM_EPS) * g_ref[...]
            h_ref[rs, :] = (y * (1.0 + sc_ref[0]) + sh_ref[0]).astype(BF16)
            return carry
        lax.fori_loop(0, x_ref.shape[0] // NORM_ROWS, rows, 0)

    o_ref[...] = jnp.dot(h_ref[...], w_ref[...], preferred_element_type=F32).astype(o_ref.dtype)


def _inproj(xf, g, sc, sh, w, seq):
    t, d = xf.shape
    n = w.shape[1]
    tm = min(PROJ_TM, seq)
    tn = PROJ_TN
    per_b = seq // tm
    return pl.pallas_call(
        _inproj_kernel,
        grid=(t // tm, n // tn),
        in_specs=[pl.BlockSpec((tm, d), lambda i, j: (i, 0)),
                  pl.BlockSpec((1, d), lambda i, j: (0, 0)),
                  pl.BlockSpec((1, 1, d), lambda i, j: (i // per_b, 0, 0)),
                  pl.BlockSpec((1, 1, d), lambda i, j: (i // per_b, 0, 0)),
                  pl.BlockSpec((d, tn), lambda i, j: (0, j))],
        out_specs=pl.BlockSpec((tm, tn), lambda i, j: (i, j)),
        out_shape=jax.ShapeDtypeStruct((t, n), BF16),
        scratch_shapes=[pltpu.VMEM((tm, d), BF16)],
        compiler_params=_cp(("arbitrary", "arbitrary")),
        name="inproj",
    )(xf, g.reshape(1, d), sc, sh, w)


def _rwkv_kernel(r_ref, k_ref, v_ref, lo_ref, mu_ref, mulo_ref, w0_ref, a0_ref, kk_ref, ka_ref, rk_ref,
                 gng_ref, gnb_ref, w2_ref, a2_ref, g2_ref, o_ref,
                 st_ref, cr_ref, ck_ref, cv_ref, clo_ref):
    L = RWKV_CHUNK
    H2 = 2 * L
    n_pair = RWKV_WIDTH // LANES

    @pl.when(pl.program_id(1) == 0)
    def _():
        st_ref[...] = jnp.zeros_like(st_ref)
        cr_ref[...] = jnp.zeros_like(cr_ref)
        ck_ref[...] = jnp.zeros_like(ck_ref)
        cv_ref[...] = jnp.zeros_like(cv_ref)
        clo_ref[...] = jnp.zeros_like(clo_ref)

    def shift(x_ref, mu, carry_ref):
        x = x_ref[...].astype(F32)
        rows = lax.broadcasted_iota(jnp.int32, x.shape, 0)
        prev = jnp.where(rows == 0, carry_ref[...], pltpu.roll(x, 1, axis=0))
        carry_ref[...] = x[L - 1:L, :]
        return x + (prev - x) * mu

    mu = mu_ref[...]
    r = shift(r_ref, mu[:, 0:RWKV_WIDTH], cr_ref)
    k = shift(k_ref, mu[:, RWKV_WIDTH:2 * RWKV_WIDTH], ck_ref)
    v = shift(v_ref, mu[:, 2 * RWKV_WIDTH:3 * RWKV_WIDTH], cv_ref)
    lo = shift(lo_ref, mulo_ref[...], clo_ref)

    wpre = w0_ref[...] + _mm(jnp.tanh(lo), w2_ref[...])
    lw = -jnp.exp(-_softplus(-wpre) - 0.5)
    a = _sigmoid(a0_ref[...] + _mm(lo, a2_ref[...]))
    g = _mm(_sigmoid(lo), g2_ref[...])

    ti = lax.broadcasted_iota(jnp.int32, (L, L), 0)
    tj = lax.broadcasted_iota(jnp.int32, (L, L), 1)
    tri = jnp.where(tj <= ti, 1.0, 0.0).astype(BF16)
    cl = _mm_exact_rhs(tri, lw)
    cl_last = cl[L - 1:L, :]
    e_pos = jnp.exp(cl)
    e_neg = jnp.exp(-cl)
    e_prev = jnp.exp(cl - lw)
    e_end = jnp.exp(cl_last - cl)
    p_last = jnp.exp(cl_last)

    kk = k * kk_ref[...]
    k2 = k * (1.0 + (a - 1.0) * ka_ref[...])
    f_b = a * e_neg
    f_bend = a * e_end
    rk = rk_ref[...]
    gng = gng_ref[...]
    gnb = gnb_ref[...]

    lane = lax.broadcasted_iota(jnp.int32, (L, LANES), 1)
    lo_half = lane < RWKV_HEAD
    si = lax.broadcasted_iota(jnp.int32, (H2, H2), 0)
    sj = lax.broadcasted_iota(jnp.int32, (H2, H2), 1)
    same = (si // L) == (sj // L)
    m_strict = same & (sj < si)
    m_incl = same & (sj <= si)
    eye = si == sj
    own = (si // L) == (sj // RWKV_HEAD)

    def stack(x):
        return jnp.concatenate([jnp.where(lo_half, x, 0.0), jnp.where(lo_half, 0.0, x)], axis=0)

    def rep(x):
        return jnp.concatenate([x, x], axis=0)

    pairs = range(n_pair)
    sls = [slice(p * LANES, (p + 1) * LANES) for p in pairs]
    at, bt, bend, kt, kend, rt, v_s, bonus = [], [], [], [], [], [], [], []
    for sl in sls:
        kk_s = stack(kk[:, sl])
        nrm = jnp.sqrt(jnp.sum(kk_s * kk_s, axis=-1, keepdims=True))
        kkn = kk_s / jnp.maximum(nrm, 1e-12)
        at.append(-kkn * rep(e_prev[:, sl]))
        bt.append(kkn * rep(f_b[:, sl]))
        bend.append(kkn * rep(f_bend[:, sl]))
        k2_s = stack(k2[:, sl])
        kt.append(k2_s * rep(e_neg[:, sl]))
        kend.append(k2_s * rep(e_end[:, sl]))
        r_s = stack(r[:, sl])
        rt.append(r_s * rep(e_pos[:, sl]))
        vs = stack(v[:, sl])
        v_s.append(vs)
        bonus.append(jnp.sum(r_s * k2_s * rk[:, sl], axis=-1, keepdims=True) * vs)

    gm = [_mm_nt(jnp.concatenate([at[p], rt[p]], axis=0), jnp.concatenate([bt[p], kt[p]], axis=0))
          for p in pairs]
    a_ab = [jnp.where(m_strict, gm[p][:H2, :H2], 0.0) for p in pairs]
    a_ak = [jnp.where(m_strict, gm[p][:H2, H2:], 0.0) for p in pairs]
    a_rb = [jnp.where(m_incl, gm[p][H2:, :H2], 0.0) for p in pairs]
    a_rk = [jnp.where(m_incl, gm[p][H2:, H2:], 0.0) for p in pairs]

    ident = jnp.where(eye, 1.0, 0.0)
    tinv = [ident + a_ab[p] for p in pairs]
    pw = a_ab
    akv = [_mm(a_ak[p], v_s[p]) for p in pairs]
    ark_v = [_mm(a_rk[p], v_s[p]) for p in pairs]
    kend_v = [_mm_tn(kend[p], v_s[p]) for p in pairs]
    for _ in range(max(1, (L - 1).bit_length() - 1)):
        pw = [_mm(pw[p], pw[p]) for p in pairs]
        tinv = [tinv[p] + _mm(tinv[p], pw[p]) for p in pairs]

    tx = [_mm(tinv[p], jnp.concatenate([at[p], akv[p]], axis=1)) for p in pairs]
    y2 = [_mm(a_rb[p], tx[p]) for p in pairs]
    btx = [_mm_tn(bend[p], tx[p]) for p in pairs]
    s0 = [st_ref[p] for p in pairs]
    y = [_mm(rt[p] + y2[p][:, :LANES], s0[p]) + (y2[p][:, LANES:] + ark_v[p]) for p in pairs]
    for p in pairs:
        m_mat = jnp.where(eye, jnp.broadcast_to(p_last[:, sls[p]], (H2, LANES)), 0.0) + btx[p][:, :LANES]
        st_ref[p] = _mm(m_mat, s0[p]) + (btx[p][:, LANES:] + kend_v[p])

    for p in pairs:
        sl = sls[p]
        mean = jnp.sum(y[p], axis=-1, keepdims=True) * (1.0 / RWKV_HEAD)
        dlt = jnp.where(own, y[p] - mean, 0.0)
        var = jnp.sum(dlt * dlt, axis=-1, keepdims=True) * (1.0 / RWKV_HEAD)
        yn = dlt * lax.rsqrt(var + RWKV_GN_EPS) * gng[:, sl] + jnp.where(own, gnb[:, sl], 0.0)
        ys = yn + bonus[p]
        o_ref[:, sl] = ((ys[:L] + ys[L:]) * g[:, sl]).astype(o_ref.dtype)


def _rwkv(proj, bsz, seq, mu_rkv, mu_lo, w0, a0, k_k, k_a, r_k, gn_g, gn_b, w2p, a2p, g2p):
    L = RWKV_CHUNK
    nc = seq // L
    W = RWKV_WIDTH
    row = lambda n: pl.BlockSpec((1, n), lambda b, c: (0, 0))
    full = lambda a: pl.BlockSpec(a.shape, lambda b, c: (0, 0))
    lo_blk = proj.shape[1] // LORA_PAD - 1
    v1 = lambda a: a.reshape(1, -1)
    return pl.pallas_call(
        _rwkv_kernel,
        grid=(bsz, nc),
        in_specs=[pl.BlockSpec((L, W), lambda b, c: (b * nc + c, 0)),
                  pl.BlockSpec((L, W), lambda b, c: (b * nc + c, 1)),
                  pl.BlockSpec((L, W), lambda b, c: (b * nc + c, 2)),
                  pl.BlockSpec((L, LORA_PAD), lambda b, c: (b * nc + c, lo_blk)),
                  row(3 * W), row(LORA_PAD), row(W), row(W), row(W), row(W), row(W), row(W), row(W),
                  full(w2p), full(a2p), full(g2p)],
        out_specs=pl.BlockSpec((L, W), lambda b, c: (b * nc + c, 0)),
        out_shape=jax.ShapeDtypeStruct((bsz * seq, W), BF16),
        scratch_shapes=[pltpu.VMEM((W // LANES, LANES, LANES), F32),
                        pltpu.VMEM((1, W), F32), pltpu.VMEM((1, W), F32), pltpu.VMEM((1, W), F32),
                        pltpu.VMEM((1, LORA_PAD), F32)],
        compiler_params=_cp(("arbitrary", "arbitrary")),
        name="rwkv7",
    )(proj, proj, proj, proj, v1(mu_rkv), v1(mu_lo), v1(w0), v1(a0), v1(k_k), v1(k_a), v1(r_k), v1(gn_g),
      v1(gn_b), w2p, a2p, g2p)


def _hgrn_kernel(q_ref, f_ref, i_ref, og_ref, lb_ref, gn_ref, o_ref, st_ref):
    n = HGRN_STEP
    blk = GLA_BLOCK
    nb = n // blk
    n_head = HGRN_WIDTH // HGRN_HEAD

    @pl.when(pl.program_id(1) == 0)
    def _():
        st_ref[...] = jnp.zeros_like(st_ref)

    lb = lb_ref[...]
    qf = q_ref[...].astype(F32)
    q = qf * _sigmoid(qf)
    forget = lb + (1.0 - lb) * _sigmoid(f_ref[...].astype(F32))
    log_f = jnp.log(forget)
    kx = 1.0 - forget
    vv = i_ref[...].astype(F32)
    og = og_ref[...].astype(F32)

    ti = lax.broadcasted_iota(jnp.int32, (n, n), 0)
    tj = lax.broadcasted_iota(jnp.int32, (n, n), 1)
    same = (ti // blk) == (tj // blk)
    causal = same & (tj <= ti)
    b = _mm_exact_rhs(jnp.where(causal, 1.0, 0.0).astype(BF16), log_f)
    b_last = _mm_exact_rhs(jnp.where(same, 1.0, 0.0).astype(BF16), log_f)
    q_dec = q * jnp.exp(b)
    k_dec = kx * jnp.exp(-b)
    k_end = kx * jnp.exp(b_last - b)
    d_end = jnp.exp(b_last)
    rows = lax.broadcasted_iota(jnp.int32, (n, HGRN_HEAD), 0)
    gn = gn_ref[...]

    heads = range(n_head)
    sls = [slice(h * HGRN_HEAD, (h + 1) * HGRN_HEAD) for h in heads]
    attn = [jnp.where(causal, _mm_nt(q_dec[:, sl], k_dec[:, sl]), 0.0) for sl in sls]
    o_intra = [_mm(attn[h], vv[:, sls[h]]) for h in heads]
    vt = [vv[:, sl].T for sl in sls]
    blk_mask = [(rows // blk) == c for c in range(nb)]
    kv_all = [_mm(vt[h], jnp.concatenate([jnp.where(blk_mask[c], k_end[:, sls[h]], 0.0) for c in range(nb)],
                                         axis=1)) for h in heads]
    kv_t = [[kv_all[h][:, c * HGRN_HEAD:(c + 1) * HGRN_HEAD] for c in range(nb)] for h in heads]
    st = [[st_ref[h]] for h in heads]
    for h in heads:
        for c in range(nb):
            st[h].append(st[h][c] * d_end[c * blk:c * blk + 1, sls[h]] + kv_t[h][c])
        st_ref[h] = st[h][nb]
    for h in heads:
        sl = sls[h]
        inter = [_mm_nt(q_dec[c * blk:(c + 1) * blk, sl], st[h][c]) for c in range(nb)]
        o = o_intra[h] + jnp.concatenate(inter, axis=0)
        o = _rms(o, NORM_EPS) * gn
        ogh = og[:, sl]
        o_ref[:, sl] = (o * (ogh * _sigmoid(ogh))).astype(o_ref.dtype)


def _hgrn(proj, bsz, seq, lb, gn_g):
    n = HGRN_STEP
    ns = seq // n
    W = HGRN_WIDTH
    col = lambda j: pl.BlockSpec((n, W), lambda b, c: (b * ns + c, j))
    return pl.pallas_call(
        _hgrn_kernel,
        grid=(bsz, ns),
        in_specs=[col(3), col(4), col(5), col(6),
                  pl.BlockSpec((1, W), lambda b, c: (0, 0)),
                  pl.BlockSpec((1, HGRN_HEAD), lambda b, c: (0, 0))],
        out_specs=pl.BlockSpec((n, W), lambda b, c: (b * ns + c, 0)),
        out_shape=jax.ShapeDtypeStruct((bsz * seq, W), BF16),
        scratch_shapes=[pltpu.VMEM((W // HGRN_HEAD, HGRN_HEAD, HGRN_HEAD), F32)],
        compiler_params=_cp(("arbitrary", "arbitrary")),
        name="hgrn2",
    )(proj, proj, proj, proj, lb.reshape(1, W), gn_g.reshape(1, HGRN_HEAD))


def _mixout_kernel(ya_ref, yb_ref, ga0_ref, ga1_ref, gb0_ref, gb1_ref, pa_ref, pb_ref, wo_ref, x_ref, g_ref, o_ref):
    half = ga0_ref.shape[1]
    ya = ya_ref[...]
    yb = yb_ref[...]
    parts = []
    for c, (ga_ref, gb_ref) in enumerate(((ga0_ref, gb0_ref), (ga1_ref, gb1_ref))):
        cs = slice(c * half, (c + 1) * half)
        ma = jnp.dot(ya, pa_ref[:, cs], preferred_element_type=F32)
        mb = jnp.dot(yb, pb_ref[:, cs], preferred_element_type=F32)
        parts.append((_sigmoid(ga_ref[...].astype(F32)) * ma + _sigmoid(gb_ref[...].astype(F32)) * mb).astype(BF16))
    mix = (jnp.dot(parts[0], wo_ref[0:half, :], preferred_element_type=F32)
           + jnp.dot(parts[1], wo_ref[half:2 * half, :], preferred_element_type=F32))
    o_ref[...] = x_ref[...] + g_ref[0] * mix


def _mixout(ya, yb, proj, pa, pb, wo, xf, g1, seq):
    t, w = ya.shape
    d = pa.shape[1]
    tm = min(MIX_TM, seq)
    per_b = seq // tm
    half = d // 2
    ga0 = (3 * RWKV_WIDTH + 4 * HGRN_WIDTH) // half
    gate = lambda j: pl.BlockSpec((tm, half), lambda i: (i, ga0 + j))
    const = lambda a: pl.BlockSpec(a.shape, lambda i: (0, 0), pipeline_mode=pl.Buffered(1))
    return pl.pallas_call(
        _mixout_kernel,
        grid=(t // tm,),
        in_specs=[pl.BlockSpec((tm, w), lambda i: (i, 0)),
                  pl.BlockSpec((tm, w), lambda i: (i, 0)),
                  gate(0), gate(1), gate(2), gate(3),
                  const(pa), const(pb), const(wo),
                  pl.BlockSpec((tm, d), lambda i: (i, 0)),
                  pl.BlockSpec((1, 1, d), lambda i: (i // per_b, 0, 0))],
        out_specs=pl.BlockSpec((tm, d), lambda i: (i, 0)),
        out_shape=jax.ShapeDtypeStruct((t, d), F32),
        compiler_params=_cp(("arbitrary",)),
        name="mixout",
    )(ya, yb, proj, proj, proj, proj, pa, pb, wo, xf, g1)


def _route_kernel(x_ref, g_ref, sc_ref, sh_ref, rwh_ref, rwm_ref, rb_ref, idx_ref, gate_ref, rank_ref, cnt_ref,
                  carry_ref):
    tb = x_ref.shape[0]

    @pl.when(pl.program_id(0) == 0)
    def _():
        carry_ref[...] = jnp.zeros_like(carry_ref)

    h = (_rms(x_ref[...], NORM_EPS) * g_ref[...]) * (1.0 + sc_ref[0]) + sh_ref[0]
    h_hi, h_mid, _ = _split3(h)
    d = lambda a, b: jnp.dot(a, b, preferred_element_type=F32)
    logits = d(h_hi, rwh_ref[...]) + (d(h_hi, rwm_ref[...]) + d(h_mid, rwh_ref[...])) + rb_ref[...]

    lane = lax.broadcasted_iota(jnp.int32, logits.shape, 1).astype(F32)
    work = logits
    vals, idxs, sels = [], [], []
    for _ in range(TOP_K):
        m = jnp.max(work, axis=-1, keepdims=True)
        idx = jnp.min(jnp.where(work == m, lane, float(N_EXPERTS)), axis=-1, keepdims=True)
        sel = lane == idx
        vals.append(m)
        idxs.append(idx)
        sels.append(sel)
        work = jnp.where(sel, -jnp.inf, work)
    exps = [jnp.exp(v - vals[0]) for v in vals]
    den = exps[0] + exps[1] + exps[2] + exps[3]

    onehot = jnp.zeros(logits.shape, F32)
    for sel in sels:
        onehot = onehot + jnp.where(sel, 1.0, 0.0)
    ti = lax.broadcasted_iota(jnp.int32, (tb, tb), 0)
    tj = lax.broadcasted_iota(jnp.int32, (tb, tb), 1)
    below = jnp.where(tj < ti, 1.0, 0.0).astype(BF16)
    before = jnp.dot(below, onehot.astype(BF16), preferred_element_type=F32) + carry_ref[...]
    carry_ref[...] = carry_ref[...] + jnp.sum(onehot, axis=0, keepdims=True)
    cnt_ref[...] = carry_ref[...]

    out_lane = lax.broadcasted_iota(jnp.int32, (tb, LANES), 1)
    idx_out = jnp.zeros((tb, LANES), jnp.int32)
    rank_out = jnp.zeros((tb, LANES), jnp.int32)
    gate_out = jnp.zeros((tb, LANES), F32)
    for j in range(TOP_K):
        rank = jnp.sum(jnp.where(sels[j], before, 0.0), axis=-1, keepdims=True)
        idx_out = jnp.where(out_lane == j, idxs[j].astype(jnp.int32), idx_out)
        rank_out = jnp.where(out_lane == j, rank.astype(jnp.int32), rank_out)
        gate_out = jnp.where(out_lane == j, exps[j] / den, gate_out)
    idx_ref[...] = idx_out
    rank_ref[...] = rank_out
    gate_ref[...] = gate_out


def _route(x1, g, sc, sh, rw, rb, seq):
    t, d = x1.shape
    tb = min(ROUTE_TB, seq)
    per_b = seq // tb
    e = rw.shape[1]
    rw_hi = rw.astype(BF16)
    rw_mid = (rw - rw_hi.astype(F32)).astype(BF16)
    wide = lambda dt: jax.ShapeDtypeStruct((t, LANES), dt)
    return pl.pallas_call(
        _route_kernel,
        grid=(t // tb,),
        in_specs=[pl.BlockSpec((tb, d), lambda i: (i, 0)),
                  pl.BlockSpec((1, d), lambda i: (0, 0)),
                  pl.BlockSpec((1, 1, d), lambda i: (i // per_b, 0, 0)),
                  pl.BlockSpec((1, 1, d), lambda i: (i // per_b, 0, 0)),
                  pl.BlockSpec((d, e), lambda i: (0, 0)),
                  pl.BlockSpec((d, e), lambda i: (0, 0)),
                  pl.BlockSpec((1, e), lambda i: (0, 0))],
        out_specs=[pl.BlockSpec((tb, LANES), lambda i: (i, 0)),
                   pl.BlockSpec((tb, LANES), lambda i: (i, 0)),
                   pl.BlockSpec((tb, LANES), lambda i: (i, 0)),
                   pl.BlockSpec((1, e), lambda i: (0, 0))],
        out_shape=[wide(jnp.int32), wide(F32), wide(jnp.int32), jax.ShapeDtypeStruct((1, e), F32)],
        scratch_shapes=[pltpu.VMEM((1, e), F32)],
        compiler_params=_cp(("arbitrary",)),
        name="route",
    )(x1, g.reshape(1, d), sc, sh, rw_hi, rw_mid, rb.reshape(1, e))


def _scatter_kernel(dest_ref, zflag_ref, x_ref, g_ref, sc_ref, sh_ref, xs_ref, h_ref, zbuf, sem, zsem):
    tb = x_ref.shape[0]
    h_ref[...] = (_rms(x_ref[...], NORM_EPS) * g_ref[...]) * (1.0 + sc_ref[0]) + sh_ref[0]
    zrows = zbuf.shape[0]
    base = pl.program_id(0) * tb

    @pl.when(pl.program_id(0) == 0)
    def _():
        zbuf[...] = jnp.zeros_like(zbuf)

        def zcopy(b):
            return pltpu.make_async_copy(zbuf, xs_ref.at[pl.ds(pl.multiple_of(b * zrows, zrows), zrows), :], zsem)

        def zstart(b, carry):
            @pl.when(zflag_ref[b] != 0)
            def _():
                zcopy(b).start()
            return carry

        def zwait(b, carry):
            @pl.when(zflag_ref[b] != 0)
            def _():
                zcopy(b).wait()
            return carry

        n_blk = xs_ref.shape[0] // zrows
        lax.fori_loop(0, n_blk, zstart, 0)
        lax.fori_loop(0, n_blk, zwait, 0)

    def copy(t, j):
        d = dest_ref[(base + t) * TOP_K + j]
        return pltpu.make_async_copy(h_ref.at[pl.ds(t, 1), :], xs_ref.at[pl.ds(d, 1), :], sem)

    def start(t, carry):
        for j in range(TOP_K):
            copy(t, j).start()
        return carry

    lax.fori_loop(0, tb, start, 0, unroll=ROW_DMA_UNROLL)
    for j in range(TOP_K):
        pltpu.make_async_copy(h_ref, xs_ref.at[pl.ds(0, tb), :], sem).wait()


def _scatter_rows(dest_flat, zero_flag, x1, g, sc, sh, n_rows, seq):
    t, d = x1.shape
    tb = min(SCATTER_TB, seq)
    per_b = seq // tb
    return pl.pallas_call(
        _scatter_kernel,
        grid_spec=pltpu.PrefetchScalarGridSpec(
            num_scalar_prefetch=2,
            grid=(t // tb,),
            in_specs=[pl.BlockSpec((tb, d), lambda i, dest, zf: (i, 0)),
                      pl.BlockSpec((1, d), lambda i, dest, zf: (0, 0)),
                      pl.BlockSpec((1, 1, d), lambda i, dest, zf: (i // per_b, 0, 0)),
                      pl.BlockSpec((1, 1, d), lambda i, dest, zf: (i // per_b, 0, 0))],
            out_specs=pl.BlockSpec(memory_space=pl.ANY),
            scratch_shapes=[pltpu.VMEM((tb, d), F32), pltpu.VMEM((MOE_TM, d), F32), pltpu.SemaphoreType.DMA(()),
                            pltpu.SemaphoreType.DMA(())],
        ),
        out_shape=jax.ShapeDtypeStruct((n_rows, d), F32),
        compiler_params=_cp(("arbitrary",)),
        name="scatter_rows",
    )(dest_flat, zero_flag, x1, g.reshape(1, d), sc, sh)


def _expert_kernel(be_ref, bs_ref, nu_ref, x_ref, bgu_ref, bd_ref, wgu_hbm, wd_hbm, o_ref,
                   xb_ref, act_ref, wbuf, sem):
    del bs_ref
    i = pl.program_id(0)
    n = pl.program_id(1)
    n_steps = pl.num_programs(1)
    nt, _, tn = act_ref.shape
    n_slot, de, tnb = wbuf.shape
    look = n_slot - 1
    g = i * n_steps + n
    n_live = nu_ref[0] * n_steps
    live = i < nu_ref[0]

    def up_copies(blk, step, slot):
        e = be_ref[blk]
        col = pl.multiple_of(step * tn, tn)
        glu = pltpu.make_async_copy(wgu_hbm.at[e, :, pl.ds(col, tn)], wbuf.at[slot, :, 0:tn], sem.at[slot])
        lin = pltpu.make_async_copy(wgu_hbm.at[e, :, pl.ds(pl.multiple_of(de + col, tn), tn)],
                                    wbuf.at[slot, :, tn:2 * tn], sem.at[slot])
        return glu, lin

    def down_copy(blk, step, slot):
        colb = pl.multiple_of((step - nt) * tnb, tnb)
        return pltpu.make_async_copy(wd_hbm.at[be_ref[blk], :, pl.ds(colb, tnb)], wbuf.at[slot], sem.at[slot])

    def start(gs):
        blk = gs // n_steps
        step = gs - blk * n_steps
        slot = gs % n_slot

        @pl.when((gs < n_live) & (step < nt))
        def _():
            glu, lin = up_copies(blk, step, slot)
            glu.start()
            lin.start()

        @pl.when((gs < n_live) & (step >= nt))
        def _():
            down_copy(blk, step, slot).start()

    @pl.when(g == 0)
    def _():
        for gs in range(look):
            start(jnp.int32(gs))

    start(g + look)
    slot = g % n_slot

    @pl.when(live & (n == 0))
    def _():
        xb_ref[...] = x_ref[...].astype(BF16)

    @pl.when(live & (n < nt))
    def _():
        glu_cp, lin_cp = up_copies(i, n, slot)
        glu_cp.wait()
        lin_cp.wait()
        xb = xb_ref[...]
        w = wbuf[slot]
        glu = jnp.dot(xb, w[:, 0:tn].astype(BF16), preferred_element_type=F32) + bgu_ref[0, n]
        lin = jnp.dot(xb, w[:, tn:2 * tn].astype(BF16), preferred_element_type=F32) + bgu_ref[0, n + nt]
        glu = jnp.minimum(glu, SWIGLU_LIMIT)
        lin = jnp.clip(lin, -SWIGLU_LIMIT, SWIGLU_LIMIT)
        act_ref[n] = (glu * _sigmoid(SWIGLU_ALPHA * glu) * (lin + 1.0)).astype(BF16)

    for m in range(o_ref.shape[1] // tnb):
        @pl.when(live & (n == nt + m))
        def _():
            down_copy(i, n, slot).wait()
            wd = wbuf[slot]
            y = jnp.dot(act_ref[0], wd[0:tn].astype(BF16), preferred_element_type=F32)
            for k in range(1, nt):
                y = y + jnp.dot(act_ref[k], wd[k * tn:(k + 1) * tn].astype(BF16), preferred_element_type=F32)
            o_ref[:, m * tnb:(m + 1) * tnb] = y + bd_ref[0, m]

    @pl.when(jnp.logical_not(live) & (n == 0))
    def _():
        o_ref[...] = jnp.zeros_like(o_ref)


def _experts(blk_expert, blk_src, n_used, xs, w_gu, b_gu, w_down, b_down):
    n_rows, d = xs.shape
    tm, tn, tnb = MOE_TM, MOE_TN, MOE_TNB
    nb = n_rows // tm
    e, _, two_de = w_gu.shape
    de = two_de // 2
    nt = de // tn
    ntb = d // tnb
    assert 2 * tn == tnb and de == d
    b_gu4 = b_gu.reshape(e, 2 * nt, 1, tn)
    b_d4 = b_down.reshape(e, ntb, 1, tnb)
    return pl.pallas_call(
        _expert_kernel,
        grid_spec=pltpu.PrefetchScalarGridSpec(
            num_scalar_prefetch=3,
            grid=(nb, nt + ntb),
            in_specs=[pl.BlockSpec((tm, d), lambda i, n, be, bs, nu: (bs[i], 0)),
                      pl.BlockSpec((1, 2 * nt, 1, tn), lambda i, n, be, bs, nu: (be[i], 0, 0, 0)),
                      pl.BlockSpec((1, ntb, 1, tnb), lambda i, n, be, bs, nu: (be[i], 0, 0, 0)),
                      pl.BlockSpec(memory_space=pl.ANY),
                      pl.BlockSpec(memory_space=pl.ANY)],
            out_specs=pl.BlockSpec((tm, d), lambda i, n, be, bs, nu: (i, 0)),
            scratch_shapes=[pltpu.VMEM((tm, d), BF16), pltpu.VMEM((nt, tm, tn), BF16),
                            pltpu.VMEM((MOE_RING, de, tnb), F32), pltpu.SemaphoreType.DMA((MOE_RING,))],
        ),
        out_shape=jax.ShapeDtypeStruct((n_rows, d), F32),
        compiler_params=_cp(("arbitrary", "arbitrary")),
        name="experts",
    )(blk_expert, blk_src, n_used, xs, b_gu4, b_d4, w_gu, w_down)


def _combine_kernel(dest_ref, ys_ref, gate_ref, x_ref, g2_ref, fg_ref, o_ref, buf_ref, sem):
    tb = x_ref.shape[0]
    i = pl.program_id(0)
    half = i % 2

    def gather(blk, slot):
        def start(t, carry):
            for j in range(TOP_K):
                d = dest_ref[(blk * tb + t) * TOP_K + j]
                pltpu.make_async_copy(ys_ref.at[pl.ds(d, 1), :], buf_ref.at[slot, j, pl.ds(t, 1), :],
                                      sem.at[slot]).start()
            return carry
        lax.fori_loop(0, tb, start, 0, unroll=ROW_DMA_UNROLL)

    @pl.when(i == 0)
    def _():
        gather(i, half)

    @pl.when(i + 1 < pl.num_programs(0))
    def _():
        gather(i + 1, 1 - half)

    for j in range(TOP_K):
        pltpu.make_async_copy(ys_ref.at[pl.ds(0, tb), :], buf_ref.at[half, j], sem.at[half]).wait()

    gate = gate_ref[...]
    ffn = gate[:, 0:1] * buf_ref[half, 0]
    for j in range(1, TOP_K):
        ffn = ffn + gate[:, j:j + 1] * buf_ref[half, j]
    x2 = x_ref[...] + g2_ref[0] * ffn
    o_ref[...] = _rms(x2, NORM_EPS) * fg_ref[...]


def _combine(dest_flat, ys, gates, x1, g2, fg, seq):
    t, d = x1.shape
    tb = min(COMBINE_TB, seq)
    per_b = seq // tb
    return pl.pallas_call(
        _combine_kernel,
        grid_spec=pltpu.PrefetchScalarGridSpec(
            num_scalar_prefetch=1,
            grid=(t // tb,),
            in_specs=[pl.BlockSpec(memory_space=pl.ANY),
                      pl.BlockSpec((tb, LANES), lambda i, dest: (i, 0)),
                      pl.BlockSpec((tb, d), lambda i, dest: (i, 0)),
                      pl.BlockSpec((1, 1, d), lambda i, dest: (i // per_b, 0, 0)),
                      pl.BlockSpec((1, d), lambda i, dest: (0, 0))],
            out_specs=pl.BlockSpec((tb, d), lambda i, dest: (i, 0)),
            scratch_shapes=[pltpu.VMEM((2, TOP_K, tb, d), F32), pltpu.SemaphoreType.DMA((2,))],
        ),
        out_shape=jax.ShapeDtypeStruct((t, d), F32),
        compiler_params=_cp(("arbitrary",)),
        name="combine",
    )(dest_flat, ys, gates, x1, g2, fg.reshape(1, d))


def _moe(x1, g, sc2, sh2, g2, router_w, router_b, w_gu, b_gu, w_down, b_down, final_g, seq):
    t, d = x1.shape
    a = t * TOP_K
    tm = MOE_TM
    nb = -(-a // tm) + N_EXPERTS
    idx_w, gate_w, rank_w, counts = _route(x1, g, sc2, sh2, router_w, router_b, seq)
    idx = idx_w[:, :TOP_K]
    rank = rank_w[:, :TOP_K]
    cnt = counts.reshape(-1).astype(jnp.int32)
    padded = ((cnt + tm - 1) // tm) * tm
    pad_end = jnp.cumsum(padded)
    pad_start = pad_end - padded
    experts = jnp.arange(N_EXPERTS, dtype=jnp.int32)
    start_of = jnp.sum(jnp.where(idx[:, :, None] == experts, pad_start, 0), axis=-1)
    dest = (start_of + rank).reshape(a).astype(jnp.int32)
    n_used = (pad_end[-1] // tm).astype(jnp.int32)
    blk = jnp.arange(nb, dtype=jnp.int32)
    blk_src = jnp.clip(blk, 0, jnp.maximum(n_used - 1, 0))
    blk_expert = jnp.sum((pad_end[None, :] <= (blk_src * tm)[:, None]).astype(jnp.int32), axis=1)
    blk_expert = jnp.minimum(blk_expert, N_EXPERTS - 1)
    last_blk = jnp.where(padded > 0, pad_end // tm - 1, -1)
    zero_flag = ((blk >= n_used) | jnp.any(last_blk[None, :] == blk[:, None], axis=1)).astype(jnp.int32)
    xs = _scatter_rows(dest, zero_flag, x1, g, sc2, sh2, nb * tm, seq)
    ys = _experts(blk_expert, blk_src, n_used.reshape(1), xs, w_gu, b_gu, w_down, b_down)
    return _combine(dest, ys, gate_w, x1, g2, final_g, seq)


def kernel(x, c, ada_w, ada_b, norm1_g, norm2_g, w_in, rwkv_mu, rwkv_w0, rwkv_w2, rwkv_a0, rwkv_a2, rwkv_g2,
           rwkv_k_k, rwkv_k_a, rwkv_r_k, rwkv_gn_g, rwkv_gn_b, hgrn_lb_logits, hgrn_gn_g, proj_a, proj_b, w_out,
           router_w, router_b, exp_w_gate_up, exp_b_gate_up, exp_w_down, exp_b_down, final_norm_g):
    bsz, seq, d = x.shape
    depth = ada_w.shape[0]
    lower_bounds = jnp.cumsum(jax.nn.softmax(hgrn_lb_logits.astype(F32), axis=0), axis=0)
    xf = x.reshape(bsz * seq, d)
    rkv = 3 * RWKV_WIDTH
    rwkv_cols = rkv + LORA_COLS
    hg_gate = 4 * HGRN_WIDTH + 2 * d
    for l in range(depth):
        mod = _adaln(c, ada_w[l], ada_b[l])
        sh1, sc1, g1, sh2, sc2, g2 = [m.reshape(bsz, 1, d) for m in jnp.split(mod, 6, axis=-1)]

        wl = w_in[l]
        w_cat = jnp.concatenate([wl[:, :rkv], wl[:, rwkv_cols:rwkv_cols + hg_gate], wl[:, rkv:rwkv_cols],
                                 jnp.zeros((d, LORA_PAD - LORA_COLS), wl.dtype)], axis=1).astype(BF16)
        proj = _inproj(xf, norm1_g[l], sc1, sh1, w_cat, seq)

        mu = rwkv_mu[l]
        mu_lo = jnp.pad(mu[rkv:], (0, LORA_PAD - LORA_COLS))
        pad_rows = lambda w, r0: jnp.zeros((LORA_PAD, RWKV_WIDTH), F32).at[r0:r0 + w.shape[0]].set(w).astype(BF16)
        ya = _rwkv(proj, bsz, seq, mu[:rkv], mu_lo, rwkv_w0[l], rwkv_a0[l], rwkv_k_k[l], rwkv_k_a[l],
                   rwkv_r_k[l].reshape(-1), rwkv_gn_g[l], rwkv_gn_b[l],
                   pad_rows(rwkv_w2[l], 0), pad_rows(rwkv_a2[l], DECAY_LORA),
                   pad_rows(rwkv_g2[l], DECAY_LORA + AAA_LORA))
        yb = _hgrn(proj, bsz, seq, lower_bounds[l], hgrn_gn_g[l])
        x1 = _mixout(ya, yb, proj, proj_a[l].astype(BF16), proj_b[l].astype(BF16), w_out[l].astype(BF16), xf, g1, seq)
        assert depth == 1
        out = _moe(x1, norm2_g[l], sc2, sh2, g2, router_w[l], router_b[l], exp_w_gate_up[l], exp_b_gate_up[l],
                   exp_w_down[l], exp_b_down[l], final_norm_g, seq)
    return out.reshape(bsz, seq, d)
```

```python
import jax
import jax.numpy as jnp
from jax import lax
from jax.experimental import pallas as pl
from jax.experimental.pallas import tpu as pltpu

F32 = jnp.float32
BF16 = jnp.bfloat16

D_MODEL = 2048
RWKV_WIDTH = 1024
RWKV_HEAD = 64
DECAY_LORA = 64
AAA_LORA = 64
GATE_LORA = 160
LORA_COLS = DECAY_LORA + AAA_LORA + GATE_LORA
LORA_PAD = 512
RWKV_GN_EPS = 64e-5
HGRN_WIDTH = 1024
HGRN_HEAD = 128
GLA_BLOCK = 16
N_EXPERTS = 32
TOP_K = 4
D_EXPERT = 2048
SWIGLU_ALPHA = 1.702
SWIGLU_LIMIT = 7.0
NORM_EPS = 1e-5

LANES = 128
VMEM_LIMIT = 56 * 1024 * 1024

RWKV_CHUNK = 64
RWKV_STEP_CHUNKS = 2
HGRN_STEP = 128
PROJ_TM = 1024
PROJ_TN = 512
NORM_ROWS = 128
MIX_TM = 256
MOE_TM = 512
MOE_TN = 512
MOE_TNB = 1024
MOE_RING = 3
ROUTE_TB = 256
SCATTER_TB = 256
COMBINE_TB = 128
ROW_DMA_UNROLL = 8


def _cp(sem, vmem=VMEM_LIMIT):
    return pltpu.CompilerParams(dimension_semantics=sem, vmem_limit_bytes=vmem)


def _mm(a, b):
    return jnp.dot(a.astype(BF16), b.astype(BF16), preferred_element_type=F32)


def _mm_nt(a, b):
    return lax.dot_general(a.astype(BF16), b.astype(BF16), (((1,), (1,)), ((), ())),
                           preferred_element_type=F32)


def _mm_tn(a, b):
    return lax.dot_general(a.astype(BF16), b.astype(BF16), (((0,), (0,)), ((), ())),
                           preferred_element_type=F32)


def _split3(x):
    hi = x.astype(BF16)
    r1 = x - hi.astype(F32)
    mid = r1.astype(BF16)
    lo = (r1 - mid.astype(F32)).astype(BF16)
    return hi, mid, lo


def _mm_exact_rhs(m_bf16, x):
    hi, mid, lo = _split3(x)
    d = lambda p: jnp.dot(m_bf16, p, preferred_element_type=F32)
    return d(hi) + d(mid) + d(lo)


def _sigmoid(x):
    return 1.0 / (1.0 + jnp.exp(-x))


def _softplus(x):
    return jnp.maximum(x, 0.0) + jnp.log(1.0 + jnp.exp(-jnp.abs(x)))


def _rms(x, eps):
    return x * lax.rsqrt(jnp.mean(x * x, axis=-1, keepdims=True) + eps)


def _adaln_kernel(c_ref, w_ref, b_ref, o_ref):
    c = c_ref[...]
    c_act = c * _sigmoid(c)
    o_ref[...] = jnp.dot(c_act, w_ref[...], preferred_element_type=F32,
                         precision=lax.Precision.HIGHEST) + b_ref[...]


def _adaln(c, w, b):
    bsz, d = c.shape
    n = w.shape[1]
    tn = 1024
    return pl.pallas_call(
        _adaln_kernel,
        grid=(n // tn,),
        in_specs=[pl.BlockSpec((bsz, d), lambda j: (0, 0)),
                  pl.BlockSpec((d, tn), lambda j: (0, j)),
                  pl.BlockSpec((1, tn), lambda j: (0, j))],
        out_specs=pl.BlockSpec((bsz, tn), lambda j: (0, j)),
        out_shape=jax.ShapeDtypeStruct((bsz, n), F32),
        compiler_params=_cp(("arbitrary",)),
        name="adaln",
    )(c, w, b.reshape(1, n))


def _inproj_kernel(x_ref, g_ref, sc_ref, sh_ref, w_ref, o_ref, h_ref):
    @pl.when(pl.program_id(1) == 0)
    def _():
        def rows(c, carry):
            rs = pl.ds(pl.multiple_of(c * NORM_ROWS, NORM_ROWS), NORM_ROWS)
            y = _rms(x_ref[rs, :], NORM_EPS) * g_ref[...]
            h_ref[rs, :] = (y * (1.0 + sc_ref[0]) + sh_ref[0]).astype(BF16)
            return carry
        lax.fori_loop(0, x_ref.shape[0] // NORM_ROWS, rows, 0)

    o_ref[...] = jnp.dot(h_ref[...], w_ref[...], preferred_element_type=F32).astype(o_ref.dtype)


def _inproj(xf, g, sc, sh, w, seq):
    t, d = xf.shape
    n = w.shape[1]
    tm = min(PROJ_TM, seq)
    tn = PROJ_TN
    per_b = seq // tm
    return pl.pallas_call(
        _inproj_kernel,
        grid=(t // tm, n // tn),
        in_specs=[pl.BlockSpec((tm, d), lambda i, j: (i, 0)),
                  pl.BlockSpec((1, d), lambda i, j: (0, 0)),
                  pl.BlockSpec((1, 1, d), lambda i, j: (i // per_b, 0, 0)),
                  pl.BlockSpec((1, 1, d), lambda i, j: (i // per_b, 0, 0)),
                  pl.BlockSpec((d, tn), lambda i, j: (0, j))],
        out_specs=pl.BlockSpec((tm, tn), lambda i, j: (i, j)),
        out_shape=jax.ShapeDtypeStruct((t, n), BF16),
        scratch_shapes=[pltpu.VMEM((tm, d), BF16)],
        compiler_params=_cp(("arbitrary", "arbitrary")),
        name="inproj",
    )(xf, g.reshape(1, d), sc, sh, w)


def _rwkv_chunk(rows_sl, r_ref, k_ref, v_ref, lo_ref, mu_ref, mulo_ref, w0_ref, a0_ref, kk_ref, ka_ref, rk_ref,
                gng_ref, gnb_ref, w2_ref, a2_ref, g2_ref, o_ref,
                st_ref, cr_ref, ck_ref, cv_ref, clo_ref):
    L = RWKV_CHUNK
    H2 = 2 * L
    n_pair = RWKV_WIDTH // LANES

    def shift(x_ref, mu, carry_ref):
        x = x_ref[rows_sl, :].astype(F32)
        rows = lax.broadcasted_iota(jnp.int32, x.shape, 0)
        prev = jnp.where(rows == 0, carry_ref[...], pltpu.roll(x, 1, axis=0))
        carry_ref[...] = x[L - 1:L, :]
        return x + (prev - x) * mu

    mu = mu_ref[...]
    r = shift(r_ref, mu[:, 0:RWKV_WIDTH], cr_ref)
    k = shift(k_ref, mu[:, RWKV_WIDTH:2 * RWKV_WIDTH], ck_ref)
    v = shift(v_ref, mu[:, 2 * RWKV_WIDTH:3 * RWKV_WIDTH], cv_ref)
    lo = shift(lo_ref, mulo_ref[...], clo_ref)

    wpre = w0_ref[...] + _mm(jnp.tanh(lo), w2_ref[...])
    lw = -jnp.exp(-_softplus(-wpre) - 0.5)
    a = _sigmoid(a0_ref[...] + _mm(lo, a2_ref[...]))
    g = _mm(_sigmoid(lo), g2_ref[...])

    ti = lax.broadcasted_iota(jnp.int32, (L, L), 0)
    tj = lax.broadcasted_iota(jnp.int32, (L, L), 1)
    tri = jnp.where(tj <= ti, 1.0, 0.0).astype(BF16)
    cl = _mm_exact_rhs(tri, lw)
    cl_last = cl[L - 1:L, :]
    e_pos = jnp.exp(cl)
    e_neg = jnp.exp(-cl)
    e_prev = jnp.exp(cl - lw)
    e_end = jnp.exp(cl_last - cl)
    p_last = jnp.exp(cl_last)

    kk = k * kk_ref[...]
    k2 = k * (1.0 + (a - 1.0) * ka_ref[...])
    f_b = a * e_neg
    f_bend = a * e_end
    rk = rk_ref[...]
    gng = gng_ref[...]
    gnb = gnb_ref[...]

    lane = lax.broadcasted_iota(jnp.int32, (L, LANES), 1)
    lo_half = lane < RWKV_HEAD
    si = lax.broadcasted_iota(jnp.int32, (H2, H2), 0)
    sj = lax.broadcasted_iota(jnp.int32, (H2, H2), 1)
    same = (si // L) == (sj // L)
    m_strict = same & (sj < si)
    m_incl = same & (sj <= si)
    eye = si == sj
    own = (si // L) == (sj // RWKV_HEAD)

    def stack(x):
        return jnp.concatenate([jnp.where(lo_half, x, 0.0), jnp.where(lo_half, 0.0, x)], axis=0)

    def rep(x):
        return jnp.concatenate([x, x], axis=0)

    pairs = range(n_pair)
    sls = [slice(p * LANES, (p + 1) * LANES) for p in pairs]
    at, bt, bend, kt, kend, rt, v_s, bonus = [], [], [], [], [], [], [], []
    for sl in sls:
        kk_s = stack(kk[:, sl])
        nrm = jnp.sqrt(jnp.sum(kk_s * kk_s, axis=-1, keepdims=True))
        kkn = kk_s / jnp.maximum(nrm, 1e-12)
        at.append(-kkn * rep(e_prev[:, sl]))
        bt.append(kkn * rep(f_b[:, sl]))
        bend.append(kkn * rep(f_bend[:, sl]))
        k2_s = stack(k2[:, sl])
        kt.append(k2_s * rep(e_neg[:, sl]))
        kend.append(k2_s * rep(e_end[:, sl]))
        r_s = stack(r[:, sl])
        rt.append(r_s * rep(e_pos[:, sl]))
        vs = stack(v[:, sl])
        v_s.append(vs)
        bonus.append(jnp.sum(r_s * k2_s * rk[:, sl], axis=-1, keepdims=True) * vs)

    gm = [_mm_nt(jnp.concatenate([at[p], rt[p]], axis=0), jnp.concatenate([bt[p], kt[p]], axis=0))
          for p in pairs]
    a_ab = [jnp.where(m_strict, gm[p][:H2, :H2], 0.0) for p in pairs]
    a_ak = [jnp.where(m_strict, gm[p][:H2, H2:], 0.0) for p in pairs]
    a_rb = [jnp.where(m_incl, gm[p][H2:, :H2], 0.0) for p in pairs]
    a_rk = [jnp.where(m_incl, gm[p][H2:, H2:], 0.0) for p in pairs]

    ident = jnp.where(eye, 1.0, 0.0)
    tinv = [ident + a_ab[p] for p in pairs]
    pw = a_ab
    akv = [_mm(a_ak[p], v_s[p]) for p in pairs]
    ark_v = [_mm(a_rk[p], v_s[p]) for p in pairs]
    kend_v = [_mm_tn(kend[p], v_s[p]) for p in pairs]
    for _ in range(max(1, (L - 1).bit_length() - 1)):
        pw = [_mm(pw[p], pw[p]) for p in pairs]
        tinv = [tinv[p] + _mm(tinv[p], pw[p]) for p in pairs]

    tx = [_mm(tinv[p], jnp.concatenate([at[p], akv[p]], axis=1)) for p in pairs]
    y2 = [_mm(a_rb[p], tx[p]) for p in pairs]
    btx = [_mm_tn(bend[p], tx[p]) for p in pairs]
    s0 = [st_ref[p] for p in pairs]
    y = [_mm(rt[p] + y2[p][:, :LANES], s0[p]) + (y2[p][:, LANES:] + ark_v[p]) for p in pairs]
    for p in pairs:
        m_mat = jnp.where(eye, jnp.broadcast_to(p_last[:, sls[p]], (H2, LANES)), 0.0) + btx[p][:, :LANES]
        st_ref[p] = _mm(m_mat, s0[p]) + (btx[p][:, LANES:] + kend_v[p])

    for p in pairs:
        sl = sls[p]
        mean = jnp.sum(y[p], axis=-1, keepdims=True) * (1.0 / RWKV_HEAD)
        dlt = jnp.where(own, y[p] - mean, 0.0)
        var = jnp.sum(dlt * dlt, axis=-1, keepdims=True) * (1.0 / RWKV_HEAD)
        yn = dlt * lax.rsqrt(var + RWKV_GN_EPS) * gng[:, sl] + jnp.where(own, gnb[:, sl], 0.0)
        ys = yn + bonus[p]
        o_ref[rows_sl, sl] = ((ys[:L] + ys[L:]) * g[:, sl]).astype(o_ref.dtype)


def _rwkv_kernel(*refs):
    st_ref, cr_ref, ck_ref, cv_ref, clo_ref = refs[-5:]

    @pl.when(pl.program_id(1) == 0)
    def _():
        for ref in (st_ref, cr_ref, ck_ref, cv_ref, clo_ref):
            ref[...] = jnp.zeros_like(ref)

    for c in range(RWKV_STEP_CHUNKS):
        _rwkv_chunk(slice(c * RWKV_CHUNK, (c + 1) * RWKV_CHUNK), *refs)


def _rwkv(proj, bsz, seq, mu_rkv, mu_lo, w0, a0, k_k, k_a, r_k, gn_g, gn_b, w2p, a2p, g2p):
    L = RWKV_CHUNK * RWKV_STEP_CHUNKS
    nc = seq // L
    W = RWKV_WIDTH
    row = lambda n: pl.BlockSpec((1, n), lambda b, c: (0, 0))
    full = lambda a: pl.BlockSpec(a.shape, lambda b, c: (0, 0))
    lo_blk = proj.shape[1] // LORA_PAD - 1
    v1 = lambda a: a.reshape(1, -1)
    return pl.pallas_call(
        _rwkv_kernel,
        grid=(bsz, nc),
        in_specs=[pl.BlockSpec((L, W), lambda b, c: (b * nc + c, 0)),
                  pl.BlockSpec((L, W), lambda b, c: (b * nc + c, 1)),
                  pl.BlockSpec((L, W), lambda b, c: (b * nc + c, 2)),
                  pl.BlockSpec((L, LORA_PAD), lambda b, c: (b * nc + c, lo_blk)),
                  row(3 * W), row(LORA_PAD), row(W), row(W), row(W), row(W), row(W), row(W), row(W),
                  full(w2p), full(a2p), full(g2p)],
        out_specs=pl.BlockSpec((L, W), lambda b, c: (b * nc + c, 0)),
        out_shape=jax.ShapeDtypeStruct((bsz * seq, W), BF16),
        scratch_shapes=[pltpu.VMEM((W // LANES, LANES, LANES), F32),
                        pltpu.VMEM((1, W), F32), pltpu.VMEM((1, W), F32), pltpu.VMEM((1, W), F32),
                        pltpu.VMEM((1, LORA_PAD), F32)],
        compiler_params=_cp(("arbitrary", "arbitrary")),
        name="rwkv7",
    )(proj, proj, proj, proj, v1(mu_rkv), v1(mu_lo), v1(w0), v1(a0), v1(k_k), v1(k_a), v1(r_k), v1(gn_g),
      v1(gn_b), w2p, a2p, g2p)


def _hgrn_kernel(q_ref, f_ref, i_ref, og_ref, lb_ref, gn_ref, o_ref, st_ref):
    n = HGRN_STEP
    blk = GLA_BLOCK
    nb = n // blk
    n_head = HGRN_WIDTH // HGRN_HEAD

    @pl.when(pl.program_id(1) == 0)
    def _():
        st_ref[...] = jnp.zeros_like(st_ref)

    lb = lb_ref[...]
    qf = q_ref[...].astype(F32)
    q = qf * _sigmoid(qf)
    forget = lb + (1.0 - lb) * _sigmoid(f_ref[...].astype(F32))
    log_f = jnp.log(forget)
    kx = 1.0 - forget
    vv = i_ref[...].astype(F32)
    og = og_ref[...].astype(F32)

    ti = lax.broadcasted_iota(jnp.int32, (n, n), 0)
    tj = lax.broadcasted_iota(jnp.int32, (n, n), 1)
    same = (ti // blk) == (tj // blk)
    causal = same & (tj <= ti)
    b = _mm_exact_rhs(jnp.where(causal, 1.0, 0.0).astype(BF16), log_f)
    b_last = _mm_exact_rhs(jnp.where(same, 1.0, 0.0).astype(BF16), log_f)
    q_dec = q * jnp.exp(b)
    k_dec = kx * jnp.exp(-b)
    k_end = kx * jnp.exp(b_last - b)
    d_end = jnp.exp(b_last)
    rows = lax.broadcasted_iota(jnp.int32, (n, HGRN_HEAD), 0)
    gn = gn_ref[...]

    heads = range(n_head)
    sls = [slice(h * HGRN_HEAD, (h + 1) * HGRN_HEAD) for h in heads]
    attn = [jnp.where(causal, _mm_nt(q_dec[:, sl], k_dec[:, sl]), 0.0) for sl in sls]
    o_intra = [_mm(attn[h], vv[:, sls[h]]) for h in heads]
    vt = [vv[:, sl].T for sl in sls]
    blk_mask = [(rows // blk) == c for c in range(nb)]
    kv_all = [_mm(vt[h], jnp.concatenate([jnp.where(blk_mask[c], k_end[:, sls[h]], 0.0) for c in range(nb)],
                                         axis=1)) for h in heads]
    kv_t = [[kv_all[h][:, c * HGRN_HEAD:(c + 1) * HGRN_HEAD] for c in range(nb)] for h in heads]
    st = [[st_ref[h]] for h in heads]
    for h in heads:
        for c in range(nb):
            st[h].append(st[h][c] * d_end[c * blk:c * blk + 1, sls[h]] + kv_t[h][c])
        st_ref[h] = st[h][nb]
    for h in heads:
        sl = sls[h]
        inter = [_mm_nt(q_dec[c * blk:(c + 1) * blk, sl], st[h][c]) for c in range(nb)]
        o = o_intra[h] + jnp.concatenate(inter, axis=0)
        o = _rms(o, NORM_EPS) * gn
        ogh = og[:, sl]
        o_ref[:, sl] = (o * (ogh * _sigmoid(ogh))).astype(o_ref.dtype)


def _hgrn(proj, bsz, seq, lb, gn_g):
    n = HGRN_STEP
    ns = seq // n
    W = HGRN_WIDTH
    col = lambda j: pl.BlockSpec((n, W), lambda b, c: (b * ns + c, j))
    return pl.pallas_call(
        _hgrn_kernel,
        grid=(bsz, ns),
        in_specs=[col(3), col(4), col(5), col(6),
                  pl.BlockSpec((1, W), lambda b, c: (0, 0)),
                  pl.BlockSpec((1, HGRN_HEAD), lambda b, c: (0, 0))],
        out_specs=pl.BlockSpec((n, W), lambda b, c: (b * ns + c, 0)),
        out_shape=jax.ShapeDtypeStruct((bsz * seq, W), BF16),
        scratch_shapes=[pltpu.VMEM((W // HGRN_HEAD, HGRN_HEAD, HGRN_HEAD), F32)],
        compiler_params=_cp(("arbitrary", "arbitrary")),
        name="hgrn2",
    )(proj, proj, proj, proj, lb.reshape(1, W), gn_g.reshape(1, HGRN_HEAD))


def _mixout_kernel(ya_ref, yb_ref, ga0_ref, ga1_ref, gb0_ref, gb1_ref, pa_ref, pb_ref, wo_ref, x_ref, g_ref, o_ref):
    half = ga0_ref.shape[1]
    ya = ya_ref[...]
    yb = yb_ref[...]
    parts = []
    for c, (ga_ref, gb_ref) in enumerate(((ga0_ref, gb0_ref), (ga1_ref, gb1_ref))):
        cs = slice(c * half, (c + 1) * half)
        ma = jnp.dot(ya, pa_ref[:, cs], preferred_element_type=F32)
        mb = jnp.dot(yb, pb_ref[:, cs], preferred_element_type=F32)
        parts.append((_sigmoid(ga_ref[...].astype(F32)) * ma + _sigmoid(gb_ref[...].astype(F32)) * mb).astype(BF16))
    mix = (jnp.dot(parts[0], wo_ref[0:half, :], preferred_element_type=F32)
           + jnp.dot(parts[1], wo_ref[half:2 * half, :], preferred_element_type=F32))
    o_ref[...] = x_ref[...] + g_ref[0] * mix


def _mixout(ya, yb, proj, pa, pb, wo, xf, g1, seq):
    t, w = ya.shape
    d = pa.shape[1]
    tm = min(MIX_TM, seq)
    per_b = seq // tm
    half = d // 2
    ga0 = (3 * RWKV_WIDTH + 4 * HGRN_WIDTH) // half
    gate = lambda j: pl.BlockSpec((tm, half), lambda i: (i, ga0 + j))
    const = lambda a: pl.BlockSpec(a.shape, lambda i: (0, 0), pipeline_mode=pl.Buffered(1))
    return pl.pallas_call(
        _mixout_kernel,
        grid=(t // tm,),
        in_specs=[pl.BlockSpec((tm, w), lambda i: (i, 0)),
                  pl.BlockSpec((tm, w), lambda i: (i, 0)),
                  gate(0), gate(1), gate(2), gate(3),
                  const(pa), const(pb), const(wo),
                  pl.BlockSpec((tm, d), lambda i: (i, 0)),
                  pl.BlockSpec((1, 1, d), lambda i: (i // per_b, 0, 0))],
        out_specs=pl.BlockSpec((tm, d), lambda i: (i, 0)),
        out_shape=jax.ShapeDtypeStruct((t, d), F32),
        compiler_params=_cp(("arbitrary",)),
        name="mixout",
    )(ya, yb, proj, proj, proj, proj, pa, pb, wo, xf, g1)


def _route_kernel(x_ref, g_ref, sc_ref, sh_ref, rwh_ref, rwm_ref, rb_ref, h_ref, idx_ref, gate_ref, rank_ref,
                  cnt_ref, carry_ref):
    tb = x_ref.shape[0]

    @pl.when(pl.program_id(0) == 0)
    def _():
        carry_ref[...] = jnp.zeros_like(carry_ref)

    h = (_rms(x_ref[...], NORM_EPS) * g_ref[...]) * (1.0 + sc_ref[0]) + sh_ref[0]
    h_ref[...] = h
    h_hi, h_mid, _ = _split3(h)
    d = lambda a, b: jnp.dot(a, b, preferred_element_type=F32)
    logits = d(h_hi, rwh_ref[...]) + (d(h_hi, rwm_ref[...]) + d(h_mid, rwh_ref[...])) + rb_ref[...]

    lane = lax.broadcasted_iota(jnp.int32, logits.shape, 1).astype(F32)
    work = logits
    vals, idxs, sels = [], [], []
    for _ in range(TOP_K):
        m = jnp.max(work, axis=-1, keepdims=True)
        idx = jnp.min(jnp.where(work == m, lane, float(N_EXPERTS)), axis=-1, keepdims=True)
        sel = lane == idx
        vals.append(m)
        idxs.append(idx)
        sels.append(sel)
        work = jnp.where(sel, -jnp.inf, work)
    exps = [jnp.exp(v - vals[0]) for v in vals]
    den = exps[0] + exps[1] + exps[2] + exps[3]

    onehot = jnp.zeros(logits.shape, F32)
    for sel in sels:
        onehot = onehot + jnp.where(sel, 1.0, 0.0)
    ti = lax.broadcasted_iota(jnp.int32, (tb, tb), 0)
    tj = lax.broadcasted_iota(jnp.int32, (tb, tb), 1)
    below = jnp.where(tj < ti, 1.0, 0.0).astype(BF16)
    before = jnp.dot(below, onehot.astype(BF16), preferred_element_type=F32) + carry_ref[...]
    carry_ref[...] = carry_ref[...] + jnp.sum(onehot, axis=0, keepdims=True)
    cnt_ref[...] = carry_ref[...]

    out_lane = lax.broadcasted_iota(jnp.int32, (tb, LANES), 1)
    idx_out = jnp.zeros((tb, LANES), jnp.int32)
    rank_out = jnp.zeros((tb, LANES), jnp.int32)
    gate_out = jnp.zeros((tb, LANES), F32)
    for j in range(TOP_K):
        rank = jnp.sum(jnp.where(sels[j], before, 0.0), axis=-1, keepdims=True)
        idx_out = jnp.where(out_lane == j, idxs[j].astype(jnp.int32), idx_out)
        rank_out = jnp.where(out_lane == j, rank.astype(jnp.int32), rank_out)
        gate_out = jnp.where(out_lane == j, exps[j] / den, gate_out)
    idx_ref[...] = idx_out
    rank_ref[...] = rank_out
    gate_ref[...] = gate_out


def _route(x1, g, sc, sh, rw, rb, seq):
    t, d = x1.shape
    tb = min(ROUTE_TB, seq)
    per_b = seq // tb
    e = rw.shape[1]
    rw_hi = rw.astype(BF16)
    rw_mid = (rw - rw_hi.astype(F32)).astype(BF16)
    wide = lambda dt: jax.ShapeDtypeStruct((t, LANES), dt)
    return pl.pallas_call(
        _route_kernel,
        grid=(t // tb,),
        in_specs=[pl.BlockSpec((tb, d), lambda i: (i, 0)),
                  pl.BlockSpec((1, d), lambda i: (0, 0)),
                  pl.BlockSpec((1, 1, d), lambda i: (i // per_b, 0, 0)),
                  pl.BlockSpec((1, 1, d), lambda i: (i // per_b, 0, 0)),
                  pl.BlockSpec((d, e), lambda i: (0, 0)),
                  pl.BlockSpec((d, e), lambda i: (0, 0)),
                  pl.BlockSpec((1, e), lambda i: (0, 0))],
        out_specs=[pl.BlockSpec((tb, d), lambda i: (i, 0)),
                   pl.BlockSpec((tb, LANES), lambda i: (i, 0)),
                   pl.BlockSpec((tb, LANES), lambda i: (i, 0)),
                   pl.BlockSpec((tb, LANES), lambda i: (i, 0)),
                   pl.BlockSpec((1, e), lambda i: (0, 0))],
        out_shape=[jax.ShapeDtypeStruct((t, d), F32), wide(jnp.int32), wide(F32), wide(jnp.int32),
                   jax.ShapeDtypeStruct((1, e), F32)],
        scratch_shapes=[pltpu.VMEM((1, e), F32)],
        compiler_params=_cp(("arbitrary",)),
        name="route",
    )(x1, g.reshape(1, d), sc, sh, rw_hi, rw_mid, rb.reshape(1, e))


def _scatter_kernel(dest_ref, zflag_ref, h_ref, xs_ref, zbuf, sem, zsem):
    tb = h_ref.shape[0]
    zrows = zbuf.shape[0]
    base = pl.program_id(0) * tb

    @pl.when(pl.program_id(0) == 0)
    def _():
        zbuf[...] = jnp.zeros_like(zbuf)

        def zcopy(b):
            return pltpu.make_async_copy(zbuf, xs_ref.at[pl.ds(pl.multiple_of(b * zrows, zrows), zrows), :], zsem)

        def zstart(b, carry):
            @pl.when(zflag_ref[b] != 0)
            def _():
                zcopy(b).start()
            return carry

        def zwait(b, carry):
            @pl.when(zflag_ref[b] != 0)
            def _():
                zcopy(b).wait()
            return carry

        n_blk = xs_ref.shape[0] // zrows
        lax.fori_loop(0, n_blk, zstart, 0)
        lax.fori_loop(0, n_blk, zwait, 0)

    def copy(t, j):
        d = dest_ref[(base + t) * TOP_K + j]
        return pltpu.make_async_copy(h_ref.at[pl.ds(t, 1), :], xs_ref.at[pl.ds(d, 1), :], sem)

    def start(t, carry):
        for j in range(TOP_K):
            copy(t, j).start()
        return carry

    lax.fori_loop(0, tb, start, 0, unroll=ROW_DMA_UNROLL)
    for j in range(TOP_K):
        pltpu.make_async_copy(h_ref, xs_ref.at[pl.ds(0, tb), :], sem).wait()


def _scatter_rows(dest_flat, zero_flag, h2, n_rows):
    t, d = h2.shape
    tb = min(SCATTER_TB, t)
    return pl.pallas_call(
        _scatter_kernel,
        grid_spec=pltpu.PrefetchScalarGridSpec(
            num_scalar_prefetch=2,
            grid=(t // tb,),
            in_specs=[pl.BlockSpec((tb, d), lambda i, dest, zf: (i, 0))],
            out_specs=pl.BlockSpec(memory_space=pl.ANY),
            scratch_shapes=[pltpu.VMEM((MOE_TM, d), F32), pltpu.SemaphoreType.DMA(()),
                            pltpu.SemaphoreType.DMA(())],
        ),
        out_shape=jax.ShapeDtypeStruct((n_rows, d), F32),
        compiler_params=_cp(("arbitrary",)),
        name="scatter_rows",
    )(dest_flat, zero_flag, h2)


def _expert_kernel(be_ref, bs_ref, nu_ref, x_ref, bgu_ref, bd_ref, wgu_hbm, wd_hbm, o_ref,
                   xb_ref, act_ref, wbuf, sem):
    del bs_ref
    i = pl.program_id(0)
    nt, _, tn = act_ref.shape
    n_slot, de, tnb = wbuf.shape
    ntb = o_ref.shape[1] // tnb
    n_steps = nt + ntb
    look = n_slot - 1
    live = i < nu_ref[0]

    def tile_copies(blk, step, slot):
        e = be_ref[blk]
        if step < nt:
            return [pltpu.make_async_copy(wgu_hbm.at[e, :, half * de + step * tn:half * de + (step + 1) * tn],
                                          wbuf.at[slot, :, half * tn:(half + 1) * tn], sem.at[slot])
                    for half in range(2)]
        colb = (step - nt) * tnb
        return [pltpu.make_async_copy(wd_hbm.at[e, :, colb:colb + tnb], wbuf.at[slot], sem.at[slot])]

    def request(blk, sub):
        blk = blk + sub // n_steps
        step = sub % n_steps

        @pl.when(blk < nu_ref[0])
        def _():
            for cp in tile_copies(blk, step, (blk * n_steps + step) % n_slot):
                cp.start()

    @pl.when(i == 0)
    def _():
        for sub in range(look):
            request(i, sub)

    @pl.when(live)
    def _():
        xb_ref[...] = x_ref[...].astype(BF16)
        for n in range(n_steps):
            request(i, n + look)
            slot = (i * n_steps + n) % n_slot
            for cp in tile_copies(i, n, slot):
                cp.wait()
            if n < nt:
                xb = xb_ref[...]
                w = wbuf[slot]
                glu = jnp.dot(xb, w[:, 0:tn].astype(BF16), preferred_element_type=F32) + bgu_ref[0, n]
                lin = jnp.dot(xb, w[:, tn:2 * tn].astype(BF16), preferred_element_type=F32) + bgu_ref[0, n + nt]
                glu = jnp.minimum(glu, SWIGLU_LIMIT)
                lin = jnp.clip(lin, -SWIGLU_LIMIT, SWIGLU_LIMIT)
                act_ref[n] = (glu * _sigmoid(SWIGLU_ALPHA * glu) * (lin + 1.0)).astype(BF16)
            else:
                m = n - nt
                wd = wbuf[slot]
                y = jnp.dot(act_ref[0], wd[0:tn].astype(BF16), preferred_element_type=F32)
                for k in range(1, nt):
                    y = y + jnp.dot(act_ref[k], wd[k * tn:(k + 1) * tn].astype(BF16), preferred_element_type=F32)
                o_ref[:, m * tnb:(m + 1) * tnb] = y + bd_ref[0, m]

    @pl.when(jnp.logical_not(live))
    def _():
        o_ref[...] = jnp.zeros_like(o_ref)


def _experts(blk_expert, blk_src, n_used, xs, w_gu, b_gu, w_down, b_down):
    n_rows, d = xs.shape
    tm, tn, tnb = MOE_TM, MOE_TN, MOE_TNB
    nb = n_rows // tm
    e, _, two_de = w_gu.shape
    de = two_de // 2
    nt = de // tn
    ntb = d // tnb
    assert 2 * tn == tnb and de == d
    b_gu4 = b_gu.reshape(e, 2 * nt, 1, tn)
    b_d4 = b_down.reshape(e, ntb, 1, tnb)
    return pl.pallas_call(
        _expert_kernel,
        grid_spec=pltpu.PrefetchScalarGridSpec(
            num_scalar_prefetch=3,
            grid=(nb,),
            in_specs=[pl.BlockSpec((tm, d), lambda i, be, bs, nu: (bs[i], 0)),
                      pl.BlockSpec((1, 2 * nt, 1, tn), lambda i, be, bs, nu: (be[i], 0, 0, 0)),
                      pl.BlockSpec((1, ntb, 1, tnb), lambda i, be, bs, nu: (be[i], 0, 0, 0)),
                      pl.BlockSpec(memory_space=pl.ANY),
                      pl.BlockSpec(memory_space=pl.ANY)],
            out_specs=pl.BlockSpec((tm, d), lambda i, be, bs, nu: (i, 0)),
            scratch_shapes=[pltpu.VMEM((tm, d), BF16), pltpu.VMEM((nt, tm, tn), BF16),
                            pltpu.VMEM((MOE_RING, de, tnb), F32), pltpu.SemaphoreType.DMA((MOE_RING,))],
        ),
        out_shape=jax.ShapeDtypeStruct((n_rows, d), F32),
        compiler_params=_cp(("arbitrary",)),
        name="experts",
    )(blk_expert, blk_src, n_used, xs, b_gu4, b_d4, w_gu, w_down)


def _combine_kernel(dest_ref, ys_ref, gate_ref, x_ref, g2_ref, fg_ref, o_ref, buf_ref, sem):
    tb = x_ref.shape[0]
    i = pl.program_id(0)
    half = i % 2

    def gather(blk, slot):
        def start(t, carry):
            for j in range(TOP_K):
                d = dest_ref[(blk * tb + t) * TOP_K + j]
                pltpu.make_async_copy(ys_ref.at[pl.ds(d, 1), :], buf_ref.at[slot, j, pl.ds(t, 1), :],
                                      sem.at[slot]).start()
            return carry
        lax.fori_loop(0, tb, start, 0, unroll=ROW_DMA_UNROLL)

    @pl.when(i == 0)
    def _():
        gather(i, half)

    @pl.when(i + 1 < pl.num_programs(0))
    def _():
        gather(i + 1, 1 - half)

    for j in range(TOP_K):
        pltpu.make_async_copy(ys_ref.at[pl.ds(0, tb), :], buf_ref.at[half, j], sem.at[half]).wait()

    gate = gate_ref[...]
    ffn = gate[:, 0:1] * buf_ref[half, 0]
    for j in range(1, TOP_K):
        ffn = ffn + gate[:, j:j + 1] * buf_ref[half, j]
    x2 = x_ref[...] + g2_ref[0] * ffn
    o_ref[...] = _rms(x2, NORM_EPS) * fg_ref[...]


def _combine(dest_flat, ys, gates, x1, g2, fg, seq):
    t, d = x1.shape
    tb = min(COMBINE_TB, seq)
    per_b = seq // tb
    return pl.pallas_call(
        _combine_kernel,
        grid_spec=pltpu.PrefetchScalarGridSpec(
            num_scalar_prefetch=1,
            grid=(t // tb,),
            in_specs=[pl.BlockSpec(memory_space=pl.ANY),
                      pl.BlockSpec((tb, LANES), lambda i, dest: (i, 0)),
                      pl.BlockSpec((tb, d), lambda i, dest: (i, 0)),
                      pl.BlockSpec((1, 1, d), lambda i, dest: (i // per_b, 0, 0)),
                      pl.BlockSpec((1, d), lambda i, dest: (0, 0))],
            out_specs=pl.BlockSpec((tb, d), lambda i, dest: (i, 0)),
            scratch_shapes=[pltpu.VMEM((2, TOP_K, tb, d), F32), pltpu.SemaphoreType.DMA((2,))],
        ),
        out_shape=jax.ShapeDtypeStruct((t, d), F32),
        compiler_params=_cp(("arbitrary",)),
        name="combine",
    )(dest_flat, ys, gates, x1, g2, fg.reshape(1, d))


def _moe(x1, g, sc2, sh2, g2, router_w, router_b, w_gu, b_gu, w_down, b_down, final_g, seq):
    t, d = x1.shape
    a = t * TOP_K
    tm = MOE_TM
    nb = -(-a // tm) + N_EXPERTS
    h2, idx_w, gate_w, rank_w, counts = _route(x1, g, sc2, sh2, router_w, router_b, seq)
    idx = idx_w[:, :TOP_K]
    rank = rank_w[:, :TOP_K]
    cnt = counts.reshape(-1).astype(jnp.int32)
    padded = ((cnt + tm - 1) // tm) * tm
    pad_end = jnp.cumsum(padded)
    pad_start = pad_end - padded
    experts = jnp.arange(N_EXPERTS, dtype=jnp.int32)
    start_of = jnp.sum(jnp.where(idx[:, :, None] == experts, pad_start, 0), axis=-1)
    dest = (start_of + rank).reshape(a).astype(jnp.int32)
    n_used = (pad_end[-1] // tm).astype(jnp.int32)
    blk = jnp.arange(nb, dtype=jnp.int32)
    blk_src = jnp.clip(blk, 0, jnp.maximum(n_used - 1, 0))
    blk_expert = jnp.sum((pad_end[None, :] <= (blk_src * tm)[:, None]).astype(jnp.int32), axis=1)
    blk_expert = jnp.minimum(blk_expert, N_EXPERTS - 1)
    last_blk = jnp.where(padded > 0, pad_end // tm - 1, -1)
    zero_flag = ((blk >= n_used) | jnp.any(last_blk[None, :] == blk[:, None], axis=1)).astype(jnp.int32)
    xs = _scatter_rows(dest, zero_flag, h2, nb * tm)
    ys = _experts(blk_expert, blk_src, n_used.reshape(1), xs, w_gu, b_gu, w_down, b_down)
    return _combine(dest, ys, gate_w, x1, g2, final_g, seq)


def kernel(x, c, ada_w, ada_b, norm1_g, norm2_g, w_in, rwkv_mu, rwkv_w0, rwkv_w2, rwkv_a0, rwkv_a2, rwkv_g2,
           rwkv_k_k, rwkv_k_a, rwkv_r_k, rwkv_gn_g, rwkv_gn_b, hgrn_lb_logits, hgrn_gn_g, proj_a, proj_b, w_out,
           router_w, router_b, exp_w_gate_up, exp_b_gate_up, exp_w_down, exp_b_down, final_norm_g):
    bsz, seq, d = x.shape
    depth = ada_w.shape[0]
    lower_bounds = jnp.cumsum(jax.nn.softmax(hgrn_lb_logits.astype(F32), axis=0), axis=0)
    xf = x.reshape(bsz * seq, d)
    rkv = 3 * RWKV_WIDTH
    rwkv_cols = rkv + LORA_COLS
    hg_gate = 4 * HGRN_WIDTH + 2 * d
    for l in range(depth):
        mod = _adaln(c, ada_w[l], ada_b[l])
        sh1, sc1, g1, sh2, sc2, g2 = [m.reshape(bsz, 1, d) for m in jnp.split(mod, 6, axis=-1)]

        wl = w_in[l]
        w_cat = jnp.concatenate([wl[:, :rkv], wl[:, rwkv_cols:rwkv_cols + hg_gate], wl[:, rkv:rwkv_cols],
                                 jnp.zeros((d, LORA_PAD - LORA_COLS), wl.dtype)], axis=1).astype(BF16)
        proj = _inproj(xf, norm1_g[l], sc1, sh1, w_cat, seq)

        mu = rwkv_mu[l]
        mu_lo = jnp.pad(mu[rkv:], (0, LORA_PAD - LORA_COLS))
        pad_rows = lambda w, r0: jnp.zeros((LORA_PAD, RWKV_WIDTH), F32).at[r0:r0 + w.shape[0]].set(w).astype(BF16)
        ya = _rwkv(proj, bsz, seq, mu[:rkv], mu_lo, rwkv_w0[l], rwkv_a0[l], rwkv_k_k[l], rwkv_k_a[l],
                   rwkv_r_k[l].reshape(-1), rwkv_gn_g[l], rwkv_gn_b[l],
                   pad_rows(rwkv_w2[l], 0), pad_rows(rwkv_a2[l], DECAY_LORA),
                   pad_rows(rwkv_g2[l], DECAY_LORA + AAA_LORA))
        yb = _hgrn(proj, bsz, seq, lower_bounds[l], hgrn_gn_g[l])
        x1 = _mixout(ya, yb, proj, proj_a[l].astype(BF16), proj_b[l].astype(BF16), w_out[l].astype(BF16), xf, g1, seq)
        assert depth == 1
        out = _moe(x1, norm2_g[l], sc2, sh2, g2, router_w[l], router_b[l], exp_w_gate_up[l], exp_b_gate_up[l],
                   exp_w_down[l], exp_b_down[l], final_norm_g, seq)
    return out.reshape(bsz, seq, d)
```

```python
import functools

import jax
import jax.numpy as jnp
from jax import lax
from jax.experimental import pallas as pl
from jax.experimental.pallas import tpu as pltpu

F32 = jnp.float32
BF16 = jnp.bfloat16

D_MODEL = 2048
RWKV_WIDTH = 1024
RWKV_HEAD = 64
DECAY_LORA = 64
AAA_LORA = 64
GATE_LORA = 160
LORA_COLS = DECAY_LORA + AAA_LORA + GATE_LORA
LORA_PAD = 512
RWKV_GN_EPS = 64e-5
HGRN_WIDTH = 1024
HGRN_HEAD = 128
GLA_BLOCK = 16
N_EXPERTS = 32
TOP_K = 4
D_EXPERT = 2048
SWIGLU_ALPHA = 1.702
SWIGLU_LIMIT = 7.0
NORM_EPS = 1e-5

LANES = 128
VMEM_LIMIT = 56 * 1024 * 1024

RWKV_CHUNK = 64
RWKV_STEP_CHUNKS = 2
HGRN_STEP = 128
PROJ_TM = 1024
PROJ_TN = 512
NORM_ROWS = 128
MIX_TM = 256
MIX_GATE_BLOCKS = 4
MOE_TM = 512
MOE_TN = 512
MOE_TNB = 1024
MOE_RING = 3
ROUTE_TB = 256
SCATTER_TB = 256
COMBINE_TB = 128
ROW_DMA_UNROLL = 8


def _cp(sem, vmem=VMEM_LIMIT):
    return pltpu.CompilerParams(dimension_semantics=sem, vmem_limit_bytes=vmem)


def _mm(a, b):
    return jnp.dot(a.astype(BF16), b.astype(BF16), preferred_element_type=F32)


def _mm_nt(a, b):
    return lax.dot_general(a.astype(BF16), b.astype(BF16), (((1,), (1,)), ((), ())),
                           preferred_element_type=F32)


def _mm_tn(a, b):
    return lax.dot_general(a.astype(BF16), b.astype(BF16), (((0,), (0,)), ((), ())),
                           preferred_element_type=F32)


def _split3(x):
    hi = x.astype(BF16)
    r1 = x - hi.astype(F32)
    mid = r1.astype(BF16)
    lo = (r1 - mid.astype(F32)).astype(BF16)
    return hi, mid, lo


def _mm_exact_rhs(m_bf16, x):
    hi, mid, lo = _split3(x)
    d = lambda p: jnp.dot(m_bf16, p, preferred_element_type=F32)
    return d(hi) + d(mid) + d(lo)


def _sigmoid(x):
    return 1.0 / (1.0 + jnp.exp(-x))


def _softplus(x):
    return jnp.maximum(x, 0.0) + jnp.log(1.0 + jnp.exp(-jnp.abs(x)))


def _rms(x, eps):
    return x * lax.rsqrt(jnp.mean(x * x, axis=-1, keepdims=True) + eps)


def _adaln_kernel(c_ref, w_ref, b_ref, o_ref):
    c = c_ref[...]
    c_act = c * _sigmoid(c)
    o_ref[...] = jnp.dot(c_act, w_ref[...], preferred_element_type=F32,
                         precision=lax.Precision.HIGHEST) + b_ref[...]


def _adaln(c, w, b):
    bsz, d = c.shape
    n = w.shape[1]
    tn = 1024
    return pl.pallas_call(
        _adaln_kernel,
        grid=(n // tn,),
        in_specs=[pl.BlockSpec((bsz, d), lambda j: (0, 0)),
                  pl.BlockSpec((d, tn), lambda j: (0, j)),
                  pl.BlockSpec((1, tn), lambda j: (0, j))],
        out_specs=pl.BlockSpec((bsz, tn), lambda j: (0, j)),
        out_shape=jax.ShapeDtypeStruct((bsz, n), F32),
        compiler_params=_cp(("arbitrary",)),
        name="adaln",
    )(c, w, b.reshape(1, n))


def _inproj_kernel(n_a, x_ref, g_ref, sc_ref, sh_ref, wa_ref, wb_ref, o_ref, h_ref):
    j = pl.program_id(1)

    @pl.when(j == 0)
    def _():
        def rows(c, carry):
            rs = pl.ds(pl.multiple_of(c * NORM_ROWS, NORM_ROWS), NORM_ROWS)
            y = _rms(x_ref[rs, :], NORM_EPS) * g_ref[...]
            h_ref[rs, :] = (y * (1.0 + sc_ref[0]) + sh_ref[0]).astype(BF16)
            return carry
        lax.fori_loop(0, x_ref.shape[0] // NORM_ROWS, rows, 0)

    @pl.when(j < n_a)
    def _():
        o_ref[...] = jnp.dot(h_ref[...], wa_ref[...], preferred_element_type=F32).astype(o_ref.dtype)

    @pl.when(j >= n_a)
    def _():
        o_ref[...] = jnp.dot(h_ref[...], wb_ref[...], preferred_element_type=F32).astype(o_ref.dtype)


def _inproj(xf, g, sc, sh, w_a, w_b, seq):
    t, d = xf.shape
    tm = min(PROJ_TM, seq)
    tn = PROJ_TN
    n_a, n_b = w_a.shape[1] // tn, w_b.shape[1] // tn
    per_b = seq // tm
    return pl.pallas_call(
        functools.partial(_inproj_kernel, n_a),
        grid=(t // tm, n_a + n_b),
        in_specs=[pl.BlockSpec((tm, d), lambda i, j: (i, 0)),
                  pl.BlockSpec((1, d), lambda i, j: (0, 0)),
                  pl.BlockSpec((1, 1, d), lambda i, j: (i // per_b, 0, 0)),
                  pl.BlockSpec((1, 1, d), lambda i, j: (i // per_b, 0, 0)),
                  pl.BlockSpec((d, tn), lambda i, j: (0, jnp.minimum(j, n_a - 1))),
                  pl.BlockSpec((d, tn), lambda i, j: (0, jnp.maximum(j - n_a, 0)))],
        out_specs=pl.BlockSpec((tm, tn), lambda i, j: (i, j)),
        out_shape=jax.ShapeDtypeStruct((t, (n_a + n_b) * tn), BF16),
        scratch_shapes=[pltpu.VMEM((tm, d), BF16)],
        compiler_params=_cp(("arbitrary", "arbitrary")),
        name="inproj",
    )(xf, g.reshape(1, d), sc, sh, w_a, w_b)


def _rwkv_chunk(rows_sl, r_ref, k_ref, v_ref, lo_ref, mu_ref, mulo_ref, w0_ref, a0_ref, kk_ref, ka_ref, rk_ref,
                gng_ref, gnb_ref, w2_ref, a2_ref, g2_ref, o_ref,
                st_ref, cr_ref, ck_ref, cv_ref, clo_ref):
    L = RWKV_CHUNK
    H2 = 2 * L
    n_pair = RWKV_WIDTH // LANES

    def shift(x_ref, mu, carry_ref):
        x = x_ref[rows_sl, :].astype(F32)
        rows = lax.broadcasted_iota(jnp.int32, x.shape, 0)
        prev = jnp.where(rows == 0, carry_ref[...], pltpu.roll(x, 1, axis=0))
        carry_ref[...] = x[L - 1:L, :]
        return x + (prev - x) * mu

    mu = mu_ref[...]
    r = shift(r_ref, mu[:, 0:RWKV_WIDTH], cr_ref)
    k = shift(k_ref, mu[:, RWKV_WIDTH:2 * RWKV_WIDTH], ck_ref)
    v = shift(v_ref, mu[:, 2 * RWKV_WIDTH:3 * RWKV_WIDTH], cv_ref)
    lo = shift(lo_ref, mulo_ref[...], clo_ref)

    wpre = w0_ref[...] + _mm(jnp.tanh(lo), w2_ref[...])
    lw = -jnp.exp(-_softplus(-wpre) - 0.5)
    a = _sigmoid(a0_ref[...] + _mm(lo, a2_ref[...]))
    g = _mm(_sigmoid(lo), g2_ref[...])

    ti = lax.broadcasted_iota(jnp.int32, (L, L), 0)
    tj = lax.broadcasted_iota(jnp.int32, (L, L), 1)
    tri = jnp.where(tj <= ti, 1.0, 0.0).astype(BF16)
    cl = _mm_exact_rhs(tri, lw)
    cl_last = cl[L - 1:L, :]
    e_pos = jnp.exp(cl)
    e_neg = jnp.exp(-cl)
    e_prev = jnp.exp(cl - lw)
    e_end = jnp.exp(cl_last - cl)
    p_last = jnp.exp(cl_last)

    kk = k * kk_ref[...]
    k2 = k * (1.0 + (a - 1.0) * ka_ref[...])
    f_b = a * e_neg
    f_bend = a * e_end
    rk = rk_ref[...]
    gng = gng_ref[...]
    gnb = gnb_ref[...]

    lane = lax.broadcasted_iota(jnp.int32, (L, LANES), 1)
    lo_half = lane < RWKV_HEAD
    si = lax.broadcasted_iota(jnp.int32, (H2, H2), 0)
    sj = lax.broadcasted_iota(jnp.int32, (H2, H2), 1)
    same = (si // L) == (sj // L)
    m_strict = same & (sj < si)
    m_incl = same & (sj <= si)
    eye = si == sj
    own = (si // L) == (sj // RWKV_HEAD)

    def stack(x):
        return jnp.concatenate([jnp.where(lo_half, x, 0.0), jnp.where(lo_half, 0.0, x)], axis=0)

    def rep(x):
        return jnp.concatenate([x, x], axis=0)

    pairs = range(n_pair)
    sls = [slice(p * LANES, (p + 1) * LANES) for p in pairs]
    at, bt, bend, kt, kend, rt, v_s, bonus = [], [], [], [], [], [], [], []
    for sl in sls:
        kk_s = stack(kk[:, sl])
        nrm = jnp.sqrt(jnp.sum(kk_s * kk_s, axis=-1, keepdims=True))
        kkn = kk_s / jnp.maximum(nrm, 1e-12)
        at.append(-kkn * rep(e_prev[:, sl]))
        bt.append(kkn * rep(f_b[:, sl]))
        bend.append(kkn * rep(f_bend[:, sl]))
        k2_s = stack(k2[:, sl])
        kt.append(k2_s * rep(e_neg[:, sl]))
        kend.append(k2_s * rep(e_end[:, sl]))
        r_s = stack(r[:, sl])
        rt.append(r_s * rep(e_pos[:, sl]))
        vs = stack(v[:, sl])
        v_s.append(vs)
        bonus.append(jnp.sum(r_s * k2_s * rk[:, sl], axis=-1, keepdims=True) * vs)

    gm = [_mm_nt(jnp.concatenate([at[p], rt[p]], axis=0), jnp.concatenate([bt[p], kt[p]], axis=0))
          for p in pairs]
    a_ab = [jnp.where(m_strict, gm[p][:H2, :H2], 0.0) for p in pairs]
    a_ak = [jnp.where(m_strict, gm[p][:H2, H2:], 0.0) for p in pairs]
    a_rb = [jnp.where(m_incl, gm[p][H2:, :H2], 0.0) for p in pairs]
    a_rk = [jnp.where(m_incl, gm[p][H2:, H2:], 0.0) for p in pairs]

    ident = jnp.where(eye, 1.0, 0.0)
    tinv = [ident + a_ab[p] for p in pairs]
    pw = a_ab
    akv = [_mm(a_ak[p], v_s[p]) for p in pairs]
    ark_v = [_mm(a_rk[p], v_s[p]) for p in pairs]
    kend_v = [_mm_tn(kend[p], v_s[p]) for p in pairs]
    for _ in range(max(1, (L - 1).bit_length() - 1)):
        pw = [_mm(pw[p], pw[p]) for p in pairs]
        tinv = [tinv[p] + _mm(tinv[p], pw[p]) for p in pairs]

    tx = [_mm(tinv[p], jnp.concatenate([at[p], akv[p]], axis=1)) for p in pairs]
    y2 = [_mm(a_rb[p], tx[p]) for p in pairs]
    btx = [_mm_tn(bend[p], tx[p]) for p in pairs]
    s0 = [st_ref[p] for p in pairs]
    y = [_mm(rt[p] + y2[p][:, :LANES], s0[p]) + (y2[p][:, LANES:] + ark_v[p]) for p in pairs]
    for p in pairs:
        m_mat = jnp.where(eye, jnp.broadcast_to(p_last[:, sls[p]], (H2, LANES)), 0.0) + btx[p][:, :LANES]
        st_ref[p] = _mm(m_mat, s0[p]) + (btx[p][:, LANES:] + kend_v[p])

    for p in pairs:
        sl = sls[p]
        mean = jnp.sum(y[p], axis=-1, keepdims=True) * (1.0 / RWKV_HEAD)
        dlt = jnp.where(own, y[p] - mean, 0.0)
        var = jnp.sum(dlt * dlt, axis=-1, keepdims=True) * (1.0 / RWKV_HEAD)
        yn = dlt * lax.rsqrt(var + RWKV_GN_EPS) * gng[:, sl] + jnp.where(own, gnb[:, sl], 0.0)
        ys = yn + bonus[p]
        o_ref[rows_sl, sl] = ((ys[:L] + ys[L:]) * g[:, sl]).astype(o_ref.dtype)


def _rwkv_kernel(*refs):
    st_ref, cr_ref, ck_ref, cv_ref, clo_ref = refs[-5:]

    @pl.when(pl.program_id(1) == 0)
    def _():
        for ref in (st_ref, cr_ref, ck_ref, cv_ref, clo_ref):
            ref[...] = jnp.zeros_like(ref)

    for c in range(RWKV_STEP_CHUNKS):
        _rwkv_chunk(slice(c * RWKV_CHUNK, (c + 1) * RWKV_CHUNK), *refs)


def _rwkv(proj, bsz, seq, mu_rkv, mu_lo, w0, a0, k_k, k_a, r_k, gn_g, gn_b, w2p, a2p, g2p):
    L = RWKV_CHUNK * RWKV_STEP_CHUNKS
    nc = seq // L
    W = RWKV_WIDTH
    row = lambda n: pl.BlockSpec((1, n), lambda b, c: (0, 0))
    full = lambda a: pl.BlockSpec(a.shape, lambda b, c: (0, 0))
    lo_blk = 3 * W // LORA_PAD
    v1 = lambda a: a.reshape(1, -1)
    return pl.pallas_call(
        _rwkv_kernel,
        grid=(bsz, nc),
        in_specs=[pl.BlockSpec((L, W), lambda b, c: (b * nc + c, 0)),
                  pl.BlockSpec((L, W), lambda b, c: (b * nc + c, 1)),
                  pl.BlockSpec((L, W), lambda b, c: (b * nc + c, 2)),
                  pl.BlockSpec((L, LORA_PAD), lambda b, c: (b * nc + c, lo_blk)),
                  row(3 * W), row(LORA_PAD), row(W), row(W), row(W), row(W), row(W), row(W), row(W),
                  full(w2p), full(a2p), full(g2p)],
        out_specs=pl.BlockSpec((L, W), lambda b, c: (b * nc + c, 0)),
        out_shape=jax.ShapeDtypeStruct((bsz * seq, W), BF16),
        scratch_shapes=[pltpu.VMEM((W // LANES, LANES, LANES), F32),
                        pltpu.VMEM((1, W), F32), pltpu.VMEM((1, W), F32), pltpu.VMEM((1, W), F32),
                        pltpu.VMEM((1, LORA_PAD), F32)],
        compiler_params=_cp(("arbitrary", "arbitrary")),
        name="rwkv7",
    )(proj, proj, proj, proj, v1(mu_rkv), v1(mu_lo), v1(w0), v1(a0), v1(k_k), v1(k_a), v1(r_k), v1(gn_g),
      v1(gn_b), w2p, a2p, g2p)


def _hgrn_kernel(qa_ref, qb_ref, fa_ref, fb_ref, ia_ref, ib_ref, oa_ref, ob_ref, lb_ref, gn_ref, o_ref, st_ref):
    n = HGRN_STEP
    wide = lambda a_ref, b_ref: jnp.concatenate([a_ref[...], b_ref[...]], axis=1).astype(F32)
    blk = GLA_BLOCK
    nb = n // blk
    n_head = HGRN_WIDTH // HGRN_HEAD

    @pl.when(pl.program_id(1) == 0)
    def _():
        st_ref[...] = jnp.zeros_like(st_ref)

    lb = lb_ref[...]
    qf = wide(qa_ref, qb_ref)
    q = qf * _sigmoid(qf)
    forget = lb + (1.0 - lb) * _sigmoid(wide(fa_ref, fb_ref))
    log_f = jnp.log(forget)
    kx = 1.0 - forget
    vv = wide(ia_ref, ib_ref)
    og = wide(oa_ref, ob_ref)

    ti = lax.broadcasted_iota(jnp.int32, (n, n), 0)
    tj = lax.broadcasted_iota(jnp.int32, (n, n), 1)
    same = (ti // blk) == (tj // blk)
    causal = same & (tj <= ti)
    b = _mm_exact_rhs(jnp.where(causal, 1.0, 0.0).astype(BF16), log_f)
    b_last = _mm_exact_rhs(jnp.where(same, 1.0, 0.0).astype(BF16), log_f)
    q_dec = q * jnp.exp(b)
    k_dec = kx * jnp.exp(-b)
    k_end = kx * jnp.exp(b_last - b)
    d_end = jnp.exp(b_last)
    rows = lax.broadcasted_iota(jnp.int32, (n, HGRN_HEAD), 0)
    gn = gn_ref[...]

    heads = range(n_head)
    sls = [slice(h * HGRN_HEAD, (h + 1) * HGRN_HEAD) for h in heads]
    attn = [jnp.where(causal, _mm_nt(q_dec[:, sl], k_dec[:, sl]), 0.0) for sl in sls]
    o_intra = [_mm(attn[h], vv[:, sls[h]]) for h in heads]
    vt = [vv[:, sl].T for sl in sls]
    blk_mask = [(rows // blk) == c for c in range(nb)]
    kv_all = [_mm(vt[h], jnp.concatenate([jnp.where(blk_mask[c], k_end[:, sls[h]], 0.0) for c in range(nb)],
                                         axis=1)) for h in heads]
    kv_t = [[kv_all[h][:, c * HGRN_HEAD:(c + 1) * HGRN_HEAD] for c in range(nb)] for h in heads]
    st = [[st_ref[h]] for h in heads]
    for h in heads:
        for c in range(nb):
            st[h].append(st[h][c] * d_end[c * blk:c * blk + 1, sls[h]] + kv_t[h][c])
        st_ref[h] = st[h][nb]
    for h in heads:
        sl = sls[h]
        inter = [_mm_nt(q_dec[c * blk:(c + 1) * blk, sl], st[h][c]) for c in range(nb)]
        o = o_intra[h] + jnp.concatenate(inter, axis=0)
        o = _rms(o, NORM_EPS) * gn
        ogh = og[:, sl]
        o_ref[:, sl] = (o * (ogh * _sigmoid(ogh))).astype(o_ref.dtype)


def _hgrn(proj, bsz, seq, lb, gn_g):
    n = HGRN_STEP
    ns = seq // n
    W = HGRN_WIDTH
    half = W // 2
    first = (3 * RWKV_WIDTH + LORA_PAD) // half
    col = lambda j: pl.BlockSpec((n, half), lambda b, c: (b * ns + c, first + j))
    return pl.pallas_call(
        _hgrn_kernel,
        grid=(bsz, ns),
        in_specs=[col(j) for j in range(8)] + [
                  pl.BlockSpec((1, W), lambda b, c: (0, 0)),
                  pl.BlockSpec((1, HGRN_HEAD), lambda b, c: (0, 0))],
        out_specs=pl.BlockSpec((n, W), lambda b, c: (b * ns + c, 0)),
        out_shape=jax.ShapeDtypeStruct((bsz * seq, W), BF16),
        scratch_shapes=[pltpu.VMEM((W // HGRN_HEAD, HGRN_HEAD, HGRN_HEAD), F32)],
        compiler_params=_cp(("arbitrary", "arbitrary")),
        name="hgrn2",
    )(*([proj] * 8), lb.reshape(1, W), gn_g.reshape(1, HGRN_HEAD))


def _mixout_kernel(ya_ref, yb_ref, *refs):
    ga_refs, gb_refs = refs[:MIX_GATE_BLOCKS], refs[MIX_GATE_BLOCKS:2 * MIX_GATE_BLOCKS]
    pa_ref, pb_ref, wo_ref, x_ref, g_ref, o_ref = refs[2 * MIX_GATE_BLOCKS:]
    wc = ga_refs[0].shape[1]
    ya = ya_ref[...]
    yb = yb_ref[...]
    mix = None
    for c, (ga_ref, gb_ref) in enumerate(zip(ga_refs, gb_refs)):
        cs = slice(c * wc, (c + 1) * wc)
        ma = jnp.dot(ya, pa_ref[:, cs], preferred_element_type=F32)
        mb = jnp.dot(yb, pb_ref[:, cs], preferred_element_type=F32)
        part = (_sigmoid(ga_ref[...].astype(F32)) * ma + _sigmoid(gb_ref[...].astype(F32)) * mb).astype(BF16)
        term = jnp.dot(part, wo_ref[cs, :], preferred_element_type=F32)
        mix = term if mix is None else mix + term
    o_ref[...] = x_ref[...] + g_ref[0] * mix


def _mixout(ya, yb, proj, pa, pb, wo, xf, g1, seq):
    t, w = ya.shape
    d = pa.shape[1]
    tm = min(MIX_TM, seq)
    per_b = seq // tm
    wc = d // MIX_GATE_BLOCKS
    ga0 = (3 * RWKV_WIDTH + LORA_PAD + 4 * HGRN_WIDTH) // wc
    gate = lambda j: pl.BlockSpec((tm, wc), lambda i: (i, ga0 + j))
    const = lambda a: pl.BlockSpec(a.shape, lambda i: (0, 0), pipeline_mode=pl.Buffered(1))
    return pl.pallas_call(
        _mixout_kernel,
        grid=(t // tm,),
        in_specs=[pl.BlockSpec((tm, w), lambda i: (i, 0)),
                  pl.BlockSpec((tm, w), lambda i: (i, 0))] + [gate(j) for j in range(2 * MIX_GATE_BLOCKS)] + [
                  const(pa), const(pb), const(wo),
                  pl.BlockSpec((tm, d), lambda i: (i, 0)),
                  pl.BlockSpec((1, 1, d), lambda i: (i // per_b, 0, 0))],
        out_specs=pl.BlockSpec((tm, d), lambda i: (i, 0)),
        out_shape=jax.ShapeDtypeStruct((t, d), F32),
        compiler_params=_cp(("arbitrary",)),
        name="mixout",
    )(ya, yb, *([proj] * (2 * MIX_GATE_BLOCKS)), pa, pb, wo, xf, g1)


def _route_kernel(x_ref, g_ref, sc_ref, sh_ref, rwh_ref, rwm_ref, rb_ref, h_ref, idx_ref, gate_ref, rank_ref,
                  cnt_ref, carry_ref):
    tb = x_ref.shape[0]

    @pl.when(pl.program_id(0) == 0)
    def _():
        carry_ref[...] = jnp.zeros_like(carry_ref)

    h = (_rms(x_ref[...], NORM_EPS) * g_ref[...]) * (1.0 + sc_ref[0]) + sh_ref[0]
    h_ref[...] = h
    h_hi, h_mid, _ = _split3(h)
    d = lambda a, b: jnp.dot(a, b, preferred_element_type=F32)
    logits = d(h_hi, rwh_ref[...]) + (d(h_hi, rwm_ref[...]) + d(h_mid, rwh_ref[...])) + rb_ref[...]

    lane = lax.broadcasted_iota(jnp.int32, logits.shape, 1).astype(F32)
    work = logits
    vals, idxs, sels = [], [], []
    for _ in range(TOP_K):
        m = jnp.max(work, axis=-1, keepdims=True)
        idx = jnp.min(jnp.where(work == m, lane, float(N_EXPERTS)), axis=-1, keepdims=True)
        sel = lane == idx
        vals.append(m)
        idxs.append(idx)
        sels.append(sel)
        work = jnp.where(sel, -jnp.inf, work)
    exps = [jnp.exp(v - vals[0]) for v in vals]
    den = exps[0] + exps[1] + exps[2] + exps[3]

    onehot = jnp.zeros(logits.shape, F32)
    for sel in sels:
        onehot = onehot + jnp.where(sel, 1.0, 0.0)
    ti = lax.broadcasted_iota(jnp.int32, (tb, tb), 0)
    tj = lax.broadcasted_iota(jnp.int32, (tb, tb), 1)
    below = jnp.where(tj < ti, 1.0, 0.0).astype(BF16)
    before = jnp.dot(below, onehot.astype(BF16), preferred_element_type=F32) + carry_ref[...]
    carry_ref[...] = carry_ref[...] + jnp.sum(onehot, axis=0, keepdims=True)
    cnt_ref[...] = carry_ref[...]

    out_lane = lax.broadcasted_iota(jnp.int32, (tb, LANES), 1)
    idx_out = jnp.zeros((tb, LANES), jnp.int32)
    rank_out = jnp.zeros((tb, LANES), jnp.int32)
    gate_out = jnp.zeros((tb, LANES), F32)
    for j in range(TOP_K):
        rank = jnp.sum(jnp.where(sels[j], before, 0.0), axis=-1, keepdims=True)
        idx_out = jnp.where(out_lane == j, idxs[j].astype(jnp.int32), idx_out)
        rank_out = jnp.where(out_lane == j, rank.astype(jnp.int32), rank_out)
        gate_out = jnp.where(out_lane == j, exps[j] / den, gate_out)
    idx_ref[...] = idx_out
    rank_ref[...] = rank_out
    gate_ref[...] = gate_out


def _route(x1, g, sc, sh, rw, rb, seq):
    t, d = x1.shape
    tb = min(ROUTE_TB, seq)
    per_b = seq // tb
    e = rw.shape[1]
    rw_hi = rw.astype(BF16)
    rw_mid = (rw - rw_hi.astype(F32)).astype(BF16)
    wide = lambda dt: jax.ShapeDtypeStruct((t, LANES), dt)
    return pl.pallas_call(
        _route_kernel,
        grid=(t // tb,),
        in_specs=[pl.BlockSpec((tb, d), lambda i: (i, 0)),
                  pl.BlockSpec((1, d), lambda i: (0, 0)),
                  pl.BlockSpec((1, 1, d), lambda i: (i // per_b, 0, 0)),
                  pl.BlockSpec((1, 1, d), lambda i: (i // per_b, 0, 0)),
                  pl.BlockSpec((d, e), lambda i: (0, 0)),
                  pl.BlockSpec((d, e), lambda i: (0, 0)),
                  pl.BlockSpec((1, e), lambda i: (0, 0))],
        out_specs=[pl.BlockSpec((tb, d), lambda i: (i, 0)),
                   pl.BlockSpec((tb, LANES), lambda i: (i, 0)),
                   pl.BlockSpec((tb, LANES), lambda i: (i, 0)),
                   pl.BlockSpec((tb, LANES), lambda i: (i, 0)),
                   pl.BlockSpec((1, e), lambda i: (0, 0))],
        out_shape=[jax.ShapeDtypeStruct((t, d), F32), wide(jnp.int32), wide(F32), wide(jnp.int32),
                   jax.ShapeDtypeStruct((1, e), F32)],
        scratch_shapes=[pltpu.VMEM((1, e), F32)],
        compiler_params=_cp(("arbitrary",)),
        name="route",
    )(x1, g.reshape(1, d), sc, sh, rw_hi, rw_mid, rb.reshape(1, e))


def _scatter_kernel(dest_ref, zflag_ref, h_ref, xs_ref, zbuf, sem, zsem):
    tb = h_ref.shape[0]
    zrows = zbuf.shape[0]
    base = pl.program_id(0) * tb

    @pl.when(pl.program_id(0) == 0)
    def _():
        zbuf[...] = jnp.zeros_like(zbuf)

        def zcopy(b):
            return pltpu.make_async_copy(zbuf, xs_ref.at[pl.ds(pl.multiple_of(b * zrows, zrows), zrows), :], zsem)

        def zstart(b, carry):
            @pl.when(zflag_ref[b] != 0)
            def _():
                zcopy(b).start()
            return carry

        def zwait(b, carry):
            @pl.when(zflag_ref[b] != 0)
            def _():
                zcopy(b).wait()
            return carry

        n_blk = xs_ref.shape[0] // zrows
        lax.fori_loop(0, n_blk, zstart, 0)
        lax.fori_loop(0, n_blk, zwait, 0)

    def copy(t, j):
        d = dest_ref[(base + t) * TOP_K + j]
        return pltpu.make_async_copy(h_ref.at[pl.ds(t, 1), :], xs_ref.at[pl.ds(d, 1), :], sem)

    def start(t, carry):
        for j in range(TOP_K):
            copy(t, j).start()
        return carry

    lax.fori_loop(0, tb, start, 0, unroll=ROW_DMA_UNROLL)
    for j in range(TOP_K):
        pltpu.make_async_copy(h_ref, xs_ref.at[pl.ds(0, tb), :], sem).wait()


def _scatter_rows(dest_flat, zero_flag, h2, n_rows):
    t, d = h2.shape
    tb = min(SCATTER_TB, t)
    return pl.pallas_call(
        _scatter_kernel,
        grid_spec=pltpu.PrefetchScalarGridSpec(
            num_scalar_prefetch=2,
            grid=(t // tb,),
            in_specs=[pl.BlockSpec((tb, d), lambda i, dest, zf: (i, 0))],
            out_specs=pl.BlockSpec(memory_space=pl.ANY),
            scratch_shapes=[pltpu.VMEM((MOE_TM, d), F32), pltpu.SemaphoreType.DMA(()),
                            pltpu.SemaphoreType.DMA(())],
        ),
        out_shape=jax.ShapeDtypeStruct((n_rows, d), F32),
        compiler_params=_cp(("arbitrary",)),
        name="scatter_rows",
    )(dest_flat, zero_flag, h2)


def _expert_kernel(be_ref, bs_ref, nu_ref, x_ref, bgu_ref, bd_ref, wgu_hbm, wd_hbm, o_ref,
                   xb_ref, act_ref, wbuf, sem):
    del bs_ref
    i = pl.program_id(0)
    nt, _, tn = act_ref.shape
    n_slot, de, tnb = wbuf.shape
    ntb = o_ref.shape[1] // tnb
    n_steps = nt + ntb
    look = n_slot - 1
    live = i < nu_ref[0]

    def tile_copies(blk, step, slot):
        e = be_ref[blk]
        if step < nt:
            return [pltpu.make_async_copy(wgu_hbm.at[e, :, half * de + step * tn:half * de + (step + 1) * tn],
                                          wbuf.at[slot, :, half * tn:(half + 1) * tn], sem.at[slot])
                    for half in range(2)]
        colb = (step - nt) * tnb
        return [pltpu.make_async_copy(wd_hbm.at[e, :, colb:colb + tnb], wbuf.at[slot], sem.at[slot])]

    def request(blk, sub):
        blk = blk + sub // n_steps
        step = sub % n_steps

        @pl.when(blk < nu_ref[0])
        def _():
            for cp in tile_copies(blk, step, (blk * n_steps + step) % n_slot):
                cp.start()

    @pl.when(i == 0)
    def _():
        for sub in range(look):
            request(i, sub)

    @pl.when(live)
    def _():
        xb_ref[...] = x_ref[...].astype(BF16)
        for n in range(n_steps):
            request(i, n + look)
            slot = (i * n_steps + n) % n_slot
            for cp in tile_copies(i, n, slot):
                cp.wait()
            if n < nt:
                xb = xb_ref[...]
                w = wbuf[slot]
                glu = jnp.dot(xb, w[:, 0:tn].astype(BF16), preferred_element_type=F32) + bgu_ref[0, n]
                lin = jnp.dot(xb, w[:, tn:2 * tn].astype(BF16), preferred_element_type=F32) + bgu_ref[0, n + nt]
                glu = jnp.minimum(glu, SWIGLU_LIMIT)
                lin = jnp.clip(lin, -SWIGLU_LIMIT, SWIGLU_LIMIT)
                act_ref[n] = (glu * _sigmoid(SWIGLU_ALPHA * glu) * (lin + 1.0)).astype(BF16)
            else:
                m = n - nt
                wd = wbuf[slot]
                y = jnp.dot(act_ref[0], wd[0:tn].astype(BF16), preferred_element_type=F32)
                for k in range(1, nt):
                    y = y + jnp.dot(act_ref[k], wd[k * tn:(k + 1) * tn].astype(BF16), preferred_element_type=F32)
                o_ref[:, m * tnb:(m + 1) * tnb] = y + bd_ref[0, m]

    @pl.when(jnp.logical_not(live))
    def _():
        o_ref[...] = jnp.zeros_like(o_ref)


def _experts(blk_expert, blk_src, n_used, xs, w_gu, b_gu, w_down, b_down):
    n_rows, d = xs.shape
    tm, tn, tnb = MOE_TM, MOE_TN, MOE_TNB
    nb = n_rows // tm
    e, _, two_de = w_gu.shape
    de = two_de // 2
    nt = de // tn
    ntb = d // tnb
    assert 2 * tn == tnb and de == d
    b_gu4 = b_gu.reshape(e, 2 * nt, 1, tn)
    b_d4 = b_down.reshape(e, ntb, 1, tnb)
    return pl.pallas_call(
        _expert_kernel,
        grid_spec=pltpu.PrefetchScalarGridSpec(
            num_scalar_prefetch=3,
            grid=(nb,),
            in_specs=[pl.BlockSpec((tm, d), lambda i, be, bs, nu: (bs[i], 0)),
                      pl.BlockSpec((1, 2 * nt, 1, tn), lambda i, be, bs, nu: (be[i], 0, 0, 0)),
                      pl.BlockSpec((1, ntb, 1, tnb), lambda i, be, bs, nu: (be[i], 0, 0, 0)),
                      pl.BlockSpec(memory_space=pl.ANY),
                      pl.BlockSpec(memory_space=pl.ANY)],
            out_specs=pl.BlockSpec((tm, d), lambda i, be, bs, nu: (i, 0)),
            scratch_shapes=[pltpu.VMEM((tm, d), BF16), pltpu.VMEM((nt, tm, tn), BF16),
                            pltpu.VMEM((MOE_RING, de, tnb), F32), pltpu.SemaphoreType.DMA((MOE_RING,))],
        ),
        out_shape=jax.ShapeDtypeStruct((n_rows, d), F32),
        compiler_params=_cp(("arbitrary",)),
        name="experts",
    )(blk_expert, blk_src, n_used, xs, b_gu4, b_d4, w_gu, w_down)


def _combine_kernel(dest_ref, ys_ref, gate_ref, x_ref, g2_ref, fg_ref, o_ref, buf_ref, sem):
    tb = x_ref.shape[0]
    i = pl.program_id(0)
    half = i % 2

    def gather(blk, slot):
        def start(t, carry):
            for j in range(TOP_K):
                d = dest_ref[(blk * tb + t) * TOP_K + j]
                pltpu.make_async_copy(ys_ref.at[pl.ds(d, 1), :], buf_ref.at[slot, j, pl.ds(t, 1), :],
                                      sem.at[slot]).start()
            return carry
        lax.fori_loop(0, tb, start, 0, unroll=ROW_DMA_UNROLL)

    @pl.when(i == 0)
    def _():
        gather(i, half)

    @pl.when(i + 1 < pl.num_programs(0))
    def _():
        gather(i + 1, 1 - half)

    for j in range(TOP_K):
        pltpu.make_async_copy(ys_ref.at[pl.ds(0, tb), :], buf_ref.at[half, j], sem.at[half]).wait()

    gate = gate_ref[...]
    ffn = gate[:, 0:1] * buf_ref[half, 0]
    for j in range(1, TOP_K):
        ffn = ffn + gate[:, j:j + 1] * buf_ref[half, j]
    x2 = x_ref[...] + g2_ref[0] * ffn
    o_ref[...] = _rms(x2, NORM_EPS) * fg_ref[...]


def _combine(dest_flat, ys, gates, x1, g2, fg, seq):
    t, d = x1.shape
    tb = min(COMBINE_TB, seq)
    per_b = seq // tb
    return pl.pallas_call(
        _combine_kernel,
        grid_spec=pltpu.PrefetchScalarGridSpec(
            num_scalar_prefetch=1,
            grid=(t // tb,),
            in_specs=[pl.BlockSpec(memory_space=pl.ANY),
                      pl.BlockSpec((tb, LANES), lambda i, dest: (i, 0)),
                      pl.BlockSpec((tb, d), lambda i, dest: (i, 0)),
                      pl.BlockSpec((1, 1, d), lambda i, dest: (i // per_b, 0, 0)),
                      pl.BlockSpec((1, d), lambda i, dest: (0, 0))],
            out_specs=pl.BlockSpec((tb, d), lambda i, dest: (i, 0)),
            scratch_shapes=[pltpu.VMEM((2, TOP_K, tb, d), F32), pltpu.SemaphoreType.DMA((2,))],
        ),
        out_shape=jax.ShapeDtypeStruct((t, d), F32),
        compiler_params=_cp(("arbitrary",)),
        name="combine",
    )(dest_flat, ys, gates, x1, g2, fg.reshape(1, d))


def _moe(x1, g, sc2, sh2, g2, router_w, router_b, w_gu, b_gu, w_down, b_down, final_g, seq):
    t, d = x1.shape
    a = t * TOP_K
    tm = MOE_TM
    nb = -(-a // tm) + N_EXPERTS
    h2, idx_w, gate_w, rank_w, counts = _route(x1, g, sc2, sh2, router_w, router_b, seq)
    idx = idx_w[:, :TOP_K]
    rank = rank_w[:, :TOP_K]
    cnt = counts.reshape(-1).astype(jnp.int32)
    padded = ((cnt + tm - 1) // tm) * tm
    pad_end = jnp.cumsum(padded)
    pad_start = pad_end - padded
    experts = jnp.arange(N_EXPERTS, dtype=jnp.int32)
    start_of = jnp.sum(jnp.where(idx[:, :, None] == experts, pad_start, 0), axis=-1)
    dest = (start_of + rank).reshape(a).astype(jnp.int32)
    n_used = (pad_end[-1] // tm).astype(jnp.int32)
    blk = jnp.arange(nb, dtype=jnp.int32)
    blk_src = jnp.clip(blk, 0, jnp.maximum(n_used - 1, 0))
    blk_expert = jnp.sum((pad_end[None, :] <= (blk_src * tm)[:, None]).astype(jnp.int32), axis=1)
    blk_expert = jnp.minimum(blk_expert, N_EXPERTS - 1)
    last_blk = jnp.where(padded > 0, pad_end // tm - 1, -1)
    zero_flag = ((blk >= n_used) | jnp.any(last_blk[None, :] == blk[:, None], axis=1)).astype(jnp.int32)
    xs = _scatter_rows(dest, zero_flag, h2, nb * tm)
    ys = _experts(blk_expert, blk_src, n_used.reshape(1), xs, w_gu, b_gu, w_down, b_down)
    return _combine(dest, ys, gate_w, x1, g2, final_g, seq)


def kernel(x, c, ada_w, ada_b, norm1_g, norm2_g, w_in, rwkv_mu, rwkv_w0, rwkv_w2, rwkv_a0, rwkv_a2, rwkv_g2,
           rwkv_k_k, rwkv_k_a, rwkv_r_k, rwkv_gn_g, rwkv_gn_b, hgrn_lb_logits, hgrn_gn_g, proj_a, proj_b, w_out,
           router_w, router_b, exp_w_gate_up, exp_b_gate_up, exp_w_down, exp_b_down, final_norm_g):
    bsz, seq, d = x.shape
    depth = ada_w.shape[0]
    lower_bounds = jnp.cumsum(jax.nn.softmax(hgrn_lb_logits.astype(F32), axis=0), axis=0)
    xf = x.reshape(bsz * seq, d)
    rkv = 3 * RWKV_WIDTH
    rwkv_cols = rkv + LORA_COLS
    for l in range(depth):
        mod = _adaln(c, ada_w[l], ada_b[l])
        sh1, sc1, g1, sh2, sc2, g2 = [m.reshape(bsz, 1, d) for m in jnp.split(mod, 6, axis=-1)]

        wl = w_in[l]
        w_a = wl[:, :rkv + LORA_PAD].astype(BF16)
        w_b = wl[:, rwkv_cols:].astype(BF16)
        proj = _inproj(xf, norm1_g[l], sc1, sh1, w_a, w_b, seq)

        mu = rwkv_mu[l]
        mu_lo = jnp.pad(mu[rkv:], (0, LORA_PAD - LORA_COLS))
        pad_rows = lambda w, r0: jnp.zeros((LORA_PAD, RWKV_WIDTH), F32).at[r0:r0 + w.shape[0]].set(w).astype(BF16)
        ya = _rwkv(proj, bsz, seq, mu[:rkv], mu_lo, rwkv_w0[l], rwkv_a0[l], rwkv_k_k[l], rwkv_k_a[l],
                   rwkv_r_k[l].reshape(-1), rwkv_gn_g[l], rwkv_gn_b[l],
                   pad_rows(rwkv_w2[l], 0), pad_rows(rwkv_a2[l], DECAY_LORA),
                   pad_rows(rwkv_g2[l], DECAY_LORA + AAA_LORA))
        yb = _hgrn(proj, bsz, seq, lower_bounds[l], hgrn_gn_g[l])
        x1 = _mixout(ya, yb, proj, proj_a[l].astype(BF16), proj_b[l].astype(BF16), w_out[l].astype(BF16), xf, g1, seq)
        assert depth == 1
        out = _moe(x1, norm2_g[l], sc2, sh2, g2, router_w[l], router_b[l], exp_w_gate_up[l], exp_b_gate_up[l],
                   exp_w_down[l], exp_b_down[l], final_norm_g, seq)
    return out.reshape(bsz, seq, d)
```

```python
import functools

import jax
import jax.numpy as jnp
from jax import lax
from jax.experimental import pallas as pl
from jax.experimental.pallas import tpu as pltpu

F32 = jnp.float32
BF16 = jnp.bfloat16

D_MODEL = 2048
RWKV_WIDTH = 1024
RWKV_HEAD = 64
DECAY_LORA = 64
AAA_LORA = 64
GATE_LORA = 160
LORA_COLS = DECAY_LORA + AAA_LORA + GATE_LORA
LORA_PAD = 512
RWKV_GN_EPS = 64e-5
HGRN_WIDTH = 1024
HGRN_HEAD = 128
GLA_BLOCK = 16
N_EXPERTS = 32
TOP_K = 4
D_EXPERT = 2048
SWIGLU_ALPHA = 1.702
SWIGLU_LIMIT = 7.0
NORM_EPS = 1e-5

LANES = 128
VMEM_LIMIT = 56 * 1024 * 1024

RWKV_CHUNK = 64
RWKV_STEP_CHUNKS = 2
HGRN_STEP = 128
PROJ_TM = 1024
PROJ_TN = 512
PROJ_RING = 4
NORM_ROWS = 128
MIX_TM = 256
MIX_GATE_BLOCKS = 4
MOE_TM = 1024
MOE_TN = 256
MOE_TNB = 512
MOE_RING = 4
ROUTE_TB = 256
SCATTER_TB = 256
COMBINE_TB = 128
ROW_DMA_UNROLL = 8


def _cp(sem, vmem=VMEM_LIMIT):
    return pltpu.CompilerParams(dimension_semantics=sem, vmem_limit_bytes=vmem)


def _mm(a, b):
    return jnp.dot(a.astype(BF16), b.astype(BF16), preferred_element_type=F32)


def _mm_nt(a, b):
    return lax.dot_general(a.astype(BF16), b.astype(BF16), (((1,), (1,)), ((), ())),
                           preferred_element_type=F32)


def _mm_tn(a, b):
    return lax.dot_general(a.astype(BF16), b.astype(BF16), (((0,), (0,)), ((), ())),
                           preferred_element_type=F32)


def _split3(x):
    hi = x.astype(BF16)
    r1 = x - hi.astype(F32)
    mid = r1.astype(BF16)
    lo = (r1 - mid.astype(F32)).astype(BF16)
    return hi, mid, lo


def _mm_exact_rhs(m_bf16, x):
    hi, mid, lo = _split3(x)
    d = lambda p: jnp.dot(m_bf16, p, preferred_element_type=F32)
    return d(hi) + d(mid) + d(lo)


def _sigmoid(x):
    return 1.0 / (1.0 + jnp.exp(-x))


def _softplus(x):
    return jnp.maximum(x, 0.0) + jnp.log(1.0 + jnp.exp(-jnp.abs(x)))


def _rms(x, eps):
    return x * lax.rsqrt(jnp.mean(x * x, axis=-1, keepdims=True) + eps)


def _adaln_kernel(c_ref, w_ref, b_ref, o_ref):
    c = c_ref[...]
    c_act = c * _sigmoid(c)
    o_ref[...] = jnp.dot(c_act, w_ref[...], preferred_element_type=F32,
                         precision=lax.Precision.HIGHEST) + b_ref[...]


def _adaln(c, w, b):
    bsz, d = c.shape
    n = w.shape[1]
    tn = 1024
    return pl.pallas_call(
        _adaln_kernel,
        grid=(n // tn,),
        in_specs=[pl.BlockSpec((bsz, d), lambda j: (0, 0)),
                  pl.BlockSpec((d, tn), lambda j: (0, j)),
                  pl.BlockSpec((1, tn), lambda j: (0, j))],
        out_specs=pl.BlockSpec((bsz, tn), lambda j: (0, j)),
        out_shape=jax.ShapeDtypeStruct((bsz, n), F32),
        compiler_params=_cp(("arbitrary",)),
        name="adaln",
    )(c, w, b.reshape(1, n))


def _inproj_kernel(n_a, x_ref, g_ref, sc_ref, sh_ref, wa_hbm, wb_hbm, o_hbm, h_ref, wbuf, obuf, wsem, osem):
    i = pl.program_id(0)
    tm = x_ref.shape[0]
    n_slot, _, tn = wbuf.shape
    look = n_slot - 1
    n_tiles = o_hbm.shape[1] // tn

    def w_copy(tile, slot):
        src = (wa_hbm.at[:, tile * tn:(tile + 1) * tn] if tile < n_a
               else wb_hbm.at[:, (tile - n_a) * tn:(tile - n_a + 1) * tn])
        return pltpu.make_async_copy(src, wbuf.at[slot], wsem.at[slot])

    def o_copy(tile):
        rows = pl.ds(pl.multiple_of(i * tm, tm), tm)
        return pltpu.make_async_copy(obuf.at[tile % 2], o_hbm.at[rows, tile * tn:(tile + 1) * tn], osem.at[tile % 2])

    def request(sub):
        blk = i + sub // n_tiles
        tile = sub % n_tiles

        @pl.when(blk < pl.num_programs(0))
        def _():
            w_copy(tile, (blk * n_tiles + tile) % n_slot).start()

    @pl.when(i == 0)
    def _():
        for sub in range(look):
            request(sub)

    def rows(c, carry):
        rs = pl.ds(pl.multiple_of(c * NORM_ROWS, NORM_ROWS), NORM_ROWS)
        y = _rms(x_ref[rs, :], NORM_EPS) * g_ref[...]
        h_ref[rs, :] = (y * (1.0 + sc_ref[0]) + sh_ref[0]).astype(BF16)
        return carry
    lax.fori_loop(0, tm // NORM_ROWS, rows, 0)

    for tile in range(n_tiles):
        request(tile + look)
        slot = (i * n_tiles + tile) % n_slot
        w_copy(tile, slot).wait()
        if tile >= 2:
            o_copy(tile - 2).wait()
        obuf[tile % 2] = jnp.dot(h_ref[...], wbuf[slot], preferred_element_type=F32).astype(obuf.dtype)
        o_copy(tile).start()
    for tile in range(max(n_tiles - 2, 0), n_tiles):
        o_copy(tile).wait()


def _inproj(xf, g, sc, sh, w_a, w_b, seq):
    t, d = xf.shape
    tm = min(PROJ_TM, seq)
    tn = PROJ_TN
    n_a, n_b = w_a.shape[1] // tn, w_b.shape[1] // tn
    per_b = seq // tm
    return pl.pallas_call(
        functools.partial(_inproj_kernel, n_a),
        grid=(t // tm,),
        in_specs=[pl.BlockSpec((tm, d), lambda i: (i, 0)),
                  pl.BlockSpec((1, d), lambda i: (0, 0)),
                  pl.BlockSpec((1, 1, d), lambda i: (i // per_b, 0, 0)),
                  pl.BlockSpec((1, 1, d), lambda i: (i // per_b, 0, 0)),
                  pl.BlockSpec(memory_space=pl.ANY),
                  pl.BlockSpec(memory_space=pl.ANY)],
        out_specs=pl.BlockSpec(memory_space=pl.ANY),
        out_shape=jax.ShapeDtypeStruct((t, (n_a + n_b) * tn), BF16),
        scratch_shapes=[pltpu.VMEM((tm, d), BF16), pltpu.VMEM((PROJ_RING, d, tn), BF16),
                        pltpu.VMEM((2, tm, tn), BF16), pltpu.SemaphoreType.DMA((PROJ_RING,)),
                        pltpu.SemaphoreType.DMA((2,))],
        compiler_params=_cp(("arbitrary",)),
        name="inproj",
    )(xf, g.reshape(1, d), sc, sh, w_a, w_b)


def _rwkv_chunk(rows_sl, r_ref, k_ref, v_ref, lo_ref, mu_ref, mulo_ref, w0_ref, a0_ref, kk_ref, ka_ref, rk_ref,
                gng_ref, gnb_ref, w2_ref, a2_ref, g2_ref, o_ref,
                st_ref, cr_ref, ck_ref, cv_ref, clo_ref):
    L = RWKV_CHUNK
    H2 = 2 * L
    n_pair = RWKV_WIDTH // LANES

    def shift(x_ref, mu, carry_ref):
        x = x_ref[rows_sl, :].astype(F32)
        rows = lax.broadcasted_iota(jnp.int32, x.shape, 0)
        prev = jnp.where(rows == 0, carry_ref[...], pltpu.roll(x, 1, axis=0))
        carry_ref[...] = x[L - 1:L, :]
        return x + (prev - x) * mu

    mu = mu_ref[...]
    r = shift(r_ref, mu[:, 0:RWKV_WIDTH], cr_ref)
    k = shift(k_ref, mu[:, RWKV_WIDTH:2 * RWKV_WIDTH], ck_ref)
    v = shift(v_ref, mu[:, 2 * RWKV_WIDTH:3 * RWKV_WIDTH], cv_ref)
    lo = shift(lo_ref, mulo_ref[...], clo_ref)

    wpre = w0_ref[...] + _mm(jnp.tanh(lo), w2_ref[...])
    lw = -jnp.exp(-_softplus(-wpre) - 0.5)
    a = _sigmoid(a0_ref[...] + _mm(lo, a2_ref[...]))
    g = _mm(_sigmoid(lo), g2_ref[...])

    ti = lax.broadcasted_iota(jnp.int32, (L, L), 0)
    tj = lax.broadcasted_iota(jnp.int32, (L, L), 1)
    tri = jnp.where(tj <= ti, 1.0, 0.0).astype(BF16)
    cl = _mm_exact_rhs(tri, lw)
    cl_last = cl[L - 1:L, :]
    e_pos = jnp.exp(cl)
    e_neg = jnp.exp(-cl)
    e_prev = jnp.exp(cl - lw)
    e_end = jnp.exp(cl_last - cl)
    p_last = jnp.exp(cl_last)

    kk = k * kk_ref[...]
    k2 = k * (1.0 + (a - 1.0) * ka_ref[...])
    f_b = a * e_neg
    f_bend = a * e_end
    rk = rk_ref[...]
    gng = gng_ref[...]
    gnb = gnb_ref[...]

    lane = lax.broadcasted_iota(jnp.int32, (L, LANES), 1)
    lo_half = lane < RWKV_HEAD
    si = lax.broadcasted_iota(jnp.int32, (H2, H2), 0)
    sj = lax.broadcasted_iota(jnp.int32, (H2, H2), 1)
    same = (si // L) == (sj // L)
    m_strict = same & (sj < si)
    m_incl = same & (sj <= si)
    eye = si == sj
    own = (si // L) == (sj // RWKV_HEAD)

    def stack(x):
        return jnp.concatenate([jnp.where(lo_half, x, 0.0), jnp.where(lo_half, 0.0, x)], axis=0)

    def rep(x):
        return jnp.concatenate([x, x], axis=0)

    pairs = range(n_pair)
    sls = [slice(p * LANES, (p + 1) * LANES) for p in pairs]
    at, bt, bend, kt, kend, rt, v_s, bonus = [], [], [], [], [], [], [], []
    for sl in sls:
        kk_s = stack(kk[:, sl])
        nrm = jnp.sqrt(jnp.sum(kk_s * kk_s, axis=-1, keepdims=True))
        kkn = kk_s / jnp.maximum(nrm, 1e-12)
        at.append(-kkn * rep(e_prev[:, sl]))
        bt.append(kkn * rep(f_b[:, sl]))
        bend.append(kkn * rep(f_bend[:, sl]))
        k2_s = stack(k2[:, sl])
        kt.append(k2_s * rep(e_neg[:, sl]))
        kend.append(k2_s * rep(e_end[:, sl]))
        r_s = stack(r[:, sl])
        rt.append(r_s * rep(e_pos[:, sl]))
        vs = stack(v[:, sl])
        v_s.append(vs)
        bonus.append(jnp.sum(r_s * k2_s * rk[:, sl], axis=-1, keepdims=True) * vs)

    gm = [_mm_nt(jnp.concatenate([at[p], rt[p]], axis=0), jnp.concatenate([bt[p], kt[p]], axis=0))
          for p in pairs]
    a_ab = [jnp.where(m_strict, gm[p][:H2, :H2], 0.0) for p in pairs]
    a_ak = [jnp.where(m_strict, gm[p][:H2, H2:], 0.0) for p in pairs]
    a_rb = [jnp.where(m_incl, gm[p][H2:, :H2], 0.0) for p in pairs]
    a_rk = [jnp.where(m_incl, gm[p][H2:, H2:], 0.0) for p in pairs]

    ident = jnp.where(eye, 1.0, 0.0)
    tinv = [ident + a_ab[p] for p in pairs]
    pw = a_ab
    akv = [_mm(a_ak[p], v_s[p]) for p in pairs]
    ark_v = [_mm(a_rk[p], v_s[p]) for p in pairs]
    kend_v = [_mm_tn(kend[p], v_s[p]) for p in pairs]
    for _ in range(max(1, (L - 1).bit_length() - 1)):
        pw = [_mm(pw[p], pw[p]) for p in pairs]
        tinv = [tinv[p] + _mm(tinv[p], pw[p]) for p in pairs]

    tx = [_mm(tinv[p], jnp.concatenate([at[p], akv[p]], axis=1)) for p in pairs]
    y2 = [_mm(a_rb[p], tx[p]) for p in pairs]
    btx = [_mm_tn(bend[p], tx[p]) for p in pairs]
    s0 = [st_ref[p] for p in pairs]
    y = [_mm(rt[p] + y2[p][:, :LANES], s0[p]) + (y2[p][:, LANES:] + ark_v[p]) for p in pairs]
    for p in pairs:
        m_mat = jnp.where(eye, jnp.broadcast_to(p_last[:, sls[p]], (H2, LANES)), 0.0) + btx[p][:, :LANES]
        st_ref[p] = _mm(m_mat, s0[p]) + (btx[p][:, LANES:] + kend_v[p])

    for p in pairs:
        sl = sls[p]
        mean = jnp.sum(y[p], axis=-1, keepdims=True) * (1.0 / RWKV_HEAD)
        dlt = jnp.where(own, y[p] - mean, 0.0)
        var = jnp.sum(dlt * dlt, axis=-1, keepdims=True) * (1.0 / RWKV_HEAD)
        yn = dlt * lax.rsqrt(var + RWKV_GN_EPS) * gng[:, sl] + jnp.where(own, gnb[:, sl], 0.0)
        ys = yn + bonus[p]
        o_ref[rows_sl, sl] = ((ys[:L] + ys[L:]) * g[:, sl]).astype(o_ref.dtype)


def _rwkv_kernel(*refs):
    st_ref, cr_ref, ck_ref, cv_ref, clo_ref = refs[-5:]

    @pl.when(pl.program_id(1) == 0)
    def _():
        for ref in (st_ref, cr_ref, ck_ref, cv_ref, clo_ref):
            ref[...] = jnp.zeros_like(ref)

    for c in range(RWKV_STEP_CHUNKS):
        _rwkv_chunk(slice(c * RWKV_CHUNK, (c + 1) * RWKV_CHUNK), *refs)


def _rwkv(proj, bsz, seq, mu_rkv, mu_lo, w0, a0, k_k, k_a, r_k, gn_g, gn_b, w2p, a2p, g2p):
    L = RWKV_CHUNK * RWKV_STEP_CHUNKS
    nc = seq // L
    W = RWKV_WIDTH
    row = lambda n: pl.BlockSpec((1, n), lambda b, c: (0, 0))
    full = lambda a: pl.BlockSpec(a.shape, lambda b, c: (0, 0))
    lo_blk = 3 * W // LORA_PAD
    v1 = lambda a: a.reshape(1, -1)
    return pl.pallas_call(
        _rwkv_kernel,
        grid=(bsz, nc),
        in_specs=[pl.BlockSpec((L, W), lambda b, c: (b * nc + c, 0)),
                  pl.BlockSpec((L, W), lambda b, c: (b * nc + c, 1)),
                  pl.BlockSpec((L, W), lambda b, c: (b * nc + c, 2)),
                  pl.BlockSpec((L, LORA_PAD), lambda b, c: (b * nc + c, lo_blk)),
                  row(3 * W), row(LORA_PAD), row(W), row(W), row(W), row(W), row(W), row(W), row(W),
                  full(w2p), full(a2p), full(g2p)],
        out_specs=pl.BlockSpec((L, W), lambda b, c: (b * nc + c, 0)),
        out_shape=jax.ShapeDtypeStruct((bsz * seq, W), BF16),
        scratch_shapes=[pltpu.VMEM((W // LANES, LANES, LANES), F32),
                        pltpu.VMEM((1, W), F32), pltpu.VMEM((1, W), F32), pltpu.VMEM((1, W), F32),
                        pltpu.VMEM((1, LORA_PAD), F32)],
        compiler_params=_cp(("arbitrary", "arbitrary")),
        name="rwkv7",
    )(proj, proj, proj, proj, v1(mu_rkv), v1(mu_lo), v1(w0), v1(a0), v1(k_k), v1(k_a), v1(r_k), v1(gn_g),
      v1(gn_b), w2p, a2p, g2p)


def _hgrn_kernel(qa_ref, qb_ref, fa_ref, fb_ref, ia_ref, ib_ref, oa_ref, ob_ref, lb_ref, gn_ref, o_ref, st_ref):
    n = HGRN_STEP
    wide = lambda a_ref, b_ref: jnp.concatenate([a_ref[...], b_ref[...]], axis=1).astype(F32)
    blk = GLA_BLOCK
    nb = n // blk
    n_head = HGRN_WIDTH // HGRN_HEAD

    @pl.when(pl.program_id(1) == 0)
    def _():
        st_ref[...] = jnp.zeros_like(st_ref)

    lb = lb_ref[...]
    qf = wide(qa_ref, qb_ref)
    q = qf * _sigmoid(qf)
    forget = lb + (1.0 - lb) * _sigmoid(wide(fa_ref, fb_ref))
    log_f = jnp.log(forget)
    kx = 1.0 - forget
    vv = wide(ia_ref, ib_ref)
    og = wide(oa_ref, ob_ref)

    ti = lax.broadcasted_iota(jnp.int32, (n, n), 0)
    tj = lax.broadcasted_iota(jnp.int32, (n, n), 1)
    same = (ti // blk) == (tj // blk)
    causal = same & (tj <= ti)
    b = _mm_exact_rhs(jnp.where(causal, 1.0, 0.0).astype(BF16), log_f)
    b_last = _mm_exact_rhs(jnp.where(same, 1.0, 0.0).astype(BF16), log_f)
    q_dec = q * jnp.exp(b)
    k_dec = kx * jnp.exp(-b)
    k_end = kx * jnp.exp(b_last - b)
    d_end = jnp.exp(b_last)
    rows = lax.broadcasted_iota(jnp.int32, (n, HGRN_HEAD), 0)
    gn = gn_ref[...]

    heads = range(n_head)
    sls = [slice(h * HGRN_HEAD, (h + 1) * HGRN_HEAD) for h in heads]
    attn = [jnp.where(causal, _mm_nt(q_dec[:, sl], k_dec[:, sl]), 0.0) for sl in sls]
    o_intra = [_mm(attn[h], vv[:, sls[h]]) for h in heads]
    vt = [vv[:, sl].T for sl in sls]
    blk_mask = [(rows // blk) == c for c in range(nb)]
    kv_all = [_mm(vt[h], jnp.concatenate([jnp.where(blk_mask[c], k_end[:, sls[h]], 0.0) for c in range(nb)],
                                         axis=1)) for h in heads]
    kv_t = [[kv_all[h][:, c * HGRN_HEAD:(c + 1) * HGRN_HEAD] for c in range(nb)] for h in heads]
    st = [[st_ref[h]] for h in heads]
    for h in heads:
        for c in range(nb):
            st[h].append(st[h][c] * d_end[c * blk:c * blk + 1, sls[h]] + kv_t[h][c])
        st_ref[h] = st[h][nb]
    for h in heads:
        sl = sls[h]
        inter = [_mm_nt(q_dec[c * blk:(c + 1) * blk, sl], st[h][c]) for c in range(nb)]
        o = o_intra[h] + jnp.concatenate(inter, axis=0)
        o = _rms(o, NORM_EPS) * gn
        ogh = og[:, sl]
        o_ref[:, sl] = (o * (ogh * _sigmoid(ogh))).astype(o_ref.dtype)


def _hgrn(proj, bsz, seq, lb, gn_g):
    n = HGRN_STEP
    ns = seq // n
    W = HGRN_WIDTH
    half = W // 2
    first = (3 * RWKV_WIDTH + LORA_PAD) // half
    col = lambda j: pl.BlockSpec((n, half), lambda b, c: (b * ns + c, first + j))
    return pl.pallas_call(
        _hgrn_kernel,
        grid=(bsz, ns),
        in_specs=[col(j) for j in range(8)] + [
                  pl.BlockSpec((1, W), lambda b, c: (0, 0)),
                  pl.BlockSpec((1, HGRN_HEAD), lambda b, c: (0, 0))],
        out_specs=pl.BlockSpec((n, W), lambda b, c: (b * ns + c, 0)),
        out_shape=jax.ShapeDtypeStruct((bsz * seq, W), BF16),
        scratch_shapes=[pltpu.VMEM((W // HGRN_HEAD, HGRN_HEAD, HGRN_HEAD), F32)],
        compiler_params=_cp(("arbitrary", "arbitrary")),
        name="hgrn2",
    )(*([proj] * 8), lb.reshape(1, W), gn_g.reshape(1, HGRN_HEAD))


def _mixout_kernel(ya_ref, yb_ref, *refs):
    ga_refs, gb_refs = refs[:MIX_GATE_BLOCKS], refs[MIX_GATE_BLOCKS:2 * MIX_GATE_BLOCKS]
    pa_ref, pb_ref, wo_ref, x_ref, g_ref, o_ref = refs[2 * MIX_GATE_BLOCKS:]
    wc = 2 * ga_refs[0].shape[1]
    ya = ya_ref[...]
    yb = yb_ref[...]
    pair = lambda rs, c: jnp.concatenate([rs[2 * c][...], rs[2 * c + 1][...]], axis=1).astype(F32)
    mix = None
    for c in range(MIX_GATE_BLOCKS // 2):
        cs = slice(c * wc, (c + 1) * wc)
        ma = jnp.dot(ya, pa_ref[:, cs], preferred_element_type=F32)
        mb = jnp.dot(yb, pb_ref[:, cs], preferred_element_type=F32)
        part = (_sigmoid(pair(ga_refs, c)) * ma + _sigmoid(pair(gb_refs, c)) * mb).astype(BF16)
        term = jnp.dot(part, wo_ref[cs, :], preferred_element_type=F32)
        mix = term if mix is None else mix + term
    o_ref[...] = x_ref[...] + g_ref[0] * mix


def _mixout(ya, yb, proj, pa, pb, wo, xf, g1, seq):
    t, w = ya.shape
    d = pa.shape[1]
    tm = min(MIX_TM, seq)
    per_b = seq // tm
    wc = d // MIX_GATE_BLOCKS
    ga0 = (3 * RWKV_WIDTH + LORA_PAD + 4 * HGRN_WIDTH) // wc
    gate = lambda j: pl.BlockSpec((tm, wc), lambda i: (i, ga0 + j))
    const = lambda a: pl.BlockSpec(a.shape, lambda i: (0, 0), pipeline_mode=pl.Buffered(1))
    return pl.pallas_call(
        _mixout_kernel,
        grid=(t // tm,),
        in_specs=[pl.BlockSpec((tm, w), lambda i: (i, 0)),
                  pl.BlockSpec((tm, w), lambda i: (i, 0))] + [gate(j) for j in range(2 * MIX_GATE_BLOCKS)] + [
                  const(pa), const(pb), const(wo),
                  pl.BlockSpec((tm, d), lambda i: (i, 0)),
                  pl.BlockSpec((1, 1, d), lambda i: (i // per_b, 0, 0))],
        out_specs=pl.BlockSpec((tm, d), lambda i: (i, 0)),
        out_shape=jax.ShapeDtypeStruct((t, d), F32),
        compiler_params=_cp(("arbitrary",)),
        name="mixout",
    )(ya, yb, *([proj] * (2 * MIX_GATE_BLOCKS)), pa, pb, wo, xf, g1)


def _route_kernel(x_ref, g_ref, sc_ref, sh_ref, rwh_ref, rwm_ref, rb_ref, h_ref, idx_ref, gate_ref, rank_ref,
                  cnt_ref, carry_ref):
    tb = x_ref.shape[0]

    @pl.when(pl.program_id(0) == 0)
    def _():
        carry_ref[...] = jnp.zeros_like(carry_ref)

    h = (_rms(x_ref[...], NORM_EPS) * g_ref[...]) * (1.0 + sc_ref[0]) + sh_ref[0]
    h_ref[...] = h
    h_hi, h_mid, _ = _split3(h)
    d = lambda a, b: jnp.dot(a, b, preferred_element_type=F32)
    logits = d(h_hi, rwh_ref[...]) + (d(h_hi, rwm_ref[...]) + d(h_mid, rwh_ref[...])) + rb_ref[...]

    lane = lax.broadcasted_iota(jnp.int32, logits.shape, 1).astype(F32)
    work = logits
    vals, idxs, sels = [], [], []
    for _ in range(TOP_K):
        m = jnp.max(work, axis=-1, keepdims=True)
        idx = jnp.min(jnp.where(work == m, lane, float(N_EXPERTS)), axis=-1, keepdims=True)
        sel = lane == idx
        vals.append(m)
        idxs.append(idx)
        sels.append(sel)
        work = jnp.where(sel, -jnp.inf, work)
    exps = [jnp.exp(v - vals[0]) for v in vals]
    den = exps[0] + exps[1] + exps[2] + exps[3]

    onehot = jnp.zeros(logits.shape, F32)
    for sel in sels:
        onehot = onehot + jnp.where(sel, 1.0, 0.0)
    ti = lax.broadcasted_iota(jnp.int32, (tb, tb), 0)
    tj = lax.broadcasted_iota(jnp.int32, (tb, tb), 1)
    below = jnp.where(tj < ti, 1.0, 0.0).astype(BF16)
    before = jnp.dot(below, onehot.astype(BF16), preferred_element_type=F32) + carry_ref[...]
    carry_ref[...] = carry_ref[...] + jnp.sum(onehot, axis=0, keepdims=True)
    cnt_ref[...] = carry_ref[...]

    out_lane = lax.broadcasted_iota(jnp.int32, (tb, LANES), 1)
    idx_out = jnp.zeros((tb, LANES), jnp.int32)
    rank_out = jnp.zeros((tb, LANES), jnp.int32)
    gate_out = jnp.zeros((tb, LANES), F32)
    for j in range(TOP_K):
        rank = jnp.sum(jnp.where(sels[j], before, 0.0), axis=-1, keepdims=True)
        idx_out = jnp.where(out_lane == j, idxs[j].astype(jnp.int32), idx_out)
        rank_out = jnp.where(out_lane == j, rank.astype(jnp.int32), rank_out)
        gate_out = jnp.where(out_lane == j, exps[j] / den, gate_out)
    idx_ref[...] = idx_out
    rank_ref[...] = rank_out
    gate_ref[...] = gate_out


def _route(x1, g, sc, sh, rw, rb, seq):
    t, d = x1.shape
    tb = min(ROUTE_TB, seq)
    per_b = seq // tb
    e = rw.shape[1]
    rw_hi = rw.astype(BF16)
    rw_mid = (rw - rw_hi.astype(F32)).astype(BF16)
    wide = lambda dt: jax.ShapeDtypeStruct((t, LANES), dt)
    return pl.pallas_call(
        _route_kernel,
        grid=(t // tb,),
        in_specs=[pl.BlockSpec((tb, d), lambda i: (i, 0)),
                  pl.BlockSpec((1, d), lambda i: (0, 0)),
                  pl.BlockSpec((1, 1, d), lambda i: (i // per_b, 0, 0)),
                  pl.BlockSpec((1, 1, d), lambda i: (i // per_b, 0, 0)),
                  pl.BlockSpec((d, e), lambda i: (0, 0)),
                  pl.BlockSpec((d, e), lambda i: (0, 0)),
                  pl.BlockSpec((1, e), lambda i: (0, 0))],
        out_specs=[pl.BlockSpec((tb, d), lambda i: (i, 0)),
                   pl.BlockSpec((tb, LANES), lambda i: (i, 0)),
                   pl.BlockSpec((tb, LANES), lambda i: (i, 0)),
                   pl.BlockSpec((tb, LANES), lambda i: (i, 0)),
                   pl.BlockSpec((1, e), lambda i: (0, 0))],
        out_shape=[jax.ShapeDtypeStruct((t, d), F32), wide(jnp.int32), wide(F32), wide(jnp.int32),
                   jax.ShapeDtypeStruct((1, e), F32)],
        scratch_shapes=[pltpu.VMEM((1, e), F32)],
        compiler_params=_cp(("arbitrary",)),
        name="route",
    )(x1, g.reshape(1, d), sc, sh, rw_hi, rw_mid, rb.reshape(1, e))


def _scatter_kernel(dest_ref, zflag_ref, h_ref, xs_ref, zbuf, sem, zsem):
    tb = h_ref.shape[0]
    zrows = zbuf.shape[0]
    base = pl.program_id(0) * tb

    @pl.when(pl.program_id(0) == 0)
    def _():
        zbuf[...] = jnp.zeros_like(zbuf)

        def zcopy(b):
            return pltpu.make_async_copy(zbuf, xs_ref.at[pl.ds(pl.multiple_of(b * zrows, zrows), zrows), :], zsem)

        def zstart(b, carry):
            @pl.when(zflag_ref[b] != 0)
            def _():
                zcopy(b).start()
            return carry

        def zwait(b, carry):
            @pl.when(zflag_ref[b] != 0)
            def _():
                zcopy(b).wait()
            return carry

        n_blk = xs_ref.shape[0] // zrows
        lax.fori_loop(0, n_blk, zstart, 0)
        lax.fori_loop(0, n_blk, zwait, 0)

    def copy(t, j):
        d = dest_ref[(base + t) * TOP_K + j]
        return pltpu.make_async_copy(h_ref.at[pl.ds(t, 1), :], xs_ref.at[pl.ds(d, 1), :], sem)

    def start(t, carry):
        for j in range(TOP_K):
            copy(t, j).start()
        return carry

    lax.fori_loop(0, tb, start, 0, unroll=ROW_DMA_UNROLL)
    for j in range(TOP_K):
        pltpu.make_async_copy(h_ref, xs_ref.at[pl.ds(0, tb), :], sem).wait()


def _scatter_rows(dest_flat, zero_flag, h2, n_rows):
    t, d = h2.shape
    tb = min(SCATTER_TB, t)
    return pl.pallas_call(
        _scatter_kernel,
        grid_spec=pltpu.PrefetchScalarGridSpec(
            num_scalar_prefetch=2,
            grid=(t // tb,),
            in_specs=[pl.BlockSpec((tb, d), lambda i, dest, zf: (i, 0))],
            out_specs=pl.BlockSpec(memory_space=pl.ANY),
            scratch_shapes=[pltpu.VMEM((MOE_TM // 2, d), F32), pltpu.SemaphoreType.DMA(()),
                            pltpu.SemaphoreType.DMA(())],
        ),
        out_shape=jax.ShapeDtypeStruct((n_rows, d), F32),
        compiler_params=_cp(("arbitrary",)),
        name="scatter_rows",
    )(dest_flat, zero_flag, h2)


def _expert_kernel(be_ref, nu_ref, rows_ref, bgu_ref, bd_ref, xs_hbm, wgu_hbm, wd_hbm, ys_hbm,
                   xbuf, xb_ref, act_ref, wbuf, obuf, sem, xsem, osem):
    i = pl.program_id(0)
    nt, tm, tn = act_ref.shape
    sub = tm // 2
    n_slot, de, tnb = wbuf.shape
    ntb = obuf.shape[1] // tnb
    n_steps = nt + ntb
    look = n_slot - 1
    nu = nu_ref[0]
    live = i < nu

    def tile_copies(blk, step, slot):
        e = be_ref[blk]
        if step < nt:
            return [pltpu.make_async_copy(wgu_hbm.at[e, :, half * de + step * tn:half * de + (step + 1) * tn],
                                          wbuf.at[slot, :, half * tn:(half + 1) * tn], sem.at[slot])
                    for half in range(2)]
        colb = (step - nt) * tnb
        return [pltpu.make_async_copy(wd_hbm.at[e, :, colb:colb + tnb], wbuf.at[slot], sem.at[slot])]

    def request(blk, sub_step):
        blk = blk + sub_step // n_steps
        step = sub_step % n_steps

        @pl.when(blk < nu)
        def _():
            for cp in tile_copies(blk, step, (blk * n_steps + step) % n_slot):
                cp.start()

    def by_size(blk, run):
        @pl.when(rows_ref[blk] > sub)
        def _():
            run(tm)

        @pl.when(rows_ref[blk] <= sub)
        def _():
            run(sub)

    def x_copy(blk, rows):
        r0 = pl.multiple_of(blk * tm, tm)
        return pltpu.make_async_copy(xs_hbm.at[pl.ds(r0, rows), :], xbuf.at[0:rows], xsem)

    def o_copy(blk):
        return pltpu.make_async_copy(obuf, ys_hbm.at[pl.ds(pl.multiple_of(blk * tm, tm), tm), :], osem)

    @pl.when(i == 0)
    def _():
        obuf[...] = jnp.zeros_like(obuf)
        by_size(i, lambda rows: x_copy(i, rows).start())
        for sub_step in range(look):
            request(i, sub_step)

    @pl.when(live)
    def _():
        by_size(i, lambda rows: x_copy(i, rows).wait())

        def cast(rows):
            xb_ref[0:rows] = xbuf[0:rows].astype(BF16)
        by_size(i, cast)

        @pl.when(i + 1 < nu)
        def _():
            by_size(i + 1, lambda rows: x_copy(i + 1, rows).start())

        for n in range(n_steps):
            request(i, n + look)
            slot = (i * n_steps + n) % n_slot
            for cp in tile_copies(i, n, slot):
                cp.wait()
            if n < nt:
                def run(rows, n=n, slot=slot):
                    xb = xb_ref[0:rows]
                    w = wbuf[slot]
                    glu = jnp.dot(xb, w[:, 0:tn].astype(BF16), preferred_element_type=F32) + bgu_ref[0, n]
                    lin = jnp.dot(xb, w[:, tn:2 * tn].astype(BF16), preferred_element_type=F32) + bgu_ref[0, n + nt]
                    glu = jnp.minimum(glu, SWIGLU_LIMIT)
                    lin = jnp.clip(lin, -SWIGLU_LIMIT, SWIGLU_LIMIT)
                    act_ref[n, 0:rows] = (glu * _sigmoid(SWIGLU_ALPHA * glu) * (lin + 1.0)).astype(BF16)
            else:
                if n == nt:
                    @pl.when(i > 0)
                    def _():
                        o_copy(i - 1).wait()

                def run(rows, m=n - nt, slot=slot):
                    wd = wbuf[slot]
                    y = jnp.dot(act_ref[0, 0:rows], wd[0:tn].astype(BF16), preferred_element_type=F32)
                    for k in range(1, nt):
                        y = y + jnp.dot(act_ref[k, 0:rows], wd[k * tn:(k + 1) * tn].astype(BF16),
                                        preferred_element_type=F32)
                    obuf[0:rows, m * tnb:(m + 1) * tnb] = y + bd_ref[0, m]
            by_size(i, run)
        o_copy(i).start()

    @pl.when(jnp.logical_not(live))
    def _():
        @pl.when(i > 0)
        def _():
            o_copy(i - 1).wait()

        @pl.when(i == nu)
        def _():
            obuf[...] = jnp.zeros_like(obuf)
        o_copy(i).start()

    @pl.when(i == pl.num_programs(0) - 1)
    def _():
        o_copy(i).wait()


def _experts(blk_expert, n_used, blk_rows, xs, w_gu, b_gu, w_down, b_down):
    n_rows, d = xs.shape
    tm, tn, tnb = MOE_TM, MOE_TN, MOE_TNB
    nb = n_rows // tm
    e, _, two_de = w_gu.shape
    de = two_de // 2
    nt = de // tn
    ntb = d // tnb
    assert 2 * tn == tnb and de == d
    b_gu4 = b_gu.reshape(e, 2 * nt, 1, tn)
    b_d4 = b_down.reshape(e, ntb, 1, tnb)
    return pl.pallas_call(
        _expert_kernel,
        grid_spec=pltpu.PrefetchScalarGridSpec(
            num_scalar_prefetch=3,
            grid=(nb,),
            in_specs=[pl.BlockSpec((1, 2 * nt, 1, tn), lambda i, be, nu, br: (be[i], 0, 0, 0)),
                      pl.BlockSpec((1, ntb, 1, tnb), lambda i, be, nu, br: (be[i], 0, 0, 0)),
                      pl.BlockSpec(memory_space=pl.ANY),
                      pl.BlockSpec(memory_space=pl.ANY),
                      pl.BlockSpec(memory_space=pl.ANY)],
            out_specs=pl.BlockSpec(memory_space=pl.ANY),
            scratch_shapes=[pltpu.VMEM((tm, d), F32), pltpu.VMEM((tm, d), BF16), pltpu.VMEM((nt, tm, tn), BF16),
                            pltpu.VMEM((MOE_RING, de, tnb), F32), pltpu.VMEM((tm, d), F32),
                            pltpu.SemaphoreType.DMA((MOE_RING,)), pltpu.SemaphoreType.DMA(()),
                            pltpu.SemaphoreType.DMA(())],
        ),
        out_shape=jax.ShapeDtypeStruct((n_rows, d), F32),
        compiler_params=_cp(("arbitrary",)),
        name="experts",
    )(blk_expert, n_used, blk_rows, b_gu4, b_d4, xs, w_gu, w_down)


def _combine_kernel(dest_ref, ys_ref, gate_ref, x_ref, g2_ref, fg_ref, o_ref, buf_ref, sem):
    tb = x_ref.shape[0]
    i = pl.program_id(0)
    half = i % 2

    def gather(blk, slot):
        def start(t, carry):
            for j in range(TOP_K):
                d = dest_ref[(blk * tb + t) * TOP_K + j]
                pltpu.make_async_copy(ys_ref.at[pl.ds(d, 1), :], buf_ref.at[slot, j, pl.ds(t, 1), :],
                                      sem.at[slot]).start()
            return carry
        lax.fori_loop(0, tb, start, 0, unroll=ROW_DMA_UNROLL)

    @pl.when(i == 0)
    def _():
        gather(i, half)

    @pl.when(i + 1 < pl.num_programs(0))
    def _():
        gather(i + 1, 1 - half)

    for j in range(TOP_K):
        pltpu.make_async_copy(ys_ref.at[pl.ds(0, tb), :], buf_ref.at[half, j], sem.at[half]).wait()

    gate = gate_ref[...]
    ffn = gate[:, 0:1] * buf_ref[half, 0]
    for j in range(1, TOP_K):
        ffn = ffn + gate[:, j:j + 1] * buf_ref[half, j]
    x2 = x_ref[...] + g2_ref[0] * ffn
    o_ref[...] = _rms(x2, NORM_EPS) * fg_ref[...]


def _combine(dest_flat, ys, gates, x1, g2, fg, seq):
    t, d = x1.shape
    tb = min(COMBINE_TB, seq)
    per_b = seq // tb
    return pl.pallas_call(
        _combine_kernel,
        grid_spec=pltpu.PrefetchScalarGridSpec(
            num_scalar_prefetch=1,
            grid=(t // tb,),
            in_specs=[pl.BlockSpec(memory_space=pl.ANY),
                      pl.BlockSpec((tb, LANES), lambda i, dest: (i, 0)),
                      pl.BlockSpec((tb, d), lambda i, dest: (i, 0)),
                      pl.BlockSpec((1, 1, d), lambda i, dest: (i // per_b, 0, 0)),
                      pl.BlockSpec((1, d), lambda i, dest: (0, 0))],
            out_specs=pl.BlockSpec((tb, d), lambda i, dest: (i, 0)),
            scratch_shapes=[pltpu.VMEM((2, TOP_K, tb, d), F32), pltpu.SemaphoreType.DMA((2,))],
        ),
        out_shape=jax.ShapeDtypeStruct((t, d), F32),
        compiler_params=_cp(("arbitrary",)),
        name="combine",
    )(dest_flat, ys, gates, x1, g2, fg.reshape(1, d))


def _moe(x1, g, sc2, sh2, g2, router_w, router_b, w_gu, b_gu, w_down, b_down, final_g, seq):
    t, d = x1.shape
    a = t * TOP_K
    tm = MOE_TM
    nb = -(-a // tm) + N_EXPERTS
    h2, idx_w, gate_w, rank_w, counts = _route(x1, g, sc2, sh2, router_w, router_b, seq)
    idx = idx_w[:, :TOP_K]
    rank = rank_w[:, :TOP_K]
    cnt = counts.reshape(-1).astype(jnp.int32)
    padded = ((cnt + tm - 1) // tm) * tm
    pad_end = jnp.cumsum(padded)
    pad_start = pad_end - padded
    experts = jnp.arange(N_EXPERTS, dtype=jnp.int32)
    start_of = jnp.sum(jnp.where(idx[:, :, None] == experts, pad_start, 0), axis=-1)
    dest = (start_of + rank).reshape(a).astype(jnp.int32)
    n_used = (pad_end[-1] // tm).astype(jnp.int32)
    blk = jnp.arange(nb, dtype=jnp.int32)
    blk_src = jnp.clip(blk, 0, jnp.maximum(n_used - 1, 0))
    blk_expert = jnp.sum((pad_end[None, :] <= (blk_src * tm)[:, None]).astype(jnp.int32), axis=1)
    blk_expert = jnp.minimum(blk_expert, N_EXPERTS - 1)
    own = blk_expert[:, None] == experts
    rows_end = jnp.sum(jnp.where(own, pad_start + cnt, 0), axis=1)
    blk_rows = jnp.clip(rows_end - blk_src * tm, 0, tm).astype(jnp.int32)
    half = jnp.arange(2 * nb, dtype=jnp.int32)
    zero_flag = ((half // 2 >= n_used) | ((half + 1) * (tm // 2) > rows_end[half // 2])).astype(jnp.int32)
    xs = _scatter_rows(dest, zero_flag, h2, nb * tm)
    ys = _experts(blk_expert, n_used.reshape(1), blk_rows, xs, w_gu, b_gu, w_down, b_down)
    return _combine(dest, ys, gate_w, x1, g2, final_g, seq)


def kernel(x, c, ada_w, ada_b, norm1_g, norm2_g, w_in, rwkv_mu, rwkv_w0, rwkv_w2, rwkv_a0, rwkv_a2, rwkv_g2,
           rwkv_k_k, rwkv_k_a, rwkv_r_k, rwkv_gn_g, rwkv_gn_b, hgrn_lb_logits, hgrn_gn_g, proj_a, proj_b, w_out,
           router_w, router_b, exp_w_gate_up, exp_b_gate_up, exp_w_down, exp_b_down, final_norm_g):
    bsz, seq, d = x.shape
    depth = ada_w.shape[0]
    lower_bounds = jnp.cumsum(jax.nn.softmax(hgrn_lb_logits.astype(F32), axis=0), axis=0)
    xf = x.reshape(bsz * seq, d)
    rkv = 3 * RWKV_WIDTH
    rwkv_cols = rkv + LORA_COLS
    for l in range(depth):
        mod = _adaln(c, ada_w[l], ada_b[l])
        sh1, sc1, g1, sh2, sc2, g2 = [m.reshape(bsz, 1, d) for m in jnp.split(mod, 6, axis=-1)]

        wl = w_in[l]
        w_a = wl[:, :rkv + LORA_PAD].astype(BF16)
        w_b = wl[:, rwkv_cols:].astype(BF16)
        proj = _inproj(xf, norm1_g[l], sc1, sh1, w_a, w_b, seq)

        mu = rwkv_mu[l]
        mu_lo = jnp.pad(mu[rkv:], (0, LORA_PAD - LORA_COLS))
        pad_rows = lambda w, r0: jnp.zeros((LORA_PAD, RWKV_WIDTH), F32).at[r0:r0 + w.shape[0]].set(w).astype(BF16)
        ya = _rwkv(proj, bsz, seq, mu[:rkv], mu_lo, rwkv_w0[l], rwkv_a0[l], rwkv_k_k[l], rwkv_k_a[l],
                   rwkv_r_k[l].reshape(-1), rwkv_gn_g[l], rwkv_gn_b[l],
                   pad_rows(rwkv_w2[l], 0), pad_rows(rwkv_a2[l], DECAY_LORA),
                   pad_rows(rwkv_g2[l], DECAY_LORA + AAA_LORA))
        yb = _hgrn(proj, bsz, seq, lower_bounds[l], hgrn_gn_g[l])
        x1 = _mixout(ya, yb, proj, proj_a[l].astype(BF16), proj_b[l].astype(BF16), w_out[l].astype(BF16), xf, g1, seq)
        assert depth == 1
        out = _moe(x1, norm2_g[l], sc2, sh2, g2, router_w[l], router_b[l], exp_w_gate_up[l], exp_b_gate_up[l],
                   exp_w_down[l], exp_b_down[l], final_norm_g, seq)
    return out.reshape(bsz, seq, d)
```

```python
import functools

import jax
import jax.numpy as jnp
from jax import lax
from jax.experimental import pallas as pl
from jax.experimental.pallas import tpu as pltpu

F32 = jnp.float32
BF16 = jnp.bfloat16

D_MODEL = 2048
RWKV_WIDTH = 1024
RWKV_HEAD = 64
DECAY_LORA = 64
AAA_LORA = 64
GATE_LORA = 160
LORA_COLS = DECAY_LORA + AAA_LORA + GATE_LORA
LORA_PAD = 512
RWKV_GN_EPS = 64e-5
HGRN_WIDTH = 1024
HGRN_HEAD = 128
GLA_BLOCK = 16
N_EXPERTS = 32
TOP_K = 4
D_EXPERT = 2048
SWIGLU_ALPHA = 1.702
SWIGLU_LIMIT = 7.0
NORM_EPS = 1e-5

LANES = 128
VMEM_LIMIT = 56 * 1024 * 1024

RWKV_CHUNK = 64
RWKV_STEP_CHUNKS = 2
HGRN_STEP = 128
PROJ_TM = 1024
PROJ_TN = 512
PROJ_RING = 4
NORM_ROWS = 128
MIX_TM = 256
MIX_GATE_BLOCKS = 4
MOE_TM = 1024
MOE_SUB = 256
MOE_TN = 256
MOE_TNB = 512
MOE_RING = 4
ROUTE_TB = 256
SCATTER_TB = 256
COMBINE_TB = 128
ROW_DMA_UNROLL = 8


def _cp(sem, vmem=VMEM_LIMIT):
    return pltpu.CompilerParams(dimension_semantics=sem, vmem_limit_bytes=vmem)


def _mm(a, b):
    return jnp.dot(a.astype(BF16), b.astype(BF16), preferred_element_type=F32)


def _mm_nt(a, b):
    return lax.dot_general(a.astype(BF16), b.astype(BF16), (((1,), (1,)), ((), ())),
                           preferred_element_type=F32)


def _mm_tn(a, b):
    return lax.dot_general(a.astype(BF16), b.astype(BF16), (((0,), (0,)), ((), ())),
                           preferred_element_type=F32)


def _split3(x):
    hi = x.astype(BF16)
    r1 = x - hi.astype(F32)
    mid = r1.astype(BF16)
    lo = (r1 - mid.astype(F32)).astype(BF16)
    return hi, mid, lo


def _mm_exact_rhs(m_bf16, x):
    hi, mid, lo = _split3(x)
    d = lambda p: jnp.dot(m_bf16, p, preferred_element_type=F32)
    return d(hi) + d(mid) + d(lo)


def _sigmoid(x):
    return 1.0 / (1.0 + jnp.exp(-x))


def _softplus(x):
    return jnp.maximum(x, 0.0) + jnp.log(1.0 + jnp.exp(-jnp.abs(x)))


def _rms(x, eps):
    return x * lax.rsqrt(jnp.mean(x * x, axis=-1, keepdims=True) + eps)


def _adaln_kernel(c_ref, w_ref, b_ref, o_ref):
    c = c_ref[...]
    c_act = c * _sigmoid(c)
    o_ref[...] = jnp.dot(c_act, w_ref[...], preferred_element_type=F32,
                         precision=lax.Precision.HIGHEST) + b_ref[...]


def _adaln(c, w, b):
    bsz, d = c.shape
    n = w.shape[1]
    tn = 1024
    return pl.pallas_call(
        _adaln_kernel,
        grid=(n // tn,),
        in_specs=[pl.BlockSpec((bsz, d), lambda j: (0, 0)),
                  pl.BlockSpec((d, tn), lambda j: (0, j)),
                  pl.BlockSpec((1, tn), lambda j: (0, j))],
        out_specs=pl.BlockSpec((bsz, tn), lambda j: (0, j)),
        out_shape=jax.ShapeDtypeStruct((bsz, n), F32),
        compiler_params=_cp(("arbitrary",)),
        name="adaln",
    )(c, w, b.reshape(1, n))


def _inproj_kernel(n_a, x_ref, g_ref, sc_ref, sh_ref, wa_hbm, wb_hbm, o_hbm, h_ref, wbuf, obuf, wsem, osem):
    i = pl.program_id(0)
    tm = x_ref.shape[0]
    n_slot, _, tn = wbuf.shape
    look = n_slot - 1
    n_tiles = o_hbm.shape[1] // tn

    def w_copy(tile, slot):
        src = (wa_hbm.at[:, tile * tn:(tile + 1) * tn] if tile < n_a
               else wb_hbm.at[:, (tile - n_a) * tn:(tile - n_a + 1) * tn])
        return pltpu.make_async_copy(src, wbuf.at[slot], wsem.at[slot])

    def o_copy(tile):
        rows = pl.ds(pl.multiple_of(i * tm, tm), tm)
        return pltpu.make_async_copy(obuf.at[tile % 2], o_hbm.at[rows, tile * tn:(tile + 1) * tn], osem.at[tile % 2])

    def request(sub):
        blk = i + sub // n_tiles
        tile = sub % n_tiles

        @pl.when(blk < pl.num_programs(0))
        def _():
            w_copy(tile, (blk * n_tiles + tile) % n_slot).start()

    @pl.when(i == 0)
    def _():
        for sub in range(look):
            request(sub)

    def rows(c, carry):
        rs = pl.ds(pl.multiple_of(c * NORM_ROWS, NORM_ROWS), NORM_ROWS)
        y = _rms(x_ref[rs, :], NORM_EPS) * g_ref[...]
        h_ref[rs, :] = (y * (1.0 + sc_ref[0]) + sh_ref[0]).astype(BF16)
        return carry
    lax.fori_loop(0, tm // NORM_ROWS, rows, 0)

    for tile in range(n_tiles):
        request(tile + look)
        slot = (i * n_tiles + tile) % n_slot
        w_copy(tile, slot).wait()
        if tile >= 2:
            o_copy(tile - 2).wait()
        obuf[tile % 2] = jnp.dot(h_ref[...], wbuf[slot], preferred_element_type=F32).astype(obuf.dtype)
        o_copy(tile).start()
    for tile in range(max(n_tiles - 2, 0), n_tiles):
        o_copy(tile).wait()


def _inproj(xf, g, sc, sh, w_a, w_b, seq):
    t, d = xf.shape
    tm = min(PROJ_TM, seq)
    tn = PROJ_TN
    n_a, n_b = w_a.shape[1] // tn, w_b.shape[1] // tn
    per_b = seq // tm
    return pl.pallas_call(
        functools.partial(_inproj_kernel, n_a),
        grid=(t // tm,),
        in_specs=[pl.BlockSpec((tm, d), lambda i: (i, 0)),
                  pl.BlockSpec((1, d), lambda i: (0, 0)),
                  pl.BlockSpec((1, 1, d), lambda i: (i // per_b, 0, 0)),
                  pl.BlockSpec((1, 1, d), lambda i: (i // per_b, 0, 0)),
                  pl.BlockSpec(memory_space=pl.ANY),
                  pl.BlockSpec(memory_space=pl.ANY)],
        out_specs=pl.BlockSpec(memory_space=pl.ANY),
        out_shape=jax.ShapeDtypeStruct((t, (n_a + n_b) * tn), BF16),
        scratch_shapes=[pltpu.VMEM((tm, d), BF16), pltpu.VMEM((PROJ_RING, d, tn), BF16),
                        pltpu.VMEM((2, tm, tn), BF16), pltpu.SemaphoreType.DMA((PROJ_RING,)),
                        pltpu.SemaphoreType.DMA((2,))],
        compiler_params=_cp(("arbitrary",)),
        name="inproj",
    )(xf, g.reshape(1, d), sc, sh, w_a, w_b)


def _rwkv_chunk(rows_sl, r_ref, k_ref, v_ref, lo_ref, mu_ref, mulo_ref, w0_ref, a0_ref, kk_ref, ka_ref, rk_ref,
                gng_ref, gnb_ref, w2_ref, a2_ref, g2_ref, o_ref,
                st_ref, cr_ref, ck_ref, cv_ref, clo_ref):
    L = RWKV_CHUNK
    H2 = 2 * L
    n_pair = RWKV_WIDTH // LANES

    def shift(x_ref, mu, carry_ref):
        x = x_ref[rows_sl, :].astype(F32)
        rows = lax.broadcasted_iota(jnp.int32, x.shape, 0)
        prev = jnp.where(rows == 0, carry_ref[...], pltpu.roll(x, 1, axis=0))
        carry_ref[...] = x[L - 1:L, :]
        return x + (prev - x) * mu

    mu = mu_ref[...]
    r = shift(r_ref, mu[:, 0:RWKV_WIDTH], cr_ref)
    k = shift(k_ref, mu[:, RWKV_WIDTH:2 * RWKV_WIDTH], ck_ref)
    v = shift(v_ref, mu[:, 2 * RWKV_WIDTH:3 * RWKV_WIDTH], cv_ref)
    lo = shift(lo_ref, mulo_ref[...], clo_ref)

    wpre = w0_ref[...] + _mm(jnp.tanh(lo), w2_ref[...])
    lw = -jnp.exp(-_softplus(-wpre) - 0.5)
    a = _sigmoid(a0_ref[...] + _mm(lo, a2_ref[...]))
    g = _mm(_sigmoid(lo), g2_ref[...])

    ti = lax.broadcasted_iota(jnp.int32, (L, L), 0)
    tj = lax.broadcasted_iota(jnp.int32, (L, L), 1)
    tri = jnp.where(tj <= ti, 1.0, 0.0).astype(BF16)
    cl = _mm_exact_rhs(tri, lw)
    cl_last = cl[L - 1:L, :]
    e_pos = jnp.exp(cl)
    e_neg = jnp.exp(-cl)
    e_prev = jnp.exp(cl - lw)
    e_end = jnp.exp(cl_last - cl)
    p_last = jnp.exp(cl_last)

    kk = k * kk_ref[...]
    k2 = k * (1.0 + (a - 1.0) * ka_ref[...])
    f_b = a * e_neg
    f_bend = a * e_end
    rk = rk_ref[...]
    gng = gng_ref[...]
    gnb = gnb_ref[...]

    lane = lax.broadcasted_iota(jnp.int32, (L, LANES), 1)
    lo_half = lane < RWKV_HEAD
    si = lax.broadcasted_iota(jnp.int32, (H2, H2), 0)
    sj = lax.broadcasted_iota(jnp.int32, (H2, H2), 1)
    same = (si // L) == (sj // L)
    m_strict = same & (sj < si)
    m_incl = same & (sj <= si)
    eye = si == sj
    own = (si // L) == (sj // RWKV_HEAD)

    def stack(x):
        return jnp.concatenate([jnp.where(lo_half, x, 0.0), jnp.where(lo_half, 0.0, x)], axis=0)

    def rep(x):
        return jnp.concatenate([x, x], axis=0)

    pairs = range(n_pair)
    sls = [slice(p * LANES, (p + 1) * LANES) for p in pairs]
    at, bt, bend, kt, kend, rt, v_s, bonus = [], [], [], [], [], [], [], []
    for sl in sls:
        kk_s = stack(kk[:, sl])
        nrm = jnp.sqrt(jnp.sum(kk_s * kk_s, axis=-1, keepdims=True))
        kkn = kk_s / jnp.maximum(nrm, 1e-12)
        at.append(-kkn * rep(e_prev[:, sl]))
        bt.append(kkn * rep(f_b[:, sl]))
        bend.append(kkn * rep(f_bend[:, sl]))
        k2_s = stack(k2[:, sl])
        kt.append(k2_s * rep(e_neg[:, sl]))
        kend.append(k2_s * rep(e_end[:, sl]))
        r_s = stack(r[:, sl])
        rt.append(r_s * rep(e_pos[:, sl]))
        vs = stack(v[:, sl])
        v_s.append(vs)
        bonus.append(jnp.sum(r_s * k2_s * rk[:, sl], axis=-1, keepdims=True) * vs)

    gm = [_mm_nt(jnp.concatenate([at[p], rt[p]], axis=0), jnp.concatenate([bt[p], kt[p]], axis=0))
          for p in pairs]
    a_ab = [jnp.where(m_strict, gm[p][:H2, :H2], 0.0) for p in pairs]
    a_ak = [jnp.where(m_strict, gm[p][:H2, H2:], 0.0) for p in pairs]
    a_rb = [jnp.where(m_incl, gm[p][H2:, :H2], 0.0) for p in pairs]
    a_rk = [jnp.where(m_incl, gm[p][H2:, H2:], 0.0) for p in pairs]

    ident = jnp.where(eye, 1.0, 0.0)
    tinv = [ident + a_ab[p] for p in pairs]
    pw = a_ab
    akv = [_mm(a_ak[p], v_s[p]) for p in pairs]
    ark_v = [_mm(a_rk[p], v_s[p]) for p in pairs]
    kend_v = [_mm_tn(kend[p], v_s[p]) for p in pairs]
    for _ in range(max(1, (L - 1).bit_length() - 1)):
        pw = [_mm(pw[p], pw[p]) for p in pairs]
        tinv = [tinv[p] + _mm(tinv[p], pw[p]) for p in pairs]

    tx = [_mm(tinv[p], jnp.concatenate([at[p], akv[p]], axis=1)) for p in pairs]
    y2 = [_mm(a_rb[p], tx[p]) for p in pairs]
    btx = [_mm_tn(bend[p], tx[p]) for p in pairs]
    s0 = [st_ref[p] for p in pairs]
    y = [_mm(rt[p] + y2[p][:, :LANES], s0[p]) + (y2[p][:, LANES:] + ark_v[p]) for p in pairs]
    for p in pairs:
        m_mat = jnp.where(eye, jnp.broadcast_to(p_last[:, sls[p]], (H2, LANES)), 0.0) + btx[p][:, :LANES]
        st_ref[p] = _mm(m_mat, s0[p]) + (btx[p][:, LANES:] + kend_v[p])

    for p in pairs:
        sl = sls[p]
        mean = jnp.sum(y[p], axis=-1, keepdims=True) * (1.0 / RWKV_HEAD)
        dlt = jnp.where(own, y[p] - mean, 0.0)
        var = jnp.sum(dlt * dlt, axis=-1, keepdims=True) * (1.0 / RWKV_HEAD)
        yn = dlt * lax.rsqrt(var + RWKV_GN_EPS) * gng[:, sl] + jnp.where(own, gnb[:, sl], 0.0)
        ys = yn + bonus[p]
        o_ref[rows_sl, sl] = ((ys[:L] + ys[L:]) * g[:, sl]).astype(o_ref.dtype)


def _rwkv_kernel(*refs):
    st_ref, cr_ref, ck_ref, cv_ref, clo_ref = refs[-5:]

    @pl.when(pl.program_id(1) == 0)
    def _():
        for ref in (st_ref, cr_ref, ck_ref, cv_ref, clo_ref):
            ref[...] = jnp.zeros_like(ref)

    for c in range(RWKV_STEP_CHUNKS):
        _rwkv_chunk(slice(c * RWKV_CHUNK, (c + 1) * RWKV_CHUNK), *refs)


def _rwkv(proj, bsz, seq, mu_rkv, mu_lo, w0, a0, k_k, k_a, r_k, gn_g, gn_b, w2p, a2p, g2p):
    L = RWKV_CHUNK * RWKV_STEP_CHUNKS
    nc = seq // L
    W = RWKV_WIDTH
    row = lambda n: pl.BlockSpec((1, n), lambda b, c: (0, 0))
    full = lambda a: pl.BlockSpec(a.shape, lambda b, c: (0, 0))
    lo_blk = 3 * W // LORA_PAD
    v1 = lambda a: a.reshape(1, -1)
    return pl.pallas_call(
        _rwkv_kernel,
        grid=(bsz, nc),
        in_specs=[pl.BlockSpec((L, W), lambda b, c: (b * nc + c, 0)),
                  pl.BlockSpec((L, W), lambda b, c: (b * nc + c, 1)),
                  pl.BlockSpec((L, W), lambda b, c: (b * nc + c, 2)),
                  pl.BlockSpec((L, LORA_PAD), lambda b, c: (b * nc + c, lo_blk)),
                  row(3 * W), row(LORA_PAD), row(W), row(W), row(W), row(W), row(W), row(W), row(W),
                  full(w2p), full(a2p), full(g2p)],
        out_specs=pl.BlockSpec((L, W), lambda b, c: (b * nc + c, 0)),
        out_shape=jax.ShapeDtypeStruct((bsz * seq, W), BF16),
        scratch_shapes=[pltpu.VMEM((W // LANES, LANES, LANES), F32),
                        pltpu.VMEM((1, W), F32), pltpu.VMEM((1, W), F32), pltpu.VMEM((1, W), F32),
                        pltpu.VMEM((1, LORA_PAD), F32)],
        compiler_params=_cp(("arbitrary", "arbitrary")),
        name="rwkv7",
    )(proj, proj, proj, proj, v1(mu_rkv), v1(mu_lo), v1(w0), v1(a0), v1(k_k), v1(k_a), v1(r_k), v1(gn_g),
      v1(gn_b), w2p, a2p, g2p)


def _hgrn_kernel(qa_ref, qb_ref, fa_ref, fb_ref, ia_ref, ib_ref, oa_ref, ob_ref, lb_ref, gn_ref, o_ref, st_ref):
    n = HGRN_STEP
    wide = lambda a_ref, b_ref: jnp.concatenate([a_ref[...], b_ref[...]], axis=1).astype(F32)
    blk = GLA_BLOCK
    nb = n // blk
    n_head = HGRN_WIDTH // HGRN_HEAD

    @pl.when(pl.program_id(1) == 0)
    def _():
        st_ref[...] = jnp.zeros_like(st_ref)

    lb = lb_ref[...]
    qf = wide(qa_ref, qb_ref)
    q = qf * _sigmoid(qf)
    forget = lb + (1.0 - lb) * _sigmoid(wide(fa_ref, fb_ref))
    log_f = jnp.log(forget)
    kx = 1.0 - forget
    vv = wide(ia_ref, ib_ref)
    og = wide(oa_ref, ob_ref)

    ti = lax.broadcasted_iota(jnp.int32, (n, n), 0)
    tj = lax.broadcasted_iota(jnp.int32, (n, n), 1)
    same = (ti // blk) == (tj // blk)
    causal = same & (tj <= ti)
    b = _mm_exact_rhs(jnp.where(causal, 1.0, 0.0).astype(BF16), log_f)
    b_last = _mm_exact_rhs(jnp.where(same, 1.0, 0.0).astype(BF16), log_f)
    q_dec = q * jnp.exp(b)
    k_dec = kx * jnp.exp(-b)
    k_end = kx * jnp.exp(b_last - b)
    d_end = jnp.exp(b_last)
    rows = lax.broadcasted_iota(jnp.int32, (n, HGRN_HEAD), 0)
    gn = gn_ref[...]

    heads = range(n_head)
    sls = [slice(h * HGRN_HEAD, (h + 1) * HGRN_HEAD) for h in heads]
    attn = [jnp.where(causal, _mm_nt(q_dec[:, sl], k_dec[:, sl]), 0.0) for sl in sls]
    o_intra = [_mm(attn[h], vv[:, sls[h]]) for h in heads]
    vt = [vv[:, sl].T for sl in sls]
    blk_mask = [(rows // blk) == c for c in range(nb)]
    kv_all = [_mm(vt[h], jnp.concatenate([jnp.where(blk_mask[c], k_end[:, sls[h]], 0.0) for c in range(nb)],
                                         axis=1)) for h in heads]
    kv_t = [[kv_all[h][:, c * HGRN_HEAD:(c + 1) * HGRN_HEAD] for c in range(nb)] for h in heads]
    st = [[st_ref[h]] for h in heads]
    for h in heads:
        for c in range(nb):
            st[h].append(st[h][c] * d_end[c * blk:c * blk + 1, sls[h]] + kv_t[h][c])
        st_ref[h] = st[h][nb]
    for h in heads:
        sl = sls[h]
        inter = [_mm_nt(q_dec[c * blk:(c + 1) * blk, sl], st[h][c]) for c in range(nb)]
        o = o_intra[h] + jnp.concatenate(inter, axis=0)
        o = _rms(o, NORM_EPS) * gn
        ogh = og[:, sl]
        o_ref[:, sl] = (o * (ogh * _sigmoid(ogh))).astype(o_ref.dtype)


def _hgrn(proj, bsz, seq, lb, gn_g):
    n = HGRN_STEP
    ns = seq // n
    W = HGRN_WIDTH
    half = W // 2
    first = (3 * RWKV_WIDTH + LORA_PAD) // half
    col = lambda j: pl.BlockSpec((n, half), lambda b, c: (b * ns + c, first + j))
    return pl.pallas_call(
        _hgrn_kernel,
        grid=(bsz, ns),
        in_specs=[col(j) for j in range(8)] + [
                  pl.BlockSpec((1, W), lambda b, c: (0, 0)),
                  pl.BlockSpec((1, HGRN_HEAD), lambda b, c: (0, 0))],
        out_specs=pl.BlockSpec((n, W), lambda b, c: (b * ns + c, 0)),
        out_shape=jax.ShapeDtypeStruct((bsz * seq, W), BF16),
        scratch_shapes=[pltpu.VMEM((W // HGRN_HEAD, HGRN_HEAD, HGRN_HEAD), F32)],
        compiler_params=_cp(("arbitrary", "arbitrary")),
        name="hgrn2",
    )(*([proj] * 8), lb.reshape(1, W), gn_g.reshape(1, HGRN_HEAD))


def _mixout_kernel(ya_ref, yb_ref, *refs):
    ga_refs, gb_refs = refs[:MIX_GATE_BLOCKS], refs[MIX_GATE_BLOCKS:2 * MIX_GATE_BLOCKS]
    pa_ref, pb_ref, wo_ref, x_ref, g_ref, o_ref = refs[2 * MIX_GATE_BLOCKS:]
    wc = 2 * ga_refs[0].shape[1]
    ya = ya_ref[...]
    yb = yb_ref[...]
    pair = lambda rs, c: jnp.concatenate([rs[2 * c][...], rs[2 * c + 1][...]], axis=1).astype(F32)
    mix = None
    for c in range(MIX_GATE_BLOCKS // 2):
        cs = slice(c * wc, (c + 1) * wc)
        ma = jnp.dot(ya, pa_ref[:, cs], preferred_element_type=F32)
        mb = jnp.dot(yb, pb_ref[:, cs], preferred_element_type=F32)
        part = (_sigmoid(pair(ga_refs, c)) * ma + _sigmoid(pair(gb_refs, c)) * mb).astype(BF16)
        term = jnp.dot(part, wo_ref[cs, :], preferred_element_type=F32)
        mix = term if mix is None else mix + term
    o_ref[...] = x_ref[...] + g_ref[0] * mix


def _mixout(ya, yb, proj, pa, pb, wo, xf, g1, seq):
    t, w = ya.shape
    d = pa.shape[1]
    tm = min(MIX_TM, seq)
    per_b = seq // tm
    wc = d // MIX_GATE_BLOCKS
    ga0 = (3 * RWKV_WIDTH + LORA_PAD + 4 * HGRN_WIDTH) // wc
    gate = lambda j: pl.BlockSpec((tm, wc), lambda i: (i, ga0 + j))
    const = lambda a: pl.BlockSpec(a.shape, lambda i: (0, 0), pipeline_mode=pl.Buffered(1))
    return pl.pallas_call(
        _mixout_kernel,
        grid=(t // tm,),
        in_specs=[pl.BlockSpec((tm, w), lambda i: (i, 0)),
                  pl.BlockSpec((tm, w), lambda i: (i, 0))] + [gate(j) for j in range(2 * MIX_GATE_BLOCKS)] + [
                  const(pa), const(pb), const(wo),
                  pl.BlockSpec((tm, d), lambda i: (i, 0)),
                  pl.BlockSpec((1, 1, d), lambda i: (i // per_b, 0, 0))],
        out_specs=pl.BlockSpec((tm, d), lambda i: (i, 0)),
        out_shape=jax.ShapeDtypeStruct((t, d), F32),
        compiler_params=_cp(("arbitrary",)),
        name="mixout",
    )(ya, yb, *([proj] * (2 * MIX_GATE_BLOCKS)), pa, pb, wo, xf, g1)


def _route_kernel(x_ref, g_ref, sc_ref, sh_ref, rwh_ref, rwm_ref, rb_ref, h_ref, idx_ref, gate_ref, rank_ref,
                  cnt_ref, carry_ref):
    tb = x_ref.shape[0]

    @pl.when(pl.program_id(0) == 0)
    def _():
        carry_ref[...] = jnp.zeros_like(carry_ref)

    h = (_rms(x_ref[...], NORM_EPS) * g_ref[...]) * (1.0 + sc_ref[0]) + sh_ref[0]
    h_ref[...] = h
    h_hi, h_mid, _ = _split3(h)
    d = lambda a, b: jnp.dot(a, b, preferred_element_type=F32)
    logits = d(h_hi, rwh_ref[...]) + (d(h_hi, rwm_ref[...]) + d(h_mid, rwh_ref[...])) + rb_ref[...]

    lane = lax.broadcasted_iota(jnp.int32, logits.shape, 1).astype(F32)
    work = logits
    vals, idxs, sels = [], [], []
    for _ in range(TOP_K):
        m = jnp.max(work, axis=-1, keepdims=True)
        idx = jnp.min(jnp.where(work == m, lane, float(N_EXPERTS)), axis=-1, keepdims=True)
        sel = lane == idx
        vals.append(m)
        idxs.append(idx)
        sels.append(sel)
        work = jnp.where(sel, -jnp.inf, work)
    exps = [jnp.exp(v - vals[0]) for v in vals]
    den = exps[0] + exps[1] + exps[2] + exps[3]

    onehot = jnp.zeros(logits.shape, F32)
    for sel in sels:
        onehot = onehot + jnp.where(sel, 1.0, 0.0)
    ti = lax.broadcasted_iota(jnp.int32, (tb, tb), 0)
    tj = lax.broadcasted_iota(jnp.int32, (tb, tb), 1)
    below = jnp.where(tj < ti, 1.0, 0.0).astype(BF16)
    before = jnp.dot(below, onehot.astype(BF16), preferred_element_type=F32) + carry_ref[...]
    carry_ref[...] = carry_ref[...] + jnp.sum(onehot, axis=0, keepdims=True)
    cnt_ref[...] = carry_ref[...]

    out_lane = lax.broadcasted_iota(jnp.int32, (tb, LANES), 1)
    idx_out = jnp.zeros((tb, LANES), jnp.int32)
    rank_out = jnp.zeros((tb, LANES), jnp.int32)
    gate_out = jnp.zeros((tb, LANES), F32)
    for j in range(TOP_K):
        rank = jnp.sum(jnp.where(sels[j], before, 0.0), axis=-1, keepdims=True)
        idx_out = jnp.where(out_lane == j, idxs[j].astype(jnp.int32), idx_out)
        rank_out = jnp.where(out_lane == j, rank.astype(jnp.int32), rank_out)
        gate_out = jnp.where(out_lane == j, exps[j] / den, gate_out)
    idx_ref[...] = idx_out
    rank_ref[...] = rank_out
    gate_ref[...] = gate_out


def _route(x1, g, sc, sh, rw, rb, seq):
    t, d = x1.shape
    tb = min(ROUTE_TB, seq)
    per_b = seq // tb
    e = rw.shape[1]
    rw_hi = rw.astype(BF16)
    rw_mid = (rw - rw_hi.astype(F32)).astype(BF16)
    wide = lambda dt: jax.ShapeDtypeStruct((t, LANES), dt)
    return pl.pallas_call(
        _route_kernel,
        grid=(t // tb,),
        in_specs=[pl.BlockSpec((tb, d), lambda i: (i, 0)),
                  pl.BlockSpec((1, d), lambda i: (0, 0)),
                  pl.BlockSpec((1, 1, d), lambda i: (i // per_b, 0, 0)),
                  pl.BlockSpec((1, 1, d), lambda i: (i // per_b, 0, 0)),
                  pl.BlockSpec((d, e), lambda i: (0, 0)),
                  pl.BlockSpec((d, e), lambda i: (0, 0)),
                  pl.BlockSpec((1, e), lambda i: (0, 0))],
        out_specs=[pl.BlockSpec((tb, d), lambda i: (i, 0)),
                   pl.BlockSpec((tb, LANES), lambda i: (i, 0)),
                   pl.BlockSpec((tb, LANES), lambda i: (i, 0)),
                   pl.BlockSpec((tb, LANES), lambda i: (i, 0)),
                   pl.BlockSpec((1, e), lambda i: (0, 0))],
        out_shape=[jax.ShapeDtypeStruct((t, d), F32), wide(jnp.int32), wide(F32), wide(jnp.int32),
                   jax.ShapeDtypeStruct((1, e), F32)],
        scratch_shapes=[pltpu.VMEM((1, e), F32)],
        compiler_params=_cp(("arbitrary",)),
        name="route",
    )(x1, g.reshape(1, d), sc, sh, rw_hi, rw_mid, rb.reshape(1, e))


def _scatter_kernel(dest_ref, zflag_ref, h_ref, xs_ref, zbuf, sem, zsem):
    tb = h_ref.shape[0]
    zrows = zbuf.shape[0]
    base = pl.program_id(0) * tb

    @pl.when(pl.program_id(0) == 0)
    def _():
        zbuf[...] = jnp.zeros_like(zbuf)

        def zcopy(b):
            return pltpu.make_async_copy(zbuf, xs_ref.at[pl.ds(pl.multiple_of(b * zrows, zrows), zrows), :], zsem)

        def zstart(b, carry):
            @pl.when(zflag_ref[b] != 0)
            def _():
                zcopy(b).start()
            return carry

        def zwait(b, carry):
            @pl.when(zflag_ref[b] != 0)
            def _():
                zcopy(b).wait()
            return carry

        n_blk = xs_ref.shape[0] // zrows
        lax.fori_loop(0, n_blk, zstart, 0)
        lax.fori_loop(0, n_blk, zwait, 0)

    def copy(t, j):
        d = dest_ref[(base + t) * TOP_K + j]
        return pltpu.make_async_copy(h_ref.at[pl.ds(t, 1), :], xs_ref.at[pl.ds(d, 1), :], sem)

    def start(t, carry):
        for j in range(TOP_K):
            copy(t, j).start()
        return carry

    lax.fori_loop(0, tb, start, 0, unroll=ROW_DMA_UNROLL)
    for j in range(TOP_K):
        pltpu.make_async_copy(h_ref, xs_ref.at[pl.ds(0, tb), :], sem).wait()


def _scatter_rows(dest_flat, zero_flag, h2, n_rows):
    t, d = h2.shape
    tb = min(SCATTER_TB, t)
    return pl.pallas_call(
        _scatter_kernel,
        grid_spec=pltpu.PrefetchScalarGridSpec(
            num_scalar_prefetch=2,
            grid=(t // tb,),
            in_specs=[pl.BlockSpec((tb, d), lambda i, dest, zf: (i, 0))],
            out_specs=pl.BlockSpec(memory_space=pl.ANY),
            scratch_shapes=[pltpu.VMEM((MOE_TM // 2, d), F32), pltpu.SemaphoreType.DMA(()),
                            pltpu.SemaphoreType.DMA(())],
        ),
        out_shape=jax.ShapeDtypeStruct((n_rows, d), F32),
        compiler_params=_cp(("arbitrary",)),
        name="scatter_rows",
    )(dest_flat, zero_flag, h2)


def _expert_kernel(be_ref, nu_ref, rows_ref, bgu_ref, bd_ref, xs_hbm, wgu_hbm, wd_hbm, ys_hbm,
                   xbuf, xb_ref, act_ref, wbuf, obuf, sem, xsem, osem):
    i = pl.program_id(0)
    nt, tm, tn = act_ref.shape
    sub = MOE_SUB
    n_slot, de, tnb = wbuf.shape
    ntb = obuf.shape[1] // tnb
    n_steps = nt + ntb
    look = n_slot - 1
    nu = nu_ref[0]
    live = i < nu

    def tile_copies(blk, step, slot):
        e = be_ref[blk]
        if step < nt:
            return [pltpu.make_async_copy(wgu_hbm.at[e, :, half * de + step * tn:half * de + (step + 1) * tn],
                                          wbuf.at[slot, :, half * tn:(half + 1) * tn], sem.at[slot])
                    for half in range(2)]
        colb = (step - nt) * tnb
        return [pltpu.make_async_copy(wd_hbm.at[e, :, colb:colb + tnb], wbuf.at[slot], sem.at[slot])]

    def request(blk, sub_step):
        blk = blk + sub_step // n_steps
        step = sub_step % n_steps

        @pl.when(blk < nu)
        def _():
            for cp in tile_copies(blk, step, (blk * n_steps + step) % n_slot):
                cp.start()

    def by_size(blk, run):
        n_sub = (rows_ref[blk] + (sub - 1)) // sub
        for v in range(1, tm // sub + 1):
            @pl.when(n_sub == v)
            def _():
                run(v * sub)

    def x_copy(blk, rows):
        r0 = pl.multiple_of(blk * tm, tm)
        return pltpu.make_async_copy(xs_hbm.at[pl.ds(r0, rows), :], xbuf.at[0:rows], xsem)

    def o_copy(blk):
        return pltpu.make_async_copy(obuf, ys_hbm.at[pl.ds(pl.multiple_of(blk * tm, tm), tm), :], osem)

    @pl.when(i == 0)
    def _():
        obuf[...] = jnp.zeros_like(obuf)
        by_size(i, lambda rows: x_copy(i, rows).start())
        for sub_step in range(look):
            request(i, sub_step)

    @pl.when(live)
    def _():
        by_size(i, lambda rows: x_copy(i, rows).wait())

        def cast(rows):
            xb_ref[0:rows] = xbuf[0:rows].astype(BF16)
        by_size(i, cast)

        @pl.when(i + 1 < nu)
        def _():
            by_size(i + 1, lambda rows: x_copy(i + 1, rows).start())

        for n in range(n_steps):
            request(i, n + look)
            slot = (i * n_steps + n) % n_slot
            for cp in tile_copies(i, n, slot):
                cp.wait()
            if n < nt:
                def run(rows, n=n, slot=slot):
                    xb = xb_ref[0:rows]
                    w = wbuf[slot]
                    glu = jnp.dot(xb, w[:, 0:tn].astype(BF16), preferred_element_type=F32) + bgu_ref[0, n]
                    lin = jnp.dot(xb, w[:, tn:2 * tn].astype(BF16), preferred_element_type=F32) + bgu_ref[0, n + nt]
                    glu = jnp.minimum(glu, SWIGLU_LIMIT)
                    lin = jnp.clip(lin, -SWIGLU_LIMIT, SWIGLU_LIMIT)
                    act_ref[n, 0:rows] = (glu * _sigmoid(SWIGLU_ALPHA * glu) * (lin + 1.0)).astype(BF16)
            else:
                if n == nt:
                    @pl.when(i > 0)
                    def _():
                        o_copy(i - 1).wait()

                def run(rows, m=n - nt, slot=slot):
                    wd = wbuf[slot]
                    y = jnp.dot(act_ref[0, 0:rows], wd[0:tn].astype(BF16), preferred_element_type=F32)
                    for k in range(1, nt):
                        y = y + jnp.dot(act_ref[k, 0:rows], wd[k * tn:(k + 1) * tn].astype(BF16),
                                        preferred_element_type=F32)
                    obuf[0:rows, m * tnb:(m + 1) * tnb] = y + bd_ref[0, m]
            by_size(i, run)
        o_copy(i).start()

    @pl.when(jnp.logical_not(live))
    def _():
        @pl.when(i > 0)
        def _():
            o_copy(i - 1).wait()

        @pl.when(i == nu)
        def _():
            obuf[...] = jnp.zeros_like(obuf)
        o_copy(i).start()

    @pl.when(i == pl.num_programs(0) - 1)
    def _():
        o_copy(i).wait()


def _experts(blk_expert, n_used, blk_rows, xs, w_gu, b_gu, w_down, b_down):
    n_rows, d = xs.shape
    tm, tn, tnb = MOE_TM, MOE_TN, MOE_TNB
    nb = n_rows // tm
    e, _, two_de = w_gu.shape
    de = two_de // 2
    nt = de // tn
    ntb = d // tnb
    assert 2 * tn == tnb and de == d
    b_gu4 = b_gu.reshape(e, 2 * nt, 1, tn)
    b_d4 = b_down.reshape(e, ntb, 1, tnb)
    return pl.pallas_call(
        _expert_kernel,
        grid_spec=pltpu.PrefetchScalarGridSpec(
            num_scalar_prefetch=3,
            grid=(nb,),
            in_specs=[pl.BlockSpec((1, 2 * nt, 1, tn), lambda i, be, nu, br: (be[i], 0, 0, 0)),
                      pl.BlockSpec((1, ntb, 1, tnb), lambda i, be, nu, br: (be[i], 0, 0, 0)),
                      pl.BlockSpec(memory_space=pl.ANY),
                      pl.BlockSpec(memory_space=pl.ANY),
                      pl.BlockSpec(memory_space=pl.ANY)],
            out_specs=pl.BlockSpec(memory_space=pl.ANY),
            scratch_shapes=[pltpu.VMEM((tm, d), F32), pltpu.VMEM((tm, d), BF16), pltpu.VMEM((nt, tm, tn), BF16),
                            pltpu.VMEM((MOE_RING, de, tnb), F32), pltpu.VMEM((tm, d), F32),
                            pltpu.SemaphoreType.DMA((MOE_RING,)), pltpu.SemaphoreType.DMA(()),
                            pltpu.SemaphoreType.DMA(())],
        ),
        out_shape=jax.ShapeDtypeStruct((n_rows, d), F32),
        compiler_params=_cp(("arbitrary",)),
        name="experts",
    )(blk_expert, n_used, blk_rows, b_gu4, b_d4, xs, w_gu, w_down)


def _combine_kernel(dest_ref, ys_ref, gate_ref, x_ref, g2_ref, fg_ref, o_ref, buf_ref, sem):
    tb = x_ref.shape[0]
    i = pl.program_id(0)
    half = i % 2

    def gather(blk, slot):
        def start(t, carry):
            for j in range(TOP_K):
                d = dest_ref[(blk * tb + t) * TOP_K + j]
                pltpu.make_async_copy(ys_ref.at[pl.ds(d, 1), :], buf_ref.at[slot, j, pl.ds(t, 1), :],
                                      sem.at[slot]).start()
            return carry
        lax.fori_loop(0, tb, start, 0, unroll=ROW_DMA_UNROLL)

    @pl.when(i == 0)
    def _():
        gather(i, half)

    @pl.when(i + 1 < pl.num_programs(0))
    def _():
        gather(i + 1, 1 - half)

    for j in range(TOP_K):
        pltpu.make_async_copy(ys_ref.at[pl.ds(0, tb), :], buf_ref.at[half, j], sem.at[half]).wait()

    gate = gate_ref[...]
    ffn = gate[:, 0:1] * buf_ref[half, 0]
    for j in range(1, TOP_K):
        ffn = ffn + gate[:, j:j + 1] * buf_ref[half, j]
    x2 = x_ref[...] + g2_ref[0] * ffn
    o_ref[...] = _rms(x2, NORM_EPS) * fg_ref[...]


def _combine(dest_flat, ys, gates, x1, g2, fg, seq):
    t, d = x1.shape
    tb = min(COMBINE_TB, seq)
    per_b = seq // tb
    return pl.pallas_call(
        _combine_kernel,
        grid_spec=pltpu.PrefetchScalarGridSpec(
            num_scalar_prefetch=1,
            grid=(t // tb,),
            in_specs=[pl.BlockSpec(memory_space=pl.ANY),
                      pl.BlockSpec((tb, LANES), lambda i, dest: (i, 0)),
                      pl.BlockSpec((tb, d), lambda i, dest: (i, 0)),
                      pl.BlockSpec((1, 1, d), lambda i, dest: (i // per_b, 0, 0)),
                      pl.BlockSpec((1, d), lambda i, dest: (0, 0))],
            out_specs=pl.BlockSpec((tb, d), lambda i, dest: (i, 0)),
            scratch_shapes=[pltpu.VMEM((2, TOP_K, tb, d), F32), pltpu.SemaphoreType.DMA((2,))],
        ),
        out_shape=jax.ShapeDtypeStruct((t, d), F32),
        compiler_params=_cp(("arbitrary",)),
        name="combine",
    )(dest_flat, ys, gates, x1, g2, fg.reshape(1, d))


def _moe(x1, g, sc2, sh2, g2, router_w, router_b, w_gu, b_gu, w_down, b_down, final_g, seq):
    t, d = x1.shape
    a = t * TOP_K
    tm = MOE_TM
    nb = -(-a // tm) + N_EXPERTS
    h2, idx_w, gate_w, rank_w, counts = _route(x1, g, sc2, sh2, router_w, router_b, seq)
    idx = idx_w[:, :TOP_K]
    rank = rank_w[:, :TOP_K]
    cnt = counts.reshape(-1).astype(jnp.int32)
    padded = ((cnt + tm - 1) // tm) * tm
    pad_end = jnp.cumsum(padded)
    pad_start = pad_end - padded
    experts = jnp.arange(N_EXPERTS, dtype=jnp.int32)
    start_of = jnp.sum(jnp.where(idx[:, :, None] == experts, pad_start, 0), axis=-1)
    dest = (start_of + rank).reshape(a).astype(jnp.int32)
    n_used = (pad_end[-1] // tm).astype(jnp.int32)
    blk = jnp.arange(nb, dtype=jnp.int32)
    blk_src = jnp.clip(blk, 0, jnp.maximum(n_used - 1, 0))
    blk_expert = jnp.sum((pad_end[None, :] <= (blk_src * tm)[:, None]).astype(jnp.int32), axis=1)
    blk_expert = jnp.minimum(blk_expert, N_EXPERTS - 1)
    own = blk_expert[:, None] == experts
    rows_end = jnp.sum(jnp.where(own, pad_start + cnt, 0), axis=1)
    blk_rows = jnp.clip(rows_end - blk_src * tm, 0, tm).astype(jnp.int32)
    half = jnp.arange(2 * nb, dtype=jnp.int32)
    zero_flag = ((half // 2 >= n_used) | ((half + 1) * (tm // 2) > rows_end[half // 2])).astype(jnp.int32)
    xs = _scatter_rows(dest, zero_flag, h2, nb * tm)
    ys = _experts(blk_expert, n_used.reshape(1), blk_rows, xs, w_gu, b_gu, w_down, b_down)
    return _combine(dest, ys, gate_w, x1, g2, final_g, seq)


def kernel(x, c, ada_w, ada_b, norm1_g, norm2_g, w_in, rwkv_mu, rwkv_w0, rwkv_w2, rwkv_a0, rwkv_a2, rwkv_g2,
           rwkv_k_k, rwkv_k_a, rwkv_r_k, rwkv_gn_g, rwkv_gn_b, hgrn_lb_logits, hgrn_gn_g, proj_a, proj_b, w_out,
           router_w, router_b, exp_w_gate_up, exp_b_gate_up, exp_w_down, exp_b_down, final_norm_g):
    bsz, seq, d = x.shape
    depth = ada_w.shape[0]
    lower_bounds = jnp.cumsum(jax.nn.softmax(hgrn_lb_logits.astype(F32), axis=0), axis=0)
    xf = x.reshape(bsz * seq, d)
    rkv = 3 * RWKV_WIDTH
    rwkv_cols = rkv + LORA_COLS
    for l in range(depth):
        mod = _adaln(c, ada_w[l], ada_b[l])
        sh1, sc1, g1, sh2, sc2, g2 = [m.reshape(bsz, 1, d) for m in jnp.split(mod, 6, axis=-1)]

        wl = w_in[l]
        w_a = wl[:, :rkv + LORA_PAD].astype(BF16)
        w_b = wl[:, rwkv_cols:].astype(BF16)
        proj = _inproj(xf, norm1_g[l], sc1, sh1, w_a, w_b, seq)

        mu = rwkv_mu[l]
        mu_lo = jnp.pad(mu[rkv:], (0, LORA_PAD - LORA_COLS))
        pad_rows = lambda w, r0: jnp.zeros((LORA_PAD, RWKV_WIDTH), F32).at[r0:r0 + w.shape[0]].set(w).astype(BF16)
        ya = _rwkv(proj, bsz, seq, mu[:rkv], mu_lo, rwkv_w0[l], rwkv_a0[l], rwkv_k_k[l], rwkv_k_a[l],
                   rwkv_r_k[l].reshape(-1), rwkv_gn_g[l], rwkv_gn_b[l],
                   pad_rows(rwkv_w2[l], 0), pad_rows(rwkv_a2[l], DECAY_LORA),
                   pad_rows(rwkv_g2[l], DECAY_LORA + AAA_LORA))
        yb = _hgrn(proj, bsz, seq, lower_bounds[l], hgrn_gn_g[l])
        x1 = _mixout(ya, yb, proj, proj_a[l].astype(BF16), proj_b[l].astype(BF16), w_out[l].astype(BF16), xf, g1, seq)
        assert depth == 1
        out = _moe(x1, norm2_g[l], sc2, sh2, g2, router_w[l], router_b[l], exp_w_gate_up[l], exp_b_gate_up[l],
                   exp_w_down[l], exp_b_down[l], final_norm_g, seq)
    return out.reshape(bsz, seq, d)
```

```python
import functools

import jax
import jax.numpy as jnp
from jax import lax
from jax.experimental import pallas as pl
from jax.experimental.pallas import tpu as pltpu

F32 = jnp.float32
BF16 = jnp.bfloat16

D_MODEL = 2048
RWKV_WIDTH = 1024
RWKV_HEAD = 64
DECAY_LORA = 64
AAA_LORA = 64
GATE_LORA = 160
LORA_COLS = DECAY_LORA + AAA_LORA + GATE_LORA
LORA_PAD = 512
RWKV_GN_EPS = 64e-5
HGRN_WIDTH = 1024
HGRN_HEAD = 128
GLA_BLOCK = 16
N_EXPERTS = 32
TOP_K = 4
D_EXPERT = 2048
SWIGLU_ALPHA = 1.702
SWIGLU_LIMIT = 7.0
NORM_EPS = 1e-5

LANES = 128
VMEM_LIMIT = 56 * 1024 * 1024

RWKV_CHUNK = 64
RWKV_STEP_CHUNKS = 2
HGRN_STEP = 128
HGRN_STEP_CHUNKS = 2
PROJ_TM = 1024
PROJ_TN = 512
PROJ_RING = 4
NORM_ROWS = 128
MIX_TM = 256
MIX_GATE_BLOCKS = 4
MOE_TM = 512
MOE_TN = 512
MOE_TNB = 1024
MOE_RING = 3
ROUTE_TB = 256
SCATTER_TB = 512
COMBINE_TB = 256
ROW_DMA_UNROLL = 8


def _cp(sem, vmem=VMEM_LIMIT):
    return pltpu.CompilerParams(dimension_semantics=sem, vmem_limit_bytes=vmem)


def _mm(a, b):
    return jnp.dot(a.astype(BF16), b.astype(BF16), preferred_element_type=F32)


def _mm_nt(a, b):
    return lax.dot_general(a.astype(BF16), b.astype(BF16), (((1,), (1,)), ((), ())),
                           preferred_element_type=F32)


def _mm_tn(a, b):
    return lax.dot_general(a.astype(BF16), b.astype(BF16), (((0,), (0,)), ((), ())),
                           preferred_element_type=F32)


def _split3(x):
    hi = x.astype(BF16)
    r1 = x - hi.astype(F32)
    mid = r1.astype(BF16)
    lo = (r1 - mid.astype(F32)).astype(BF16)
    return hi, mid, lo


def _mm_exact_rhs(m_bf16, x):
    hi, mid, lo = _split3(x)
    d = lambda p: jnp.dot(m_bf16, p, preferred_element_type=F32)
    return d(hi) + d(mid) + d(lo)


def _sigmoid(x):
    return 1.0 / (1.0 + jnp.exp(-x))


def _softplus(x):
    return jnp.maximum(x, 0.0) + jnp.log(1.0 + jnp.exp(-jnp.abs(x)))


def _rms(x, eps):
    return x * lax.rsqrt(jnp.mean(x * x, axis=-1, keepdims=True) + eps)


def _adaln_kernel(c_ref, w_ref, b_ref, o_ref):
    c = c_ref[...]
    c_act = c * _sigmoid(c)
    o_ref[...] = jnp.dot(c_act, w_ref[...], preferred_element_type=F32,
                         precision=lax.Precision.HIGHEST) + b_ref[...]


def _adaln(c, w, b):
    bsz, d = c.shape
    n = w.shape[1]
    tn = 1024
    return pl.pallas_call(
        _adaln_kernel,
        grid=(n // tn,),
        in_specs=[pl.BlockSpec((bsz, d), lambda j: (0, 0)),
                  pl.BlockSpec((d, tn), lambda j: (0, j)),
                  pl.BlockSpec((1, tn), lambda j: (0, j))],
        out_specs=pl.BlockSpec((bsz, tn), lambda j: (0, j)),
        out_shape=jax.ShapeDtypeStruct((bsz, n), F32),
        compiler_params=_cp(("arbitrary",)),
        name="adaln",
    )(c, w, b.reshape(1, n))


def _inproj_kernel(n_a, x_ref, g_ref, sc_ref, sh_ref, wa_hbm, wb_hbm, o_hbm, h_ref, wbuf, obuf, wsem, osem):
    i = pl.program_id(0)
    tm = x_ref.shape[0]
    n_slot, _, tn = wbuf.shape
    look = n_slot - 1
    n_tiles = o_hbm.shape[1] // tn

    def w_copy(tile, slot):
        src = (wa_hbm.at[:, tile * tn:(tile + 1) * tn] if tile < n_a
               else wb_hbm.at[:, (tile - n_a) * tn:(tile - n_a + 1) * tn])
        return pltpu.make_async_copy(src, wbuf.at[slot], wsem.at[slot])

    def o_copy(tile):
        rows = pl.ds(pl.multiple_of(i * tm, tm), tm)
        return pltpu.make_async_copy(obuf.at[tile % 2], o_hbm.at[rows, tile * tn:(tile + 1) * tn], osem.at[tile % 2])

    def request(sub):
        blk = i + sub // n_tiles
        tile = sub % n_tiles

        @pl.when(blk < pl.num_programs(0))
        def _():
            w_copy(tile, (blk * n_tiles + tile) % n_slot).start()

    @pl.when(i == 0)
    def _():
        for sub in range(look):
            request(sub)

    def rows(c, carry):
        rs = pl.ds(pl.multiple_of(c * NORM_ROWS, NORM_ROWS), NORM_ROWS)
        y = _rms(x_ref[rs, :], NORM_EPS) * g_ref[...]
        h_ref[rs, :] = (y * (1.0 + sc_ref[0]) + sh_ref[0]).astype(BF16)
        return carry
    lax.fori_loop(0, tm // NORM_ROWS, rows, 0)

    for tile in range(n_tiles):
        request(tile + look)
        slot = (i * n_tiles + tile) % n_slot
        w_copy(tile, slot).wait()
        if tile >= 2:
            o_copy(tile - 2).wait()
        obuf[tile % 2] = jnp.dot(h_ref[...], wbuf[slot], preferred_element_type=F32).astype(obuf.dtype)
        o_copy(tile).start()
    for tile in range(max(n_tiles - 2, 0), n_tiles):
        o_copy(tile).wait()


def _inproj(xf, g, sc, sh, w_a, w_b, seq):
    t, d = xf.shape
    tm = min(PROJ_TM, seq)
    tn = PROJ_TN
    n_a, n_b = w_a.shape[1] // tn, w_b.shape[1] // tn
    per_b = seq // tm
    return pl.pallas_call(
        functools.partial(_inproj_kernel, n_a),
        grid=(t // tm,),
        in_specs=[pl.BlockSpec((tm, d), lambda i: (i, 0)),
                  pl.BlockSpec((1, d), lambda i: (0, 0)),
                  pl.BlockSpec((1, 1, d), lambda i: (i // per_b, 0, 0)),
                  pl.BlockSpec((1, 1, d), lambda i: (i // per_b, 0, 0)),
                  pl.BlockSpec(memory_space=pl.ANY),
                  pl.BlockSpec(memory_space=pl.ANY)],
        out_specs=pl.BlockSpec(memory_space=pl.ANY),
        out_shape=jax.ShapeDtypeStruct((t, (n_a + n_b) * tn), BF16),
        scratch_shapes=[pltpu.VMEM((tm, d), BF16), pltpu.VMEM((PROJ_RING, d, tn), BF16),
                        pltpu.VMEM((2, tm, tn), BF16), pltpu.SemaphoreType.DMA((PROJ_RING,)),
                        pltpu.SemaphoreType.DMA((2,))],
        compiler_params=_cp(("arbitrary",)),
        name="inproj",
    )(xf, g.reshape(1, d), sc, sh, w_a, w_b)


def _rwkv_chunk(rows_sl, r_ref, k_ref, v_ref, lo_ref, mu_ref, mulo_ref, w0_ref, a0_ref, kk_ref, ka_ref, rk_ref,
                gng_ref, gnb_ref, w2_ref, a2_ref, g2_ref, o_ref,
                st_ref, cr_ref, ck_ref, cv_ref, clo_ref):
    L = RWKV_CHUNK
    H2 = 2 * L
    n_pair = RWKV_WIDTH // LANES

    def shift(x_ref, mu, carry_ref):
        x = x_ref[rows_sl, :].astype(F32)
        rows = lax.broadcasted_iota(jnp.int32, x.shape, 0)
        prev = jnp.where(rows == 0, carry_ref[...], pltpu.roll(x, 1, axis=0))
        carry_ref[...] = x[L - 1:L, :]
        return x + (prev - x) * mu

    mu = mu_ref[...]
    r = shift(r_ref, mu[:, 0:RWKV_WIDTH], cr_ref)
    k = shift(k_ref, mu[:, RWKV_WIDTH:2 * RWKV_WIDTH], ck_ref)
    v = shift(v_ref, mu[:, 2 * RWKV_WIDTH:3 * RWKV_WIDTH], cv_ref)
    lo = shift(lo_ref, mulo_ref[...], clo_ref)

    wpre = w0_ref[...] + _mm(jnp.tanh(lo), w2_ref[...])
    lw = -jnp.exp(-_softplus(-wpre) - 0.5)
    a = _sigmoid(a0_ref[...] + _mm(lo, a2_ref[...]))
    g = _mm(_sigmoid(lo), g2_ref[...])

    ti = lax.broadcasted_iota(jnp.int32, (L, L), 0)
    tj = lax.broadcasted_iota(jnp.int32, (L, L), 1)
    tri = jnp.where(tj <= ti, 1.0, 0.0).astype(BF16)
    cl = _mm_exact_rhs(tri, lw)
    cl_last = cl[L - 1:L, :]
    e_pos = jnp.exp(cl)
    e_neg = jnp.exp(-cl)
    e_prev = jnp.exp(cl - lw)
    e_end = jnp.exp(cl_last - cl)
    p_last = jnp.exp(cl_last)

    kk = k * kk_ref[...]
    k2 = k * (1.0 + (a - 1.0) * ka_ref[...])
    f_b = a * e_neg
    f_bend = a * e_end
    rk = rk_ref[...]
    gng = gng_ref[...]
    gnb = gnb_ref[...]

    lane = lax.broadcasted_iota(jnp.int32, (L, LANES), 1)
    lo_half = lane < RWKV_HEAD
    si = lax.broadcasted_iota(jnp.int32, (H2, H2), 0)
    sj = lax.broadcasted_iota(jnp.int32, (H2, H2), 1)
    same = (si // L) == (sj // L)
    m_strict = same & (sj < si)
    m_incl = same & (sj <= si)
    eye = si == sj
    own = (si // L) == (sj // RWKV_HEAD)

    def stack(x):
        return jnp.concatenate([jnp.where(lo_half, x, 0.0), jnp.where(lo_half, 0.0, x)], axis=0)

    def rep(x):
        return jnp.concatenate([x, x], axis=0)

    pairs = range(n_pair)
    sls = [slice(p * LANES, (p + 1) * LANES) for p in pairs]
    at, bt, bend, kt, kend, rt, v_s, bonus = [], [], [], [], [], [], [], []
    for sl in sls:
        kk_s = stack(kk[:, sl])
        nrm = jnp.sqrt(jnp.sum(kk_s * kk_s, axis=-1, keepdims=True))
        kkn = kk_s / jnp.maximum(nrm, 1e-12)
        at.append(-kkn * rep(e_prev[:, sl]))
        bt.append(kkn * rep(f_b[:, sl]))
        bend.append(kkn * rep(f_bend[:, sl]))
        k2_s = stack(k2[:, sl])
        kt.append(k2_s * rep(e_neg[:, sl]))
        kend.append(k2_s * rep(e_end[:, sl]))
        r_s = stack(r[:, sl])
        rt.append(r_s * rep(e_pos[:, sl]))
        vs = stack(v[:, sl])
        v_s.append(vs)
        bonus.append(jnp.sum(r_s * k2_s * rk[:, sl], axis=-1, keepdims=True) * vs)

    gm = [_mm_nt(jnp.concatenate([at[p], rt[p]], axis=0), jnp.concatenate([bt[p], kt[p]], axis=0))
          for p in pairs]
    a_ab = [jnp.where(m_strict, gm[p][:H2, :H2], 0.0) for p in pairs]
    a_ak = [jnp.where(m_strict, gm[p][:H2, H2:], 0.0) for p in pairs]
    a_rb = [jnp.where(m_incl, gm[p][H2:, :H2], 0.0) for p in pairs]
    a_rk = [jnp.where(m_incl, gm[p][H2:, H2:], 0.0) for p in pairs]

    ident = jnp.where(eye, 1.0, 0.0)
    tinv = [ident + a_ab[p] for p in pairs]
    pw = a_ab
    akv = [_mm(a_ak[p], v_s[p]) for p in pairs]
    ark_v = [_mm(a_rk[p], v_s[p]) for p in pairs]
    kend_v = [_mm_tn(kend[p], v_s[p]) for p in pairs]
    for _ in range(max(1, (L - 1).bit_length() - 1)):
        pw = [_mm(pw[p], pw[p]) for p in pairs]
        tinv = [tinv[p] + _mm(tinv[p], pw[p]) for p in pairs]

    tx = [_mm(tinv[p], jnp.concatenate([at[p], akv[p]], axis=1)) for p in pairs]
    y2 = [_mm(a_rb[p], tx[p]) for p in pairs]
    btx = [_mm_tn(bend[p], tx[p]) for p in pairs]
    s0 = [st_ref[p] for p in pairs]
    y = [_mm(rt[p] + y2[p][:, :LANES], s0[p]) + (y2[p][:, LANES:] + ark_v[p]) for p in pairs]
    for p in pairs:
        m_mat = jnp.where(eye, jnp.broadcast_to(p_last[:, sls[p]], (H2, LANES)), 0.0) + btx[p][:, :LANES]
        st_ref[p] = _mm(m_mat, s0[p]) + (btx[p][:, LANES:] + kend_v[p])

    for p in pairs:
        sl = sls[p]
        mean = jnp.sum(y[p], axis=-1, keepdims=True) * (1.0 / RWKV_HEAD)
        dlt = jnp.where(own, y[p] - mean, 0.0)
        var = jnp.sum(dlt * dlt, axis=-1, keepdims=True) * (1.0 / RWKV_HEAD)
        yn = dlt * lax.rsqrt(var + RWKV_GN_EPS) * gng[:, sl] + jnp.where(own, gnb[:, sl], 0.0)
        ys = yn + bonus[p]
        o_ref[rows_sl, sl] = ((ys[:L] + ys[L:]) * g[:, sl]).astype(o_ref.dtype)


def _rwkv_kernel(*refs):
    st_ref, cr_ref, ck_ref, cv_ref, clo_ref = refs[-5:]

    @pl.when(pl.program_id(1) == 0)
    def _():
        for ref in (st_ref, cr_ref, ck_ref, cv_ref, clo_ref):
            ref[...] = jnp.zeros_like(ref)

    for c in range(RWKV_STEP_CHUNKS):
        _rwkv_chunk(slice(c * RWKV_CHUNK, (c + 1) * RWKV_CHUNK), *refs)


def _rwkv(proj, bsz, seq, mu_rkv, mu_lo, w0, a0, k_k, k_a, r_k, gn_g, gn_b, w2p, a2p, g2p):
    L = RWKV_CHUNK * RWKV_STEP_CHUNKS
    nc = seq // L
    W = RWKV_WIDTH
    row = lambda n: pl.BlockSpec((1, n), lambda b, c: (0, 0))
    full = lambda a: pl.BlockSpec(a.shape, lambda b, c: (0, 0))
    lo_blk = 3 * W // LORA_PAD
    v1 = lambda a: a.reshape(1, -1)
    return pl.pallas_call(
        _rwkv_kernel,
        grid=(bsz, nc),
        in_specs=[pl.BlockSpec((L, W), lambda b, c: (b * nc + c, 0)),
                  pl.BlockSpec((L, W), lambda b, c: (b * nc + c, 1)),
                  pl.BlockSpec((L, W), lambda b, c: (b * nc + c, 2)),
                  pl.BlockSpec((L, LORA_PAD), lambda b, c: (b * nc + c, lo_blk)),
                  row(3 * W), row(LORA_PAD), row(W), row(W), row(W), row(W), row(W), row(W), row(W),
                  full(w2p), full(a2p), full(g2p)],
        out_specs=pl.BlockSpec((L, W), lambda b, c: (b * nc + c, 0)),
        out_shape=jax.ShapeDtypeStruct((bsz * seq, W), BF16),
        scratch_shapes=[pltpu.VMEM((W // LANES, LANES, LANES), F32),
                        pltpu.VMEM((1, W), F32), pltpu.VMEM((1, W), F32), pltpu.VMEM((1, W), F32),
                        pltpu.VMEM((1, LORA_PAD), F32)],
        compiler_params=_cp(("arbitrary", "arbitrary")),
        name="rwkv7",
    )(proj, proj, proj, proj, v1(mu_rkv), v1(mu_lo), v1(w0), v1(a0), v1(k_k), v1(k_a), v1(r_k), v1(gn_g),
      v1(gn_b), w2p, a2p, g2p)


def _hgrn_chunk(rows_sl, qa_ref, qb_ref, fa_ref, fb_ref, ia_ref, ib_ref, oa_ref, ob_ref, lb_ref, gn_ref, o_ref,
                st_ref):
    n = HGRN_STEP
    wide = lambda a_ref, b_ref: jnp.concatenate([a_ref[rows_sl, :], b_ref[rows_sl, :]], axis=1).astype(F32)
    blk = GLA_BLOCK
    nb = n // blk
    n_head = HGRN_WIDTH // HGRN_HEAD

    lb = lb_ref[...]
    qf = wide(qa_ref, qb_ref)
    q = qf * _sigmoid(qf)
    forget = lb + (1.0 - lb) * _sigmoid(wide(fa_ref, fb_ref))
    log_f = jnp.log(forget)
    kx = 1.0 - forget
    vv = wide(ia_ref, ib_ref)
    og = wide(oa_ref, ob_ref)

    ti = lax.broadcasted_iota(jnp.int32, (n, n), 0)
    tj = lax.broadcasted_iota(jnp.int32, (n, n), 1)
    same = (ti // blk) == (tj // blk)
    causal = same & (tj <= ti)
    b = _mm_exact_rhs(jnp.where(causal, 1.0, 0.0).astype(BF16), log_f)
    b_last = _mm_exact_rhs(jnp.where(same, 1.0, 0.0).astype(BF16), log_f)
    q_dec = q * jnp.exp(b)
    k_dec = kx * jnp.exp(-b)
    k_end = kx * jnp.exp(b_last - b)
    d_end = jnp.exp(b_last)
    rows = lax.broadcasted_iota(jnp.int32, (n, HGRN_HEAD), 0)
    gn = gn_ref[...]

    heads = range(n_head)
    sls = [slice(h * HGRN_HEAD, (h + 1) * HGRN_HEAD) for h in heads]
    attn = [jnp.where(causal, _mm_nt(q_dec[:, sl], k_dec[:, sl]), 0.0) for sl in sls]
    o_intra = [_mm(attn[h], vv[:, sls[h]]) for h in heads]
    vt = [vv[:, sl].T for sl in sls]
    blk_mask = [(rows // blk) == c for c in range(nb)]
    kv_all = [_mm(vt[h], jnp.concatenate([jnp.where(blk_mask[c], k_end[:, sls[h]], 0.0) for c in range(nb)],
                                         axis=1)) for h in heads]
    kv_t = [[kv_all[h][:, c * HGRN_HEAD:(c + 1) * HGRN_HEAD] for c in range(nb)] for h in heads]
    st = [[st_ref[h]] for h in heads]
    for h in heads:
        for c in range(nb):
            st[h].append(st[h][c] * d_end[c * blk:c * blk + 1, sls[h]] + kv_t[h][c])
        st_ref[h] = st[h][nb]
    for h in heads:
        sl = sls[h]
        inter = [_mm_nt(q_dec[c * blk:(c + 1) * blk, sl], st[h][c]) for c in range(nb)]
        o = o_intra[h] + jnp.concatenate(inter, axis=0)
        o = _rms(o, NORM_EPS) * gn
        ogh = og[:, sl]
        o_ref[rows_sl, sl] = (o * (ogh * _sigmoid(ogh))).astype(o_ref.dtype)


def _hgrn_kernel(*refs):
    st_ref = refs[-1]

    @pl.when(pl.program_id(1) == 0)
    def _():
        st_ref[...] = jnp.zeros_like(st_ref)

    for c in range(HGRN_STEP_CHUNKS):
        _hgrn_chunk(slice(c * HGRN_STEP, (c + 1) * HGRN_STEP), *refs)


def _hgrn(proj, bsz, seq, lb, gn_g):
    n = HGRN_STEP * HGRN_STEP_CHUNKS
    ns = seq // n
    W = HGRN_WIDTH
    half = W // 2
    first = (3 * RWKV_WIDTH + LORA_PAD) // half
    col = lambda j: pl.BlockSpec((n, half), lambda b, c: (b * ns + c, first + j))
    return pl.pallas_call(
        _hgrn_kernel,
        grid=(bsz, ns),
        in_specs=[col(j) for j in range(8)] + [
                  pl.BlockSpec((1, W), lambda b, c: (0, 0)),
                  pl.BlockSpec((1, HGRN_HEAD), lambda b, c: (0, 0))],
        out_specs=pl.BlockSpec((n, W), lambda b, c: (b * ns + c, 0)),
        out_shape=jax.ShapeDtypeStruct((bsz * seq, W), BF16),
        scratch_shapes=[pltpu.VMEM((W // HGRN_HEAD, HGRN_HEAD, HGRN_HEAD), F32)],
        compiler_params=_cp(("arbitrary", "arbitrary")),
        name="hgrn2",
    )(*([proj] * 8), lb.reshape(1, W), gn_g.reshape(1, HGRN_HEAD))


def _mixout_kernel(ya_ref, yb_ref, *refs):
    ga_refs, gb_refs = refs[:MIX_GATE_BLOCKS], refs[MIX_GATE_BLOCKS:2 * MIX_GATE_BLOCKS]
    pa_ref, pb_ref, wo_ref, x_ref, g_ref, o_ref = refs[2 * MIX_GATE_BLOCKS:]
    wc = 2 * ga_refs[0].shape[1]
    ya = ya_ref[...]
    yb = yb_ref[...]
    pair = lambda rs, c: jnp.concatenate([rs[2 * c][...], rs[2 * c + 1][...]], axis=1).astype(F32)
    mix = None
    for c in range(MIX_GATE_BLOCKS // 2):
        cs = slice(c * wc, (c + 1) * wc)
        ma = jnp.dot(ya, pa_ref[:, cs], preferred_element_type=F32)
        mb = jnp.dot(yb, pb_ref[:, cs], preferred_element_type=F32)
        part = (_sigmoid(pair(ga_refs, c)) * ma + _sigmoid(pair(gb_refs, c)) * mb).astype(BF16)
        term = jnp.dot(part, wo_ref[cs, :], preferred_element_type=F32)
        mix = term if mix is None else mix + term
    o_ref[...] = x_ref[...] + g_ref[0] * mix


def _mixout(ya, yb, proj, pa, pb, wo, xf, g1, seq):
    t, w = ya.shape
    d = pa.shape[1]
    tm = min(MIX_TM, seq)
    per_b = seq // tm
    wc = d // MIX_GATE_BLOCKS
    ga0 = (3 * RWKV_WIDTH + LORA_PAD + 4 * HGRN_WIDTH) // wc
    gate = lambda j: pl.BlockSpec((tm, wc), lambda i: (i, ga0 + j))
    const = lambda a: pl.BlockSpec(a.shape, lambda i: (0, 0), pipeline_mode=pl.Buffered(1))
    return pl.pallas_call(
        _mixout_kernel,
        grid=(t // tm,),
        in_specs=[pl.BlockSpec((tm, w), lambda i: (i, 0)),
                  pl.BlockSpec((tm, w), lambda i: (i, 0))] + [gate(j) for j in range(2 * MIX_GATE_BLOCKS)] + [
                  const(pa), const(pb), const(wo),
                  pl.BlockSpec((tm, d), lambda i: (i, 0)),
                  pl.BlockSpec((1, 1, d), lambda i: (i // per_b, 0, 0))],
        out_specs=pl.BlockSpec((tm, d), lambda i: (i, 0)),
        out_shape=jax.ShapeDtypeStruct((t, d), F32),
        compiler_params=_cp(("arbitrary",)),
        name="mixout",
    )(ya, yb, *([proj] * (2 * MIX_GATE_BLOCKS)), pa, pb, wo, xf, g1)


def _route_kernel(x_ref, g_ref, sc_ref, sh_ref, rwh_ref, rwm_ref, rb_ref, h_ref, idx_ref, gate_ref, rank_ref,
                  cnt_ref, carry_ref):
    tb = x_ref.shape[0]

    @pl.when(pl.program_id(0) == 0)
    def _():
        carry_ref[...] = jnp.zeros_like(carry_ref)

    h = (_rms(x_ref[...], NORM_EPS) * g_ref[...]) * (1.0 + sc_ref[0]) + sh_ref[0]
    h_ref[...] = h
    h_hi, h_mid, _ = _split3(h)
    d = lambda a, b: jnp.dot(a, b, preferred_element_type=F32)
    logits = d(h_hi, rwh_ref[...]) + (d(h_hi, rwm_ref[...]) + d(h_mid, rwh_ref[...])) + rb_ref[...]

    lane = lax.broadcasted_iota(jnp.int32, logits.shape, 1).astype(F32)
    work = logits
    vals, idxs, sels = [], [], []
    for _ in range(TOP_K):
        m = jnp.max(work, axis=-1, keepdims=True)
        idx = jnp.min(jnp.where(work == m, lane, float(N_EXPERTS)), axis=-1, keepdims=True)
        sel = lane == idx
        vals.append(m)
        idxs.append(idx)
        sels.append(sel)
        work = jnp.where(sel, -jnp.inf, work)
    exps = [jnp.exp(v - vals[0]) for v in vals]
    den = exps[0] + exps[1] + exps[2] + exps[3]

    onehot = jnp.zeros(logits.shape, F32)
    for sel in sels:
        onehot = onehot + jnp.where(sel, 1.0, 0.0)
    ti = lax.broadcasted_iota(jnp.int32, (tb, tb), 0)
    tj = lax.broadcasted_iota(jnp.int32, (tb, tb), 1)
    below = jnp.where(tj < ti, 1.0, 0.0).astype(BF16)
    before = jnp.dot(below, onehot.astype(BF16), preferred_element_type=F32) + carry_ref[...]
    carry_ref[...] = carry_ref[...] + jnp.sum(onehot, axis=0, keepdims=True)
    cnt_ref[...] = carry_ref[...]

    out_lane = lax.broadcasted_iota(jnp.int32, (tb, LANES), 1)
    idx_out = jnp.zeros((tb, LANES), jnp.int32)
    rank_out = jnp.zeros((tb, LANES), jnp.int32)
    gate_out = jnp.zeros((tb, LANES), F32)
    for j in range(TOP_K):
        rank = jnp.sum(jnp.where(sels[j], before, 0.0), axis=-1, keepdims=True)
        idx_out = jnp.where(out_lane == j, idxs[j].astype(jnp.int32), idx_out)
        rank_out = jnp.where(out_lane == j, rank.astype(jnp.int32), rank_out)
        gate_out = jnp.where(out_lane == j, exps[j] / den, gate_out)
    idx_ref[...] = idx_out
    rank_ref[...] = rank_out
    gate_ref[...] = gate_out


def _route(x1, g, sc, sh, rw, rb, seq):
    t, d = x1.shape
    tb = min(ROUTE_TB, seq)
    per_b = seq // tb
    e = rw.shape[1]
    rw_hi = rw.astype(BF16)
    rw_mid = (rw - rw_hi.astype(F32)).astype(BF16)
    wide = lambda dt: jax.ShapeDtypeStruct((t, LANES), dt)
    return pl.pallas_call(
        _route_kernel,
        grid=(t // tb,),
        in_specs=[pl.BlockSpec((tb, d), lambda i: (i, 0)),
                  pl.BlockSpec((1, d), lambda i: (0, 0)),
                  pl.BlockSpec((1, 1, d), lambda i: (i // per_b, 0, 0)),
                  pl.BlockSpec((1, 1, d), lambda i: (i // per_b, 0, 0)),
                  pl.BlockSpec((d, e), lambda i: (0, 0)),
                  pl.BlockSpec((d, e), lambda i: (0, 0)),
                  pl.BlockSpec((1, e), lambda i: (0, 0))],
        out_specs=[pl.BlockSpec((tb, d), lambda i: (i, 0)),
                   pl.BlockSpec((tb, LANES), lambda i: (i, 0)),
                   pl.BlockSpec((tb, LANES), lambda i: (i, 0)),
                   pl.BlockSpec((tb, LANES), lambda i: (i, 0)),
                   pl.BlockSpec((1, e), lambda i: (0, 0))],
        out_shape=[jax.ShapeDtypeStruct((t, d), F32), wide(jnp.int32), wide(F32), wide(jnp.int32),
                   jax.ShapeDtypeStruct((1, e), F32)],
        scratch_shapes=[pltpu.VMEM((1, e), F32)],
        compiler_params=_cp(("arbitrary",)),
        name="route",
    )(x1, g.reshape(1, d), sc, sh, rw_hi, rw_mid, rb.reshape(1, e))


def _scatter_kernel(dest_ref, zflag_ref, h_ref, xs_ref, zbuf, sem, zsem):
    tb = h_ref.shape[0]
    zrows = zbuf.shape[0]
    base = pl.program_id(0) * tb

    @pl.when(pl.program_id(0) == 0)
    def _():
        zbuf[...] = jnp.zeros_like(zbuf)

        def zcopy(b):
            return pltpu.make_async_copy(zbuf, xs_ref.at[pl.ds(pl.multiple_of(b * zrows, zrows), zrows), :], zsem)

        def zstart(b, carry):
            @pl.when(zflag_ref[b] != 0)
            def _():
                zcopy(b).start()
            return carry

        def zwait(b, carry):
            @pl.when(zflag_ref[b] != 0)
            def _():
                zcopy(b).wait()
            return carry

        n_blk = xs_ref.shape[0] // zrows
        lax.fori_loop(0, n_blk, zstart, 0)
        lax.fori_loop(0, n_blk, zwait, 0)

    def copy(t, j):
        d = dest_ref[(base + t) * TOP_K + j]
        return pltpu.make_async_copy(h_ref.at[pl.ds(t, 1), :], xs_ref.at[pl.ds(d, 1), :], sem)

    def start(t, carry):
        for j in range(TOP_K):
            copy(t, j).start()
        return carry

    lax.fori_loop(0, tb, start, 0, unroll=ROW_DMA_UNROLL)
    for j in range(TOP_K):
        pltpu.make_async_copy(h_ref, xs_ref.at[pl.ds(0, tb), :], sem).wait()


def _scatter_rows(dest_flat, zero_flag, h2, n_rows):
    t, d = h2.shape
    tb = min(SCATTER_TB, t)
    return pl.pallas_call(
        _scatter_kernel,
        grid_spec=pltpu.PrefetchScalarGridSpec(
            num_scalar_prefetch=2,
            grid=(t // tb,),
            in_specs=[pl.BlockSpec((tb, d), lambda i, dest, zf: (i, 0))],
            out_specs=pl.BlockSpec(memory_space=pl.ANY),
            scratch_shapes=[pltpu.VMEM((MOE_TM, d), F32), pltpu.SemaphoreType.DMA(()),
                            pltpu.SemaphoreType.DMA(())],
        ),
        out_shape=jax.ShapeDtypeStruct((n_rows, d), F32),
        compiler_params=_cp(("arbitrary",)),
        name="scatter_rows",
    )(dest_flat, zero_flag, h2)


def _expert_kernel(be_ref, bs_ref, nu_ref, x_ref, bgu_ref, bd_ref, wgu_hbm, wd_hbm, o_ref,
                   xb_ref, act_ref, wbuf, sem):
    del bs_ref
    i = pl.program_id(0)
    nt, _, tn = act_ref.shape
    n_slot, de, tnb = wbuf.shape
    ntb = o_ref.shape[1] // tnb
    n_steps = nt + ntb
    look = n_slot - 1
    live = i < nu_ref[0]

    def tile_copies(blk, step, slot):
        e = be_ref[blk]
        if step < nt:
            return [pltpu.make_async_copy(wgu_hbm.at[e, :, half * de + step * tn:half * de + (step + 1) * tn],
                                          wbuf.at[slot, :, half * tn:(half + 1) * tn], sem.at[slot])
                    for half in range(2)]
        colb = (step - nt) * tnb
        return [pltpu.make_async_copy(wd_hbm.at[e, :, colb:colb + tnb], wbuf.at[slot], sem.at[slot])]

    def request(blk, sub):
        blk = blk + sub // n_steps
        step = sub % n_steps

        @pl.when(blk < nu_ref[0])
        def _():
            for cp in tile_copies(blk, step, (blk * n_steps + step) % n_slot):
                cp.start()

    @pl.when(i == 0)
    def _():
        for sub in range(look):
            request(i, sub)

    @pl.when(live)
    def _():
        xb_ref[...] = x_ref[...].astype(BF16)
        for n in range(n_steps):
            request(i, n + look)
            slot = (i * n_steps + n) % n_slot
            for cp in tile_copies(i, n, slot):
                cp.wait()
            if n < nt:
                xb = xb_ref[...]
                w = wbuf[slot]
                glu = jnp.dot(xb, w[:, 0:tn].astype(BF16), preferred_element_type=F32) + bgu_ref[0, n]
                lin = jnp.dot(xb, w[:, tn:2 * tn].astype(BF16), preferred_element_type=F32) + bgu_ref[0, n + nt]
                glu = jnp.minimum(glu, SWIGLU_LIMIT)
                lin = jnp.clip(lin, -SWIGLU_LIMIT, SWIGLU_LIMIT)
                act_ref[n] = (glu * _sigmoid(SWIGLU_ALPHA * glu) * (lin + 1.0)).astype(BF16)
            else:
                m = n - nt
                wd = wbuf[slot]
                y = jnp.dot(act_ref[0], wd[0:tn].astype(BF16), preferred_element_type=F32)
                for k in range(1, nt):
                    y = y + jnp.dot(act_ref[k], wd[k * tn:(k + 1) * tn].astype(BF16), preferred_element_type=F32)
                o_ref[:, m * tnb:(m + 1) * tnb] = y + bd_ref[0, m]

    @pl.when(jnp.logical_not(live))
    def _():
        o_ref[...] = jnp.zeros_like(o_ref)


def _experts(blk_expert, blk_src, n_used, xs, w_gu, b_gu, w_down, b_down):
    n_rows, d = xs.shape
    tm, tn, tnb = MOE_TM, MOE_TN, MOE_TNB
    nb = n_rows // tm
    e, _, two_de = w_gu.shape
    de = two_de // 2
    nt = de // tn
    ntb = d // tnb
    assert 2 * tn == tnb and de == d
    b_gu4 = b_gu.reshape(e, 2 * nt, 1, tn)
    b_d4 = b_down.reshape(e, ntb, 1, tnb)
    return pl.pallas_call(
        _expert_kernel,
        grid_spec=pltpu.PrefetchScalarGridSpec(
            num_scalar_prefetch=3,
            grid=(nb,),
            in_specs=[pl.BlockSpec((tm, d), lambda i, be, bs, nu: (bs[i], 0)),
                      pl.BlockSpec((1, 2 * nt, 1, tn), lambda i, be, bs, nu: (be[i], 0, 0, 0)),
                      pl.BlockSpec((1, ntb, 1, tnb), lambda i, be, bs, nu: (be[i], 0, 0, 0)),
                      pl.BlockSpec(memory_space=pl.ANY),
                      pl.BlockSpec(memory_space=pl.ANY)],
            out_specs=pl.BlockSpec((tm, d), lambda i, be, bs, nu: (i, 0)),
            scratch_shapes=[pltpu.VMEM((tm, d), BF16), pltpu.VMEM((nt, tm, tn), BF16),
                            pltpu.VMEM((MOE_RING, de, tnb), F32), pltpu.SemaphoreType.DMA((MOE_RING,))],
        ),
        out_shape=jax.ShapeDtypeStruct((n_rows, d), F32),
        compiler_params=_cp(("arbitrary",)),
        name="experts",
    )(blk_expert, blk_src, n_used, xs, b_gu4, b_d4, w_gu, w_down)


def _combine_kernel(dest_ref, ys_ref, gate_ref, x_ref, g2_ref, fg_ref, o_ref, buf_ref, sem):
    tb = x_ref.shape[0]
    i = pl.program_id(0)
    half = i % 2

    def gather(blk, slot):
        def start(t, carry):
            for j in range(TOP_K):
                d = dest_ref[(blk * tb + t) * TOP_K + j]
                pltpu.make_async_copy(ys_ref.at[pl.ds(d, 1), :], buf_ref.at[slot, j, pl.ds(t, 1), :],
                                      sem.at[slot]).start()
            return carry
        lax.fori_loop(0, tb, start, 0, unroll=ROW_DMA_UNROLL)

    @pl.when(i == 0)
    def _():
        gather(i, half)

    @pl.when(i + 1 < pl.num_programs(0))
    def _():
        gather(i + 1, 1 - half)

    for j in range(TOP_K):
        pltpu.make_async_copy(ys_ref.at[pl.ds(0, tb), :], buf_ref.at[half, j], sem.at[half]).wait()

    gate = gate_ref[...]
    ffn = gate[:, 0:1] * buf_ref[half, 0]
    for j in range(1, TOP_K):
        ffn = ffn + gate[:, j:j + 1] * buf_ref[half, j]
    x2 = x_ref[...] + g2_ref[0] * ffn
    o_ref[...] = _rms(x2, NORM_EPS) * fg_ref[...]


def _combine(dest_flat, ys, gates, x1, g2, fg, seq):
    t, d = x1.shape
    tb = min(COMBINE_TB, seq)
    per_b = seq // tb
    return pl.pallas_call(
        _combine_kernel,
        grid_spec=pltpu.PrefetchScalarGridSpec(
            num_scalar_prefetch=1,
            grid=(t // tb,),
            in_specs=[pl.BlockSpec(memory_space=pl.ANY),
                      pl.BlockSpec((tb, LANES), lambda i, dest: (i, 0)),
                      pl.BlockSpec((tb, d), lambda i, dest: (i, 0)),
                      pl.BlockSpec((1, 1, d), lambda i, dest: (i // per_b, 0, 0)),
                      pl.BlockSpec((1, d), lambda i, dest: (0, 0))],
            out_specs=pl.BlockSpec((tb, d), lambda i, dest: (i, 0)),
            scratch_shapes=[pltpu.VMEM((2, TOP_K, tb, d), F32), pltpu.SemaphoreType.DMA((2,))],
        ),
        out_shape=jax.ShapeDtypeStruct((t, d), F32),
        compiler_params=_cp(("arbitrary",)),
        name="combine",
    )(dest_flat, ys, gates, x1, g2, fg.reshape(1, d))


def _moe(x1, g, sc2, sh2, g2, router_w, router_b, w_gu, b_gu, w_down, b_down, final_g, seq):
    t, d = x1.shape
    a = t * TOP_K
    tm = MOE_TM
    nb = -(-a // tm) + N_EXPERTS
    h2, idx_w, gate_w, rank_w, counts = _route(x1, g, sc2, sh2, router_w, router_b, seq)
    idx = idx_w[:, :TOP_K]
    rank = rank_w[:, :TOP_K]
    cnt = counts.reshape(-1).astype(jnp.int32)
    padded = ((cnt + tm - 1) // tm) * tm
    pad_end = jnp.cumsum(padded)
    pad_start = pad_end - padded
    experts = jnp.arange(N_EXPERTS, dtype=jnp.int32)
    start_of = jnp.sum(jnp.where(idx[:, :, None] == experts, pad_start, 0), axis=-1)
    dest = (start_of + rank).reshape(a).astype(jnp.int32)
    n_used = (pad_end[-1] // tm).astype(jnp.int32)
    blk = jnp.arange(nb, dtype=jnp.int32)
    blk_src = jnp.clip(blk, 0, jnp.maximum(n_used - 1, 0))
    blk_expert = jnp.sum((pad_end[None, :] <= (blk_src * tm)[:, None]).astype(jnp.int32), axis=1)
    blk_expert = jnp.minimum(blk_expert, N_EXPERTS - 1)
    last_blk = jnp.where(padded > 0, pad_end // tm - 1, -1)
    zero_flag = ((blk >= n_used) | jnp.any(last_blk[None, :] == blk[:, None], axis=1)).astype(jnp.int32)
    xs = _scatter_rows(dest, zero_flag, h2, nb * tm)
    ys = _experts(blk_expert, blk_src, n_used.reshape(1), xs, w_gu, b_gu, w_down, b_down)
    return _combine(dest, ys, gate_w, x1, g2, final_g, seq)


def kernel(x, c, ada_w, ada_b, norm1_g, norm2_g, w_in, rwkv_mu, rwkv_w0, rwkv_w2, rwkv_a0, rwkv_a2, rwkv_g2,
           rwkv_k_k, rwkv_k_a, rwkv_r_k, rwkv_gn_g, rwkv_gn_b, hgrn_lb_logits, hgrn_gn_g, proj_a, proj_b, w_out,
           router_w, router_b, exp_w_gate_up, exp_b_gate_up, exp_w_down, exp_b_down, final_norm_g):
    bsz, seq, d = x.shape
    depth = ada_w.shape[0]
    lower_bounds = jnp.cumsum(jax.nn.softmax(hgrn_lb_logits.astype(F32), axis=0), axis=0)
    xf = x.reshape(bsz * seq, d)
    rkv = 3 * RWKV_WIDTH
    rwkv_cols = rkv + LORA_COLS
    for l in range(depth):
        mod = _adaln(c, ada_w[l], ada_b[l])
        sh1, sc1, g1, sh2, sc2, g2 = [m.reshape(bsz, 1, d) for m in jnp.split(mod, 6, axis=-1)]

        wl = w_in[l]
        w_a = wl[:, :rkv + LORA_PAD].astype(BF16)
        w_b = wl[:, rwkv_cols:].astype(BF16)
        proj = _inproj(xf, norm1_g[l], sc1, sh1, w_a, w_b, seq)

        mu = rwkv_mu[l]
        mu_lo = jnp.pad(mu[rkv:], (0, LORA_PAD - LORA_COLS))
        pad_rows = lambda w, r0: jnp.zeros((LORA_PAD, RWKV_WIDTH), F32).at[r0:r0 + w.shape[0]].set(w).astype(BF16)
        ya = _rwkv(proj, bsz, seq, mu[:rkv], mu_lo, rwkv_w0[l], rwkv_a0[l], rwkv_k_k[l], rwkv_k_a[l],
                   rwkv_r_k[l].reshape(-1), rwkv_gn_g[l], rwkv_gn_b[l],
                   pad_rows(rwkv_w2[l], 0), pad_rows(rwkv_a2[l], DECAY_LORA),
                   pad_rows(rwkv_g2[l], DECAY_LORA + AAA_LORA))
        yb = _hgrn(proj, bsz, seq, lower_bounds[l], hgrn_gn_g[l])
        x1 = _mixout(ya, yb, proj, proj_a[l].astype(BF16), proj_b[l].astype(BF16), w_out[l].astype(BF16), xf, g1, seq)
        assert depth == 1
        out = _moe(x1, norm2_g[l], sc2, sh2, g2, router_w[l], router_b[l], exp_w_gate_up[l], exp_b_gate_up[l],
                   exp_w_down[l], exp_b_down[l], final_norm_g, seq)
    return out.reshape(bsz, seq, d)
```

```python
import functools

import jax
import jax.numpy as jnp
from jax import lax
from jax.experimental import pallas as pl
from jax.experimental.pallas import tpu as pltpu

F32 = jnp.float32
BF16 = jnp.bfloat16

D_MODEL = 2048
RWKV_WIDTH = 1024
RWKV_HEAD = 64
DECAY_LORA = 64
AAA_LORA = 64
GATE_LORA = 160
LORA_COLS = DECAY_LORA + AAA_LORA + GATE_LORA
LORA_PAD = 512
RWKV_GN_EPS = 64e-5
HGRN_WIDTH = 1024
HGRN_HEAD = 128
GLA_BLOCK = 16
N_EXPERTS = 32
TOP_K = 4
D_EXPERT = 2048
SWIGLU_ALPHA = 1.702
SWIGLU_LIMIT = 7.0
NORM_EPS = 1e-5

LANES = 128
VMEM_LIMIT = 56 * 1024 * 1024

RWKV_CHUNK = 64
RWKV_STEP_CHUNKS = 2
HGRN_STEP = 128
HGRN_STEP_CHUNKS = 4
PROJ_TM = 1024
PROJ_TN = 512
PROJ_RING = 4
NORM_ROWS = 128
MIX_TM = 256
MIX_GATE_BLOCKS = 4
MOE_TM = 512
MOE_TN = 512
MOE_TNB = 1024
MOE_RING = 3
ROUTE_TB = 256
SCATTER_TB = 1024
COMBINE_TB = 256
ROW_DMA_UNROLL = 8


def _cp(sem, vmem=VMEM_LIMIT):
    return pltpu.CompilerParams(dimension_semantics=sem, vmem_limit_bytes=vmem)


def _mm(a, b):
    return jnp.dot(a.astype(BF16), b.astype(BF16), preferred_element_type=F32)


def _mm_nt(a, b):
    return lax.dot_general(a.astype(BF16), b.astype(BF16), (((1,), (1,)), ((), ())),
                           preferred_element_type=F32)


def _mm_tn(a, b):
    return lax.dot_general(a.astype(BF16), b.astype(BF16), (((0,), (0,)), ((), ())),
                           preferred_element_type=F32)


def _split3(x):
    hi = x.astype(BF16)
    r1 = x - hi.astype(F32)
    mid = r1.astype(BF16)
    lo = (r1 - mid.astype(F32)).astype(BF16)
    return hi, mid, lo


def _mm_exact_rhs(m_bf16, x):
    hi, mid, lo = _split3(x)
    d = lambda p: jnp.dot(m_bf16, p, preferred_element_type=F32)
    return d(hi) + d(mid) + d(lo)


def _sigmoid(x):
    return 1.0 / (1.0 + jnp.exp(-x))


def _softplus(x):
    return jnp.maximum(x, 0.0) + jnp.log(1.0 + jnp.exp(-jnp.abs(x)))


def _rms(x, eps):
    return x * lax.rsqrt(jnp.mean(x * x, axis=-1, keepdims=True) + eps)


def _adaln_kernel(c_ref, w_ref, b_ref, o_ref):
    c = c_ref[...]
    c_act = c * _sigmoid(c)
    o_ref[...] = jnp.dot(c_act, w_ref[...], preferred_element_type=F32,
                         precision=lax.Precision.HIGHEST) + b_ref[...]


def _adaln(c, w, b):
    bsz, d = c.shape
    n = w.shape[1]
    tn = 1024
    return pl.pallas_call(
        _adaln_kernel,
        grid=(n // tn,),
        in_specs=[pl.BlockSpec((bsz, d), lambda j: (0, 0)),
                  pl.BlockSpec((d, tn), lambda j: (0, j)),
                  pl.BlockSpec((1, tn), lambda j: (0, j))],
        out_specs=pl.BlockSpec((bsz, tn), lambda j: (0, j)),
        out_shape=jax.ShapeDtypeStruct((bsz, n), F32),
        compiler_params=_cp(("arbitrary",)),
        name="adaln",
    )(c, w, b.reshape(1, n))


def _inproj_kernel(n_a, x_ref, g_ref, sc_ref, sh_ref, wa_hbm, wb_hbm, o_hbm, h_ref, wbuf, obuf, wsem, osem):
    i = pl.program_id(0)
    tm = x_ref.shape[0]
    n_slot, _, tn = wbuf.shape
    look = n_slot - 1
    n_tiles = o_hbm.shape[1] // tn

    def w_copy(tile, slot):
        src = (wa_hbm.at[:, tile * tn:(tile + 1) * tn] if tile < n_a
               else wb_hbm.at[:, (tile - n_a) * tn:(tile - n_a + 1) * tn])
        return pltpu.make_async_copy(src, wbuf.at[slot], wsem.at[slot])

    def o_copy(tile):
        rows = pl.ds(pl.multiple_of(i * tm, tm), tm)
        return pltpu.make_async_copy(obuf.at[tile % 2], o_hbm.at[rows, tile * tn:(tile + 1) * tn], osem.at[tile % 2])

    def request(sub):
        blk = i + sub // n_tiles
        tile = sub % n_tiles

        @pl.when(blk < pl.num_programs(0))
        def _():
            w_copy(tile, (blk * n_tiles + tile) % n_slot).start()

    @pl.when(i == 0)
    def _():
        for sub in range(look):
            request(sub)

    def rows(c, carry):
        rs = pl.ds(pl.multiple_of(c * NORM_ROWS, NORM_ROWS), NORM_ROWS)
        y = _rms(x_ref[rs, :], NORM_EPS) * g_ref[...]
        h_ref[rs, :] = (y * (1.0 + sc_ref[0]) + sh_ref[0]).astype(BF16)
        return carry
    lax.fori_loop(0, tm // NORM_ROWS, rows, 0)

    for tile in range(n_tiles):
        request(tile + look)
        slot = (i * n_tiles + tile) % n_slot
        w_copy(tile, slot).wait()
        if tile >= 2:
            o_copy(tile - 2).wait()
        obuf[tile % 2] = jnp.dot(h_ref[...], wbuf[slot], preferred_element_type=F32).astype(obuf.dtype)
        o_copy(tile).start()
    for tile in range(max(n_tiles - 2, 0), n_tiles):
        o_copy(tile).wait()


def _inproj(xf, g, sc, sh, w_a, w_b, seq):
    t, d = xf.shape
    tm = min(PROJ_TM, seq)
    tn = PROJ_TN
    n_a, n_b = w_a.shape[1] // tn, w_b.shape[1] // tn
    per_b = seq // tm
    return pl.pallas_call(
        functools.partial(_inproj_kernel, n_a),
        grid=(t // tm,),
        in_specs=[pl.BlockSpec((tm, d), lambda i: (i, 0)),
                  pl.BlockSpec((1, d), lambda i: (0, 0)),
                  pl.BlockSpec((1, 1, d), lambda i: (i // per_b, 0, 0)),
                  pl.BlockSpec((1, 1, d), lambda i: (i // per_b, 0, 0)),
                  pl.BlockSpec(memory_space=pl.ANY),
                  pl.BlockSpec(memory_space=pl.ANY)],
        out_specs=pl.BlockSpec(memory_space=pl.ANY),
        out_shape=jax.ShapeDtypeStruct((t, (n_a + n_b) * tn), BF16),
        scratch_shapes=[pltpu.VMEM((tm, d), BF16), pltpu.VMEM((PROJ_RING, d, tn), BF16),
                        pltpu.VMEM((2, tm, tn), BF16), pltpu.SemaphoreType.DMA((PROJ_RING,)),
                        pltpu.SemaphoreType.DMA((2,))],
        compiler_params=_cp(("arbitrary",)),
        name="inproj",
    )(xf, g.reshape(1, d), sc, sh, w_a, w_b)


def _rwkv_chunk(rows_sl, r_ref, k_ref, v_ref, lo_ref, mu_ref, mulo_ref, w0_ref, a0_ref, kk_ref, ka_ref, rk_ref,
                gng_ref, gnb_ref, w2_ref, a2_ref, g2_ref, o_ref,
                st_ref, cr_ref, ck_ref, cv_ref, clo_ref):
    L = RWKV_CHUNK
    H2 = 2 * L
    n_pair = RWKV_WIDTH // LANES

    def shift(x_ref, mu, carry_ref):
        x = x_ref[rows_sl, :].astype(F32)
        rows = lax.broadcasted_iota(jnp.int32, x.shape, 0)
        prev = jnp.where(rows == 0, carry_ref[...], pltpu.roll(x, 1, axis=0))
        carry_ref[...] = x[L - 1:L, :]
        return x + (prev - x) * mu

    mu = mu_ref[...]
    r = shift(r_ref, mu[:, 0:RWKV_WIDTH], cr_ref)
    k = shift(k_ref, mu[:, RWKV_WIDTH:2 * RWKV_WIDTH], ck_ref)
    v = shift(v_ref, mu[:, 2 * RWKV_WIDTH:3 * RWKV_WIDTH], cv_ref)
    lo = shift(lo_ref, mulo_ref[...], clo_ref)

    wpre = w0_ref[...] + _mm(jnp.tanh(lo), w2_ref[...])
    lw = -jnp.exp(-_softplus(-wpre) - 0.5)
    a = _sigmoid(a0_ref[...] + _mm(lo, a2_ref[...]))
    g = _mm(_sigmoid(lo), g2_ref[...])

    ti = lax.broadcasted_iota(jnp.int32, (L, L), 0)
    tj = lax.broadcasted_iota(jnp.int32, (L, L), 1)
    tri = jnp.where(tj <= ti, 1.0, 0.0).astype(BF16)
    cl = _mm_exact_rhs(tri, lw)
    cl_last = cl[L - 1:L, :]
    e_pos = jnp.exp(cl)
    e_neg = jnp.exp(-cl)
    e_prev = jnp.exp(cl - lw)
    e_end = jnp.exp(cl_last - cl)
    p_last = jnp.exp(cl_last)

    kk = k * kk_ref[...]
    k2 = k * (1.0 + (a - 1.0) * ka_ref[...])
    f_b = a * e_neg
    f_bend = a * e_end
    rk = rk_ref[...]
    gng = gng_ref[...]
    gnb = gnb_ref[...]

    lane = lax.broadcasted_iota(jnp.int32, (L, LANES), 1)
    lo_half = lane < RWKV_HEAD
    si = lax.broadcasted_iota(jnp.int32, (H2, H2), 0)
    sj = lax.broadcasted_iota(jnp.int32, (H2, H2), 1)
    same = (si // L) == (sj // L)
    m_strict = same & (sj < si)
    m_incl = same & (sj <= si)
    eye = si == sj
    own = (si // L) == (sj // RWKV_HEAD)

    def stack(x):
        return jnp.concatenate([jnp.where(lo_half, x, 0.0), jnp.where(lo_half, 0.0, x)], axis=0)

    def rep(x):
        return jnp.concatenate([x, x], axis=0)

    pairs = range(n_pair)
    sls = [slice(p * LANES, (p + 1) * LANES) for p in pairs]
    at, bt, bend, kt, kend, rt, v_s, bonus = [], [], [], [], [], [], [], []
    for sl in sls:
        kk_s = stack(kk[:, sl])
        nrm = jnp.sqrt(jnp.sum(kk_s * kk_s, axis=-1, keepdims=True))
        kkn = kk_s / jnp.maximum(nrm, 1e-12)
        at.append(-kkn * rep(e_prev[:, sl]))
        bt.append(kkn * rep(f_b[:, sl]))
        bend.append(kkn * rep(f_bend[:, sl]))
        k2_s = stack(k2[:, sl])
        kt.append(k2_s * rep(e_neg[:, sl]))
        kend.append(k2_s * rep(e_end[:, sl]))
        r_s = stack(r[:, sl])
        rt.append(r_s * rep(e_pos[:, sl]))
        vs = stack(v[:, sl])
        v_s.append(vs)
        bonus.append(jnp.sum(r_s * k2_s * rk[:, sl], axis=-1, keepdims=True) * vs)

    gm = [_mm_nt(jnp.concatenate([at[p], rt[p]], axis=0), jnp.concatenate([bt[p], kt[p]], axis=0))
          for p in pairs]
    a_ab = [jnp.where(m_strict, gm[p][:H2, :H2], 0.0) for p in pairs]
    a_ak = [jnp.where(m_strict, gm[p][:H2, H2:], 0.0) for p in pairs]
    a_rb = [jnp.where(m_incl, gm[p][H2:, :H2], 0.0) for p in pairs]
    a_rk = [jnp.where(m_incl, gm[p][H2:, H2:], 0.0) for p in pairs]

    ident = jnp.where(eye, 1.0, 0.0)
    tinv = [ident + a_ab[p] for p in pairs]
    pw = a_ab
    akv = [_mm(a_ak[p], v_s[p]) for p in pairs]
    ark_v = [_mm(a_rk[p], v_s[p]) for p in pairs]
    kend_v = [_mm_tn(kend[p], v_s[p]) for p in pairs]
    for _ in range(max(1, (L - 1).bit_length() - 1)):
        pw = [_mm(pw[p], pw[p]) for p in pairs]
        tinv = [tinv[p] + _mm(tinv[p], pw[p]) for p in pairs]

    tx = [_mm(tinv[p], jnp.concatenate([at[p], akv[p]], axis=1)) for p in pairs]
    y2 = [_mm(a_rb[p], tx[p]) for p in pairs]
    btx = [_mm_tn(bend[p], tx[p]) for p in pairs]
    s0 = [st_ref[p] for p in pairs]
    y = [_mm(rt[p] + y2[p][:, :LANES], s0[p]) + (y2[p][:, LANES:] + ark_v[p]) for p in pairs]
    for p in pairs:
        m_mat = jnp.where(eye, jnp.broadcast_to(p_last[:, sls[p]], (H2, LANES)), 0.0) + btx[p][:, :LANES]
        st_ref[p] = _mm(m_mat, s0[p]) + (btx[p][:, LANES:] + kend_v[p])

    for p in pairs:
        sl = sls[p]
        mean = jnp.sum(y[p], axis=-1, keepdims=True) * (1.0 / RWKV_HEAD)
        dlt = jnp.where(own, y[p] - mean, 0.0)
        var = jnp.sum(dlt * dlt, axis=-1, keepdims=True) * (1.0 / RWKV_HEAD)
        yn = dlt * lax.rsqrt(var + RWKV_GN_EPS) * gng[:, sl] + jnp.where(own, gnb[:, sl], 0.0)
        ys = yn + bonus[p]
        o_ref[rows_sl, sl] = ((ys[:L] + ys[L:]) * g[:, sl]).astype(o_ref.dtype)


def _rwkv_kernel(*refs):
    st_ref, cr_ref, ck_ref, cv_ref, clo_ref = refs[-5:]

    @pl.when(pl.program_id(1) == 0)
    def _():
        for ref in (st_ref, cr_ref, ck_ref, cv_ref, clo_ref):
            ref[...] = jnp.zeros_like(ref)

    for c in range(RWKV_STEP_CHUNKS):
        _rwkv_chunk(slice(c * RWKV_CHUNK, (c + 1) * RWKV_CHUNK), *refs)


def _rwkv(proj, bsz, seq, mu_rkv, mu_lo, w0, a0, k_k, k_a, r_k, gn_g, gn_b, w2p, a2p, g2p):
    L = RWKV_CHUNK * RWKV_STEP_CHUNKS
    nc = seq // L
    W = RWKV_WIDTH
    row = lambda n: pl.BlockSpec((1, n), lambda b, c: (0, 0))
    full = lambda a: pl.BlockSpec(a.shape, lambda b, c: (0, 0))
    lo_blk = 3 * W // LORA_PAD
    v1 = lambda a: a.reshape(1, -1)
    return pl.pallas_call(
        _rwkv_kernel,
        grid=(bsz, nc),
        in_specs=[pl.BlockSpec((L, W), lambda b, c: (b * nc + c, 0)),
                  pl.BlockSpec((L, W), lambda b, c: (b * nc + c, 1)),
                  pl.BlockSpec((L, W), lambda b, c: (b * nc + c, 2)),
                  pl.BlockSpec((L, LORA_PAD), lambda b, c: (b * nc + c, lo_blk)),
                  row(3 * W), row(LORA_PAD), row(W), row(W), row(W), row(W), row(W), row(W), row(W),
                  full(w2p), full(a2p), full(g2p)],
        out_specs=pl.BlockSpec((L, W), lambda b, c: (b * nc + c, 0)),
        out_shape=jax.ShapeDtypeStruct((bsz * seq, W), BF16),
        scratch_shapes=[pltpu.VMEM((W // LANES, LANES, LANES), F32),
                        pltpu.VMEM((1, W), F32), pltpu.VMEM((1, W), F32), pltpu.VMEM((1, W), F32),
                        pltpu.VMEM((1, LORA_PAD), F32)],
        compiler_params=_cp(("arbitrary", "arbitrary")),
        name="rwkv7",
    )(proj, proj, proj, proj, v1(mu_rkv), v1(mu_lo), v1(w0), v1(a0), v1(k_k), v1(k_a), v1(r_k), v1(gn_g),
      v1(gn_b), w2p, a2p, g2p)


def _hgrn_chunk(rows_sl, qa_ref, qb_ref, fa_ref, fb_ref, ia_ref, ib_ref, oa_ref, ob_ref, lb_ref, gn_ref, o_ref,
                st_ref):
    n = HGRN_STEP
    wide = lambda a_ref, b_ref: jnp.concatenate([a_ref[rows_sl, :], b_ref[rows_sl, :]], axis=1).astype(F32)
    blk = GLA_BLOCK
    nb = n // blk
    n_head = HGRN_WIDTH // HGRN_HEAD

    lb = lb_ref[...]
    qf = wide(qa_ref, qb_ref)
    q = qf * _sigmoid(qf)
    forget = lb + (1.0 - lb) * _sigmoid(wide(fa_ref, fb_ref))
    log_f = jnp.log(forget)
    kx = 1.0 - forget
    vv = wide(ia_ref, ib_ref)
    og = wide(oa_ref, ob_ref)

    ti = lax.broadcasted_iota(jnp.int32, (n, n), 0)
    tj = lax.broadcasted_iota(jnp.int32, (n, n), 1)
    same = (ti // blk) == (tj // blk)
    causal = same & (tj <= ti)
    b = _mm_exact_rhs(jnp.where(causal, 1.0, 0.0).astype(BF16), log_f)
    b_last = _mm_exact_rhs(jnp.where(same, 1.0, 0.0).astype(BF16), log_f)
    q_dec = q * jnp.exp(b)
    k_dec = kx * jnp.exp(-b)
    k_end = kx * jnp.exp(b_last - b)
    d_end = jnp.exp(b_last)
    rows = lax.broadcasted_iota(jnp.int32, (n, HGRN_HEAD), 0)
    gn = gn_ref[...]

    heads = range(n_head)
    sls = [slice(h * HGRN_HEAD, (h + 1) * HGRN_HEAD) for h in heads]
    attn = [jnp.where(causal, _mm_nt(q_dec[:, sl], k_dec[:, sl]), 0.0) for sl in sls]
    o_intra = [_mm(attn[h], vv[:, sls[h]]) for h in heads]
    vt = [vv[:, sl].T for sl in sls]
    blk_mask = [(rows // blk) == c for c in range(nb)]
    kv_all = [_mm(vt[h], jnp.concatenate([jnp.where(blk_mask[c], k_end[:, sls[h]], 0.0) for c in range(nb)],
                                         axis=1)) for h in heads]
    kv_t = [[kv_all[h][:, c * HGRN_HEAD:(c + 1) * HGRN_HEAD] for c in range(nb)] for h in heads]
    st = [[st_ref[h]] for h in heads]
    for h in heads:
        for c in range(nb):
            st[h].append(st[h][c] * d_end[c * blk:c * blk + 1, sls[h]] + kv_t[h][c])
        st_ref[h] = st[h][nb]
    for h in heads:
        sl = sls[h]
        inter = [_mm_nt(q_dec[c * blk:(c + 1) * blk, sl], st[h][c]) for c in range(nb)]
        o = o_intra[h] + jnp.concatenate(inter, axis=0)
        o = _rms(o, NORM_EPS) * gn
        ogh = og[:, sl]
        o_ref[rows_sl, sl] = (o * (ogh * _sigmoid(ogh))).astype(o_ref.dtype)


def _hgrn_kernel(*refs):
    st_ref = refs[-1]

    @pl.when(pl.program_id(1) == 0)
    def _():
        st_ref[...] = jnp.zeros_like(st_ref)

    for c in range(HGRN_STEP_CHUNKS):
        _hgrn_chunk(slice(c * HGRN_STEP, (c + 1) * HGRN_STEP), *refs)


def _hgrn(proj, bsz, seq, lb, gn_g):
    n = HGRN_STEP * HGRN_STEP_CHUNKS
    ns = seq // n
    W = HGRN_WIDTH
    half = W // 2
    first = (3 * RWKV_WIDTH + LORA_PAD) // half
    col = lambda j: pl.BlockSpec((n, half), lambda b, c: (b * ns + c, first + j))
    return pl.pallas_call(
        _hgrn_kernel,
        grid=(bsz, ns),
        in_specs=[col(j) for j in range(8)] + [
                  pl.BlockSpec((1, W), lambda b, c: (0, 0)),
                  pl.BlockSpec((1, HGRN_HEAD), lambda b, c: (0, 0))],
        out_specs=pl.BlockSpec((n, W), lambda b, c: (b * ns + c, 0)),
        out_shape=jax.ShapeDtypeStruct((bsz * seq, W), BF16),
        scratch_shapes=[pltpu.VMEM((W // HGRN_HEAD, HGRN_HEAD, HGRN_HEAD), F32)],
        compiler_params=_cp(("arbitrary", "arbitrary")),
        name="hgrn2",
    )(*([proj] * 8), lb.reshape(1, W), gn_g.reshape(1, HGRN_HEAD))


def _mixout_kernel(ya_ref, yb_ref, *refs):
    ga_refs, gb_refs = refs[:MIX_GATE_BLOCKS], refs[MIX_GATE_BLOCKS:2 * MIX_GATE_BLOCKS]
    pa_ref, pb_ref, wo_ref, x_ref, g_ref, o_ref = refs[2 * MIX_GATE_BLOCKS:]
    wc = 2 * ga_refs[0].shape[1]
    ya = ya_ref[...]
    yb = yb_ref[...]
    pair = lambda rs, c: jnp.concatenate([rs[2 * c][...], rs[2 * c + 1][...]], axis=1).astype(F32)
    mix = None
    for c in range(MIX_GATE_BLOCKS // 2):
        cs = slice(c * wc, (c + 1) * wc)
        ma = jnp.dot(ya, pa_ref[:, cs], preferred_element_type=F32)
        mb = jnp.dot(yb, pb_ref[:, cs], preferred_element_type=F32)
        part = (_sigmoid(pair(ga_refs, c)) * ma + _sigmoid(pair(gb_refs, c)) * mb).astype(BF16)
        term = jnp.dot(part, wo_ref[cs, :], preferred_element_type=F32)
        mix = term if mix is None else mix + term
    o_ref[...] = x_ref[...] + g_ref[0] * mix


def _mixout(ya, yb, proj, pa, pb, wo, xf, g1, seq):
    t, w = ya.shape
    d = pa.shape[1]
    tm = min(MIX_TM, seq)
    per_b = seq // tm
    wc = d // MIX_GATE_BLOCKS
    ga0 = (3 * RWKV_WIDTH + LORA_PAD + 4 * HGRN_WIDTH) // wc
    gate = lambda j: pl.BlockSpec((tm, wc), lambda i: (i, ga0 + j))
    const = lambda a: pl.BlockSpec(a.shape, lambda i: (0, 0), pipeline_mode=pl.Buffered(1))
    return pl.pallas_call(
        _mixout_kernel,
        grid=(t // tm,),
        in_specs=[pl.BlockSpec((tm, w), lambda i: (i, 0)),
                  pl.BlockSpec((tm, w), lambda i: (i, 0))] + [gate(j) for j in range(2 * MIX_GATE_BLOCKS)] + [
                  const(pa), const(pb), const(wo),
                  pl.BlockSpec((tm, d), lambda i: (i, 0)),
                  pl.BlockSpec((1, 1, d), lambda i: (i // per_b, 0, 0))],
        out_specs=pl.BlockSpec((tm, d), lambda i: (i, 0)),
        out_shape=jax.ShapeDtypeStruct((t, d), F32),
        compiler_params=_cp(("arbitrary",)),
        name="mixout",
    )(ya, yb, *([proj] * (2 * MIX_GATE_BLOCKS)), pa, pb, wo, xf, g1)


def _route_kernel(x_ref, g_ref, sc_ref, sh_ref, rwh_ref, rwm_ref, rb_ref, h_ref, idx_ref, gate_ref, rank_ref,
                  cnt_ref, carry_ref):
    tb = x_ref.shape[0]

    @pl.when(pl.program_id(0) == 0)
    def _():
        carry_ref[...] = jnp.zeros_like(carry_ref)

    h = (_rms(x_ref[...], NORM_EPS) * g_ref[...]) * (1.0 + sc_ref[0]) + sh_ref[0]
    h_ref[...] = h
    h_hi, h_mid, _ = _split3(h)
    d = lambda a, b: jnp.dot(a, b, preferred_element_type=F32)
    logits = d(h_hi, rwh_ref[...]) + (d(h_hi, rwm_ref[...]) + d(h_mid, rwh_ref[...])) + rb_ref[...]

    lane = lax.broadcasted_iota(jnp.int32, logits.shape, 1).astype(F32)
    work = logits
    vals, idxs, sels = [], [], []
    for _ in range(TOP_K):
        m = jnp.max(work, axis=-1, keepdims=True)
        idx = jnp.min(jnp.where(work == m, lane, float(N_EXPERTS)), axis=-1, keepdims=True)
        sel = lane == idx
        vals.append(m)
        idxs.append(idx)
        sels.append(sel)
        work = jnp.where(sel, -jnp.inf, work)
    exps = [jnp.exp(v - vals[0]) for v in vals]
    den = exps[0] + exps[1] + exps[2] + exps[3]

    onehot = jnp.zeros(logits.shape, F32)
    for sel in sels:
        onehot = onehot + jnp.where(sel, 1.0, 0.0)
    ti = lax.broadcasted_iota(jnp.int32, (tb, tb), 0)
    tj = lax.broadcasted_iota(jnp.int32, (tb, tb), 1)
    below = jnp.where(tj < ti, 1.0, 0.0).astype(BF16)
    before = jnp.dot(below, onehot.astype(BF16), preferred_element_type=F32) + carry_ref[...]
    carry_ref[...] = carry_ref[...] + jnp.sum(onehot, axis=0, keepdims=True)
    cnt_ref[...] = carry_ref[...]

    out_lane = lax.broadcasted_iota(jnp.int32, (tb, LANES), 1)
    idx_out = jnp.zeros((tb, LANES), jnp.int32)
    rank_out = jnp.zeros((tb, LANES), jnp.int32)
    gate_out = jnp.zeros((tb, LANES), F32)
    for j in range(TOP_K):
        rank = jnp.sum(jnp.where(sels[j], before, 0.0), axis=-1, keepdims=True)
        idx_out = jnp.where(out_lane == j, idxs[j].astype(jnp.int32), idx_out)
        rank_out = jnp.where(out_lane == j, rank.astype(jnp.int32), rank_out)
        gate_out = jnp.where(out_lane == j, exps[j] / den, gate_out)
    idx_ref[...] = idx_out
    rank_ref[...] = rank_out
    gate_ref[...] = gate_out


def _route(x1, g, sc, sh, rw, rb, seq):
    t, d = x1.shape
    tb = min(ROUTE_TB, seq)
    per_b = seq // tb
    e = rw.shape[1]
    rw_hi = rw.astype(BF16)
    rw_mid = (rw - rw_hi.astype(F32)).astype(BF16)
    wide = lambda dt: jax.ShapeDtypeStruct((t, LANES), dt)
    return pl.pallas_call(
        _route_kernel,
        grid=(t // tb,),
        in_specs=[pl.BlockSpec((tb, d), lambda i: (i, 0)),
                  pl.BlockSpec((1, d), lambda i: (0, 0)),
                  pl.BlockSpec((1, 1, d), lambda i: (i // per_b, 0, 0)),
                  pl.BlockSpec((1, 1, d), lambda i: (i // per_b, 0, 0)),
                  pl.BlockSpec((d, e), lambda i: (0, 0)),
                  pl.BlockSpec((d, e), lambda i: (0, 0)),
                  pl.BlockSpec((1, e), lambda i: (0, 0))],
        out_specs=[pl.BlockSpec((tb, d), lambda i: (i, 0)),
                   pl.BlockSpec((tb, LANES), lambda i: (i, 0)),
                   pl.BlockSpec((tb, LANES), lambda i: (i, 0)),
                   pl.BlockSpec((tb, LANES), lambda i: (i, 0)),
                   pl.BlockSpec((1, e), lambda i: (0, 0))],
        out_shape=[jax.ShapeDtypeStruct((t, d), F32), wide(jnp.int32), wide(F32), wide(jnp.int32),
                   jax.ShapeDtypeStruct((1, e), F32)],
        scratch_shapes=[pltpu.VMEM((1, e), F32)],
        compiler_params=_cp(("arbitrary",)),
        name="route",
    )(x1, g.reshape(1, d), sc, sh, rw_hi, rw_mid, rb.reshape(1, e))


def _scatter_kernel(dest_ref, zflag_ref, h_ref, xs_ref, zbuf, sem, zsem):
    tb = h_ref.shape[0]
    zrows = zbuf.shape[0]
    base = pl.program_id(0) * tb

    @pl.when(pl.program_id(0) == 0)
    def _():
        zbuf[...] = jnp.zeros_like(zbuf)

        def zcopy(b):
            return pltpu.make_async_copy(zbuf, xs_ref.at[pl.ds(pl.multiple_of(b * zrows, zrows), zrows), :], zsem)

        def zstart(b, carry):
            @pl.when(zflag_ref[b] != 0)
            def _():
                zcopy(b).start()
            return carry

        def zwait(b, carry):
            @pl.when(zflag_ref[b] != 0)
            def _():
                zcopy(b).wait()
            return carry

        n_blk = xs_ref.shape[0] // zrows
        lax.fori_loop(0, n_blk, zstart, 0)
        lax.fori_loop(0, n_blk, zwait, 0)

    def copy(t, j):
        d = dest_ref[(base + t) * TOP_K + j]
        return pltpu.make_async_copy(h_ref.at[pl.ds(t, 1), :], xs_ref.at[pl.ds(d, 1), :], sem)

    def start(t, carry):
        for j in range(TOP_K):
            copy(t, j).start()
        return carry

    lax.fori_loop(0, tb, start, 0, unroll=ROW_DMA_UNROLL)
    for j in range(TOP_K):
        pltpu.make_async_copy(h_ref, xs_ref.at[pl.ds(0, tb), :], sem).wait()


def _scatter_rows(dest_flat, zero_flag, h2, n_rows):
    t, d = h2.shape
    tb = min(SCATTER_TB, t)
    return pl.pallas_call(
        _scatter_kernel,
        grid_spec=pltpu.PrefetchScalarGridSpec(
            num_scalar_prefetch=2,
            grid=(t // tb,),
            in_specs=[pl.BlockSpec((tb, d), lambda i, dest, zf: (i, 0))],
            out_specs=pl.BlockSpec(memory_space=pl.ANY),
            scratch_shapes=[pltpu.VMEM((MOE_TM, d), F32), pltpu.SemaphoreType.DMA(()),
                            pltpu.SemaphoreType.DMA(())],
        ),
        out_shape=jax.ShapeDtypeStruct((n_rows, d), F32),
        compiler_params=_cp(("arbitrary",)),
        name="scatter_rows",
    )(dest_flat, zero_flag, h2)


def _expert_kernel(be_ref, bs_ref, nu_ref, x_ref, bgu_ref, bd_ref, wgu_hbm, wd_hbm, o_ref,
                   xb_ref, act_ref, wbuf, sem):
    del bs_ref
    i = pl.program_id(0)
    nt, _, tn = act_ref.shape
    n_slot, de, tnb = wbuf.shape
    ntb = o_ref.shape[1] // tnb
    n_steps = nt + ntb
    look = n_slot - 1
    live = i < nu_ref[0]

    def tile_copies(blk, step, slot):
        e = be_ref[blk]
        if step < nt:
            return [pltpu.make_async_copy(wgu_hbm.at[e, :, half * de + step * tn:half * de + (step + 1) * tn],
                                          wbuf.at[slot, :, half * tn:(half + 1) * tn], sem.at[slot])
                    for half in range(2)]
        colb = (step - nt) * tnb
        return [pltpu.make_async_copy(wd_hbm.at[e, :, colb:colb + tnb], wbuf.at[slot], sem.at[slot])]

    def request(blk, sub):
        blk = blk + sub // n_steps
        step = sub % n_steps

        @pl.when(blk < nu_ref[0])
        def _():
            for cp in tile_copies(blk, step, (blk * n_steps + step) % n_slot):
                cp.start()

    @pl.when(i == 0)
    def _():
        for sub in range(look):
            request(i, sub)

    @pl.when(live)
    def _():
        xb_ref[...] = x_ref[...].astype(BF16)
        for n in range(n_steps):
            request(i, n + look)
            slot = (i * n_steps + n) % n_slot
            for cp in tile_copies(i, n, slot):
                cp.wait()
            if n < nt:
                xb = xb_ref[...]
                w = wbuf[slot]
                glu = jnp.dot(xb, w[:, 0:tn].astype(BF16), preferred_element_type=F32) + bgu_ref[0, n]
                lin = jnp.dot(xb, w[:, tn:2 * tn].astype(BF16), preferred_element_type=F32) + bgu_ref[0, n + nt]
                glu = jnp.minimum(glu, SWIGLU_LIMIT)
                lin = jnp.clip(lin, -SWIGLU_LIMIT, SWIGLU_LIMIT)
                act_ref[n] = (glu * _sigmoid(SWIGLU_ALPHA * glu) * (lin + 1.0)).astype(BF16)
            else:
                m = n - nt
                wd = wbuf[slot]
                y = jnp.dot(act_ref[0], wd[0:tn].astype(BF16), preferred_element_type=F32)
                for k in range(1, nt):
                    y = y + jnp.dot(act_ref[k], wd[k * tn:(k + 1) * tn].astype(BF16), preferred_element_type=F32)
                o_ref[:, m * tnb:(m + 1) * tnb] = y + bd_ref[0, m]

    @pl.when(jnp.logical_not(live))
    def _():
        o_ref[...] = jnp.zeros_like(o_ref)


def _experts(blk_expert, blk_src, n_used, xs, w_gu, b_gu, w_down, b_down):
    n_rows, d = xs.shape
    tm, tn, tnb = MOE_TM, MOE_TN, MOE_TNB
    nb = n_rows // tm
    e, _, two_de = w_gu.shape
    de = two_de // 2
    nt = de // tn
    ntb = d // tnb
    assert 2 * tn == tnb and de == d
    b_gu4 = b_gu.reshape(e, 2 * nt, 1, tn)
    b_d4 = b_down.reshape(e, ntb, 1, tnb)
    return pl.pallas_call(
        _expert_kernel,
        grid_spec=pltpu.PrefetchScalarGridSpec(
            num_scalar_prefetch=3,
            grid=(nb,),
            in_specs=[pl.BlockSpec((tm, d), lambda i, be, bs, nu: (bs[i], 0)),
                      pl.BlockSpec((1, 2 * nt, 1, tn), lambda i, be, bs, nu: (be[i], 0, 0, 0)),
                      pl.BlockSpec((1, ntb, 1, tnb), lambda i, be, bs, nu: (be[i], 0, 0, 0)),
                      pl.BlockSpec(memory_space=pl.ANY),
                      pl.BlockSpec(memory_space=pl.ANY)],
            out_specs=pl.BlockSpec((tm, d), lambda i, be, bs, nu: (i, 0)),
            scratch_shapes=[pltpu.VMEM((tm, d), BF16), pltpu.VMEM((nt, tm, tn), BF16),
                            pltpu.VMEM((MOE_RING, de, tnb), F32), pltpu.SemaphoreType.DMA((MOE_RING,))],
        ),
        out_shape=jax.ShapeDtypeStruct((n_rows, d), F32),
        compiler_params=_cp(("arbitrary",)),
        name="experts",
    )(blk_expert, blk_src, n_used, xs, b_gu4, b_d4, w_gu, w_down)


def _combine_kernel(dest_ref, ys_ref, gate_ref, x_ref, g2_ref, fg_ref, o_ref, buf_ref, sem):
    tb = x_ref.shape[0]
    i = pl.program_id(0)
    half = i % 2

    def gather(blk, slot):
        def start(t, carry):
            for j in range(TOP_K):
                d = dest_ref[(blk * tb + t) * TOP_K + j]
                pltpu.make_async_copy(ys_ref.at[pl.ds(d, 1), :], buf_ref.at[slot, j, pl.ds(t, 1), :],
                                      sem.at[slot]).start()
            return carry
        lax.fori_loop(0, tb, start, 0, unroll=ROW_DMA_UNROLL)

    @pl.when(i == 0)
    def _():
        gather(i, half)

    @pl.when(i + 1 < pl.num_programs(0))
    def _():
        gather(i + 1, 1 - half)

    for j in range(TOP_K):
        pltpu.make_async_copy(ys_ref.at[pl.ds(0, tb), :], buf_ref.at[half, j], sem.at[half]).wait()

    gate = gate_ref[...]
    ffn = gate[:, 0:1] * buf_ref[half, 0]
    for j in range(1, TOP_K):
        ffn = ffn + gate[:, j:j + 1] * buf_ref[half, j]
    x2 = x_ref[...] + g2_ref[0] * ffn
    o_ref[...] = _rms(x2, NORM_EPS) * fg_ref[...]


def _combine(dest_flat, ys, gates, x1, g2, fg, seq):
    t, d = x1.shape
    tb = min(COMBINE_TB, seq)
    per_b = seq // tb
    return pl.pallas_call(
        _combine_kernel,
        grid_spec=pltpu.PrefetchScalarGridSpec(
            num_scalar_prefetch=1,
            grid=(t // tb,),
            in_specs=[pl.BlockSpec(memory_space=pl.ANY),
                      pl.BlockSpec((tb, LANES), lambda i, dest: (i, 0)),
                      pl.BlockSpec((tb, d), lambda i, dest: (i, 0)),
                      pl.BlockSpec((1, 1, d), lambda i, dest: (i // per_b, 0, 0)),
                      pl.BlockSpec((1, d), lambda i, dest: (0, 0))],
            out_specs=pl.BlockSpec((tb, d), lambda i, dest: (i, 0)),
            scratch_shapes=[pltpu.VMEM((2, TOP_K, tb, d), F32), pltpu.SemaphoreType.DMA((2,))],
        ),
        out_shape=jax.ShapeDtypeStruct((t, d), F32),
        compiler_params=_cp(("arbitrary",)),
        name="combine",
    )(dest_flat, ys, gates, x1, g2, fg.reshape(1, d))


def _moe(x1, g, sc2, sh2, g2, router_w, router_b, w_gu, b_gu, w_down, b_down, final_g, seq):
    t, d = x1.shape
    a = t * TOP_K
    tm = MOE_TM
    nb = -(-a // tm) + N_EXPERTS
    h2, idx_w, gate_w, rank_w, counts = _route(x1, g, sc2, sh2, router_w, router_b, seq)
    idx = idx_w[:, :TOP_K]
    rank = rank_w[:, :TOP_K]
    cnt = counts.reshape(-1).astype(jnp.int32)
    padded = ((cnt + tm - 1) // tm) * tm
    pad_end = jnp.cumsum(padded)
    pad_start = pad_end - padded
    experts = jnp.arange(N_EXPERTS, dtype=jnp.int32)
    start_of = jnp.sum(jnp.where(idx[:, :, None] == experts, pad_start, 0), axis=-1)
    dest = (start_of + rank).reshape(a).astype(jnp.int32)
    n_used = (pad_end[-1] // tm).astype(jnp.int32)
    blk = jnp.arange(nb, dtype=jnp.int32)
    blk_src = jnp.clip(blk, 0, jnp.maximum(n_used - 1, 0))
    blk_expert = jnp.sum((pad_end[None, :] <= (blk_src * tm)[:, None]).astype(jnp.int32), axis=1)
    blk_expert = jnp.minimum(blk_expert, N_EXPERTS - 1)
    last_blk = jnp.where(padded > 0, pad_end // tm - 1, -1)
    zero_flag = ((blk >= n_used) | jnp.any(last_blk[None, :] == blk[:, None], axis=1)).astype(jnp.int32)
    xs = _scatter_rows(dest, zero_flag, h2, nb * tm)
    ys = _experts(blk_expert, blk_src, n_used.reshape(1), xs, w_gu, b_gu, w_down, b_down)
    return _combine(dest, ys, gate_w, x1, g2, final_g, seq)


def kernel(x, c, ada_w, ada_b, norm1_g, norm2_g, w_in, rwkv_mu, rwkv_w0, rwkv_w2, rwkv_a0, rwkv_a2, rwkv_g2,
           rwkv_k_k, rwkv_k_a, rwkv_r_k, rwkv_gn_g, rwkv_gn_b, hgrn_lb_logits, hgrn_gn_g, proj_a, proj_b, w_out,
           router_w, router_b, exp_w_gate_up, exp_b_gate_up, exp_w_down, exp_b_down, final_norm_g):
    bsz, seq, d = x.shape
    depth = ada_w.shape[0]
    lower_bounds = jnp.cumsum(jax.nn.softmax(hgrn_lb_logits.astype(F32), axis=0), axis=0)
    xf = x.reshape(bsz * seq, d)
    rkv = 3 * RWKV_WIDTH
    rwkv_cols = rkv + LORA_COLS
    for l in range(depth):
        mod = _adaln(c, ada_w[l], ada_b[l])
        sh1, sc1, g1, sh2, sc2, g2 = [m.reshape(bsz, 1, d) for m in jnp.split(mod, 6, axis=-1)]

        wl = w_in[l]
        w_a = wl[:, :rkv + LORA_PAD].astype(BF16)
        w_b = wl[:, rwkv_cols:].astype(BF16)
        proj = _inproj(xf, norm1_g[l], sc1, sh1, w_a, w_b, seq)

        mu = rwkv_mu[l]
        mu_lo = jnp.pad(mu[rkv:], (0, LORA_PAD - LORA_COLS))
        pad_rows = lambda w, r0: jnp.zeros((LORA_PAD, RWKV_WIDTH), F32).at[r0:r0 + w.shape[0]].set(w).astype(BF16)
        ya = _rwkv(proj, bsz, seq, mu[:rkv], mu_lo, rwkv_w0[l], rwkv_a0[l], rwkv_k_k[l], rwkv_k_a[l],
                   rwkv_r_k[l].reshape(-1), rwkv_gn_g[l], rwkv_gn_b[l],
                   pad_rows(rwkv_w2[l], 0), pad_rows(rwkv_a2[l], DECAY_LORA),
                   pad_rows(rwkv_g2[l], DECAY_LORA + AAA_LORA))
        yb = _hgrn(proj, bsz, seq, lower_bounds[l], hgrn_gn_g[l])
        x1 = _mixout(ya, yb, proj, proj_a[l].astype(BF16), proj_b[l].astype(BF16), w_out[l].astype(BF16), xf, g1, seq)
        assert depth == 1
        out = _moe(x1, norm2_g[l], sc2, sh2, g2, router_w[l], router_b[l], exp_w_gate_up[l], exp_b_gate_up[l],
                   exp_w_down[l], exp_b_down[l], final_norm_g, seq)
    return out.reshape(bsz, seq, d)
```
